```python
import jax, jax.numpy as jnp
from jax import lax
import numpy as np

D_MODEL = 1024
BATCH = 1
SEQ = 16384
DEPTH = 2
DEC_BATCH = 32
DEC_SEQ = 8
PAST_LEN = 16384
PAGE_SIZE = 128

HEAD_DIM = 64
NSA_HEADS = 8
NSA_KV = 2
NSA_GROUP = NSA_HEADS // NSA_KV
CMP_LEN = 32
CMP_STRIDE = 16
SEL_LEN = 64
N_SEL = 16
WINDOW = 512
FORCE_SCORE = 1.0e4
GM_WIDTH = D_MODEL // 2
GM_GROUPS = 8
CHUNK = 128
CV_WIDTH = D_MODEL // 2
CV_K = 31
SB_HEADS = 8
MEM_LEN = 256
MEM_HEADS = 4
MEM_HD = D_MODEL // MEM_HEADS
D_FF = 2816
FFN_K = 3
Q_BLOCK = 128
ROPE_THETA = 10000.0
EPS = 1e-6
TINY = 1e-30
N_EVEN = (DEPTH + 1) // 2
N_ODD = DEPTH // 2
NSA_Q_COLS = NSA_HEADS * HEAD_DIM
NSA_KV_COLS = 2 * NSA_KV * HEAD_DIM
NSA_GATE_COLS = 3 * NSA_HEADS
E_SPLITS = (NSA_Q_COLS, NSA_Q_COLS + NSA_KV_COLS, NSA_Q_COLS + 2 * NSA_KV_COLS, NSA_Q_COLS + 3 * NSA_KV_COLS, NSA_Q_COLS + 3 * NSA_KV_COLS + NSA_GATE_COLS)
IN_E = NSA_Q_COLS + 3 * NSA_KV_COLS + NSA_GATE_COLS + 2 * GM_WIDTH
OUT_E = NSA_Q_COLS + GM_WIDTH
SB_WIDTH = SB_HEADS * HEAD_DIM
O_SPLITS = (CV_WIDTH, 2 * CV_WIDTH, 2 * CV_WIDTH + SB_WIDTH, 2 * CV_WIDTH + 2 * SB_WIDTH)
IN_O = 2 * CV_WIDTH + 3 * SB_WIDTH
OUT_O = CV_WIDTH + SB_WIDTH
SEL_STEP = SEL_LEN // CMP_STRIDE
R_LO = 1 - CMP_LEN // CMP_STRIDE
R_HI = SEL_LEN // CMP_STRIDE - 1

kernel_name = 'hybrid_nsa_gmlp_conformer_stickbreak_step'

F32 = jnp.float32


def rmsnorm(x, g):
    x32 = x.astype(F32)
    y = x32 * lax.rsqrt(jnp.mean(x32 * x32, -1, keepdims=True) + EPS)
    return (y * g.astype(F32)).astype(x.dtype)


def layernorm(x, g, b):
    x32 = x.astype(F32)
    xc = x32 - jnp.mean(x32, -1, keepdims=True)
    y = xc * lax.rsqrt(jnp.mean(xc * xc, -1, keepdims=True) + EPS)
    return (y * g.astype(F32) + b.astype(F32)).astype(x.dtype)


def rope(x, pos):
    half = x.shape[-1] // 2
    inv = jnp.power(ROPE_THETA, -jnp.arange(half, dtype=F32) / half)
    ang = pos.astype(F32)[:, None] * inv[None, :]
    cos = jnp.cos(ang)[:, None, :]
    sin = jnp.sin(ang)[:, None, :]
    x32 = x.astype(F32)
    x1, x2 = x32[..., :half], x32[..., half:]
    return jnp.concatenate([x1 * cos - x2 * sin, x2 * cos + x1 * sin], -1).astype(x.dtype)


def masked_softmax(s, mask):
    s = jnp.where(mask, s, -jnp.inf)
    m = jnp.max(s, -1, keepdims=True)
    m = jnp.where(jnp.isfinite(m), m, 0.0)
    p = jnp.where(mask, jnp.exp(s - m), 0.0)
    return p / jnp.maximum(jnp.sum(p, -1, keepdims=True), TINY)


def causal_dwconv(full, w, b):
    c = full.shape[-1]
    out = lax.conv_general_dilated(full, w[:, None, :].astype(full.dtype), window_strides=(1,), padding='VALID', dimension_numbers=('NWC', 'WIO', 'NWC'), feature_group_count=c)
    return out + b.astype(full.dtype)


def even_inproj(h, pos, w_in, g_qk, gm_ln):
    b, t, _ = h.shape
    z = h @ w_in
    q, kc, ks, kw, gt, uv = jnp.split(z, E_SPLITS, axis=-1)
    q = rmsnorm(q.reshape(b, t, NSA_HEADS, HEAD_DIM), g_qk[0])
    q_rope = rope(q, pos)

    def kv(a, g, use_rope):
        a = a.reshape(b, t, 2, NSA_KV, HEAD_DIM)
        k = rmsnorm(a[:, :, 0], g)
        if use_rope:
            k = rope(k, pos)
        return k, a[:, :, 1]

    kc_k, kc_v = kv(kc, g_qk[1], False)
    ks_k, ks_v = kv(ks, g_qk[2], True)
    kw_k, kw_v = kv(kw, g_qk[3], True)
    rows = jnp.stack([kc_k, kc_v, ks_k, ks_v], axis=2)
    win_rows = jnp.stack([kw_k, kw_v], axis=2)
    gates = jax.nn.sigmoid(gt.astype(F32)).reshape(b, t, NSA_HEADS, 3)
    u, v = jnp.split(jax.nn.gelu(uv), 2, axis=-1)
    v = layernorm(v, gm_ln[0], gm_ln[1])
    return q, q_rope, gates, rows, win_rows, u, v


def compress(k_raw, v_raw, cmp_pe, w_cmp):
    n_tok = k_raw.shape[0]
    nc = (n_tok - CMP_LEN) // CMP_STRIDE + 1
    idx = jnp.arange(nc)[:, None] * CMP_STRIDE + jnp.arange(CMP_LEN)[None, :]

    def one(x, pe, w):
        blk = x[idx] + pe[None, :, None, :].astype(x.dtype)
        return jnp.einsum('nlkd,lde->nke', blk, w)

    return one(k_raw, cmp_pe[0], w_cmp[0]), one(v_raw, cmp_pe[1], w_cmp[1]), idx[:, -1]


def nsa_block(q_plain, q_rope, gates, q_pos, kc, vc, c_end, sel_k, sel_v, win_k, win_v, win_pos):
    t = q_plain.shape[0]
    n_tok = sel_k.shape[0]
    nc = kc.shape[0]
    scale = HEAD_DIM ** -0.5
    qp = q_plain.reshape(t, NSA_KV, NSA_GROUP, HEAD_DIM).astype(F32)
    qr = q_rope.reshape(t, NSA_KV, NSA_GROUP, HEAD_DIM).astype(F32)
    s_c = jnp.einsum('tkgd,nkd->tkgn', qp, kc.astype(F32)) * scale
    m_c = (c_end[None, :] <= q_pos[:, None])[:, None, None, :]
    p_c = masked_softmax(s_c, m_c)
    o_c = jnp.einsum('tkgn,nkd->tkgd', p_c, vc.astype(F32))
    p_grp = jnp.sum(p_c, axis=2)
    ns = -(-n_tok // SEL_LEN)
    midx = jnp.arange(ns)[:, None] * SEL_STEP + jnp.arange(R_LO, R_HI + 1)[None, :]
    mval = (midx >= 0) & (midx < nc)
    p_slc = jnp.sum(jnp.where(mval, p_grp[..., jnp.clip(midx, 0, nc - 1)], 0.0), -1)
    blk = jnp.arange(ns)
    forced = (blk[None, :] == (q_pos // SEL_LEN)[:, None]) | (blk[None, :] == 0)
    causal_ok = (blk[None, :] * SEL_LEN) <= q_pos[:, None]
    score = jnp.where(causal_ok[:, None, :], jnp.where(forced[:, None, :], FORCE_SCORE, p_slc), -1.0)
    k_eff = min(N_SEL, ns)
    top_v, top_i = lax.top_k(score, k_eff)
    tok = (top_i[..., None] * SEL_LEN + jnp.arange(SEL_LEN)).reshape(t, NSA_KV, k_eff * SEL_LEN)
    tok_ok = jnp.repeat(top_v >= 0.0, SEL_LEN, axis=-1) & (tok <= q_pos[:, None, None])
    tok_c = jnp.minimum(tok, n_tok - 1)
    kv_idx = jnp.arange(NSA_KV)[None, :, None]
    ks_g = sel_k[tok_c, kv_idx].astype(F32)
    vs_g = sel_v[tok_c, kv_idx].astype(F32)
    s_s = jnp.einsum('tkgd,tksd->tkgs', qr, ks_g) * scale
    p_s = masked_softmax(s_s, tok_ok[:, :, None, :])
    o_s = jnp.einsum('tkgs,tksd->tkgd', p_s, vs_g)
    w_ok = (win_pos[None, :] <= q_pos[:, None]) & (win_pos[None, :] > q_pos[:, None] - WINDOW) & (win_pos[None, :] >= 0)
    s_w = jnp.einsum('tkgd,skd->tkgs', qr, win_k.astype(F32)) * scale
    p_w = masked_softmax(s_w, w_ok[:, None, None, :])
    o_w = jnp.einsum('tkgs,skd->tkgd', p_w, win_v.astype(F32))
    g = gates.reshape(t, NSA_KV, NSA_GROUP, 3)
    o = g[..., 0:1] * o_c + g[..., 1:2] * o_s + g[..., 2:3] * o_w
    return o.reshape(t, NSA_HEADS * HEAD_DIM).astype(q_plain.dtype)


def nsa_prompt_seq(q_plain, q_rope, gates, rows, win_rows, cmp_pe, w_cmp):
    s_len = q_plain.shape[0]
    kc, vc, c_end = compress(rows[:, 0], rows[:, 1], cmp_pe, w_cmp)
    win_pad = jnp.pad(win_rows, ((WINDOW, 0), (0, 0), (0, 0), (0, 0)))
    nb = s_len // Q_BLOCK

    def blk(a):
        qa, qb, ga, t0 = a
        q_pos = t0 + jnp.arange(Q_BLOCK)
        w = lax.dynamic_slice_in_dim(win_pad, t0, WINDOW + Q_BLOCK, 0)
        w_pos = t0 - WINDOW + jnp.arange(WINDOW + Q_BLOCK)
        return nsa_block(qa, qb, ga, q_pos, kc, vc, c_end, rows[:, 2], rows[:, 3], w[:, 0], w[:, 1], w_pos)

    out = lax.map(blk, (q_plain.reshape(nb, Q_BLOCK, NSA_HEADS, HEAD_DIM), q_rope.reshape(nb, Q_BLOCK, NSA_HEADS, HEAD_DIM), gates.reshape(nb, Q_BLOCK, NSA_HEADS, 3), jnp.arange(nb) * Q_BLOCK))
    return out.reshape(s_len, NSA_HEADS * HEAD_DIM)


def nsa_sample_seq(pt_row, q_plain, q_rope, gates, rows_new, win_buf, win_new, pool, cmp_pe, w_cmp):
    past = pool[pt_row].reshape(-1, 4, NSA_KV, HEAD_DIM)
    n_past = past.shape[0]
    allr = jnp.concatenate([past, rows_new], 0)
    q_pos = n_past + jnp.arange(q_plain.shape[0])
    kc, vc, c_end = compress(allr[:, 0], allr[:, 1], cmp_pe, w_cmp)
    wk = jnp.concatenate([win_buf, win_new], 0)
    w_pos = n_past - win_buf.shape[0] + jnp.arange(wk.shape[0])
    return nsa_block(q_plain, q_rope, gates, q_pos, kc, vc, c_end, allr[:, 2], allr[:, 3], wk[:, 0], wk[:, 1], w_pos)


def spatial_gate(u, v, ws, sb):
    b, t, c = v.shape
    nch = -(-t // CHUNK)
    vp = jnp.pad(v, ((0, 0), (0, nch * CHUNK - t), (0, 0))).reshape(b, nch, CHUNK, GM_GROUPS, c // GM_GROUPS)
    mixed = jnp.einsum('gts,bnsgc->bntgc', jnp.tril(ws).astype(v.dtype), vp) + sb.T[None, None, :, :, None].astype(v.dtype)
    return u * mixed.reshape(b, nch * CHUNK, c)[:, :t]


def odd_inproj(h, w_in):
    b, t, _ = h.shape
    a, g, q, k, v = jnp.split(h @ w_in, O_SPLITS, axis=-1)
    glu = a * jax.nn.sigmoid(g)
    q = q.reshape(b, t, SB_HEADS, HEAD_DIM)
    rows = jnp.stack([k.reshape(b, t, SB_HEADS, HEAD_DIM), v.reshape(b, t, SB_HEADS, HEAD_DIM)], axis=2)
    return glu, q, rows


def conv_module(glu, hist, dw, db, ln):
    full = jnp.concatenate([hist.astype(glu.dtype), glu], 1)
    c = layernorm(causal_dwconv(full, dw, db), ln[0], ln[1])
    return jax.nn.silu(c), full[:, -(CV_K - 1):]


def sb_block(q, q_pos, k, v):
    t = q.shape[0]
    n_tok = k.shape[0]
    z = jnp.einsum('thd,shd->hts', q.astype(F32), k.astype(F32)) * (HEAD_DIM ** -0.5)
    mask = (jnp.arange(n_tok)[None, :] < q_pos[:, None])[None]
    log_b = jax.nn.log_sigmoid(z)
    log_1m = jnp.where(mask, log_b - z, 0.0)
    after = lax.cumsum(log_1m, axis=2, reverse=True) - log_1m
    a = jnp.where(mask, jnp.exp(log_b + after), 0.0)
    return jnp.einsum('hts,shd->thd', a, v.astype(F32)).reshape(t, SB_WIDTH).astype(q.dtype)


def sb_prompt_seq(q, rows):
    s_len = q.shape[0]
    nb = s_len // Q_BLOCK
    k, v = rows[:, 0], rows[:, 1]

    def blk(a):
        qb, t0 = a
        return sb_block(qb, t0 + jnp.arange(Q_BLOCK), k, v)

    out = lax.map(blk, (q.reshape(nb, Q_BLOCK, SB_HEADS, HEAD_DIM), jnp.arange(nb) * Q_BLOCK))
    return out.reshape(s_len, SB_WIDTH)


def sb_sample_seq(pt_row, q, rows_new, pool):
    past = pool[pt_row].reshape(-1, 2, SB_HEADS, HEAD_DIM)
    allr = jnp.concatenate([past, rows_new], 0)
    return sb_block(q, past.shape[0] + jnp.arange(q.shape[0]), allr[:, 0], allr[:, 1])


def mem_kv(mem, g_mem, w_xkv, g_xk):
    b, m, _ = mem.shape
    kv = (rmsnorm(mem, g_mem) @ w_xkv).reshape(b, m, 2, MEM_HEADS, MEM_HD)
    return jnp.stack([rmsnorm(kv[:, :, 0], g_xk), kv[:, :, 1]], axis=2)


def cross_attn(h, mkv, w_xq, g_xq, w_xo):
    b, t, _ = h.shape
    q = rmsnorm((h @ w_xq).reshape(b, t, MEM_HEADS, MEM_HD), g_xq)
    s = jnp.einsum('bthd,bmhd->bhtm', q.astype(F32), mkv[:, :, 0].astype(F32)) * (MEM_HD ** -0.5)
    p = jax.nn.softmax(s, axis=-1)
    o = jnp.einsum('bhtm,bmhd->bthd', p, mkv[:, :, 1].astype(F32))
    return o.reshape(b, t, MEM_HEADS * MEM_HD).astype(h.dtype) @ w_xo


def conv_ffn(h, hist, w_up, dw, db, w_down):
    up = h @ w_up
    full = jnp.concatenate([hist.astype(up.dtype), up], 1)
    a, g = jnp.split(causal_dwconv(full, dw, db), 2, axis=-1)
    return (jax.nn.silu(g) * a) @ w_down, full[:, -(FFN_K - 1):]


def setup_inputs(seed: int = 0) -> dict:
    key = jax.random.key(seed)
    ks = iter(jax.random.split(key, 48))

    def nrm(shape, scale):
        return jax.random.normal(next(ks), shape, F32) * scale

    n_pages = PAST_LEN // PAGE_SIZE
    n_used = DEC_BATCH * n_pages
    n_phys = n_used + (n_used + 3) // 4
    wb = min(WINDOW, PAST_LEN)
    perm = jax.random.permutation(next(ks), n_phys)
    page_table = perm[:n_used].reshape(DEC_BATCH, n_pages).astype(jnp.int32)
    gain_bias = jnp.array([1.0, 0.0], F32)[None, :, None]
    return dict(
        x_prompt=nrm((BATCH, SEQ, D_MODEL), 1.0),
        x_sample=nrm((DEC_BATCH, DEC_SEQ, D_MODEL), 1.0),
        mem_prompt=nrm((BATCH, MEM_LEN, D_MODEL), 1.0),
        page_table=page_table,
        cache_nsa=nrm((N_EVEN, n_phys, PAGE_SIZE, 4, NSA_KV, HEAD_DIM), 1.0),
        cache_nsa_win=nrm((N_EVEN, DEC_BATCH, wb, 2, NSA_KV, HEAD_DIM), 1.0),
        cache_sb=nrm((N_ODD, n_phys, PAGE_SIZE, 2, SB_HEADS, HEAD_DIM), 1.0),
        state_conv=nrm((N_ODD, DEC_BATCH, CV_K - 1, CV_WIDTH), 0.5),
        state_ffn=nrm((DEPTH, DEC_BATCH, FFN_K - 1, 2 * D_FF), 1.0),
        cache_mem=nrm((DEPTH, DEC_BATCH, MEM_LEN, 2, MEM_HEADS, MEM_HD), 1.0),
        g_norm=1.0 + nrm((DEPTH, 4, D_MODEL), 0.01),
        w_in_e=nrm((N_EVEN, D_MODEL, IN_E), D_MODEL ** -0.5),
        g_qk_nsa=1.0 + nrm((N_EVEN, 4, HEAD_DIM), 0.01),
        cmp_pe=nrm((N_EVEN, 2, CMP_LEN, HEAD_DIM), 0.1),
        w_cmp=nrm((N_EVEN, 2, CMP_LEN, HEAD_DIM, HEAD_DIM), (CMP_LEN * HEAD_DIM) ** -0.5),
        gm_ws=nrm((N_EVEN, GM_GROUPS, CHUNK, CHUNK), CHUNK ** -0.5),
        gm_b=1.0 + nrm((N_EVEN, GM_GROUPS, CHUNK), 0.01),
        gm_ln=gain_bias + nrm((N_EVEN, 2, GM_WIDTH), 0.01),
        w_out_e=nrm((N_EVEN, OUT_E, D_MODEL), OUT_E ** -0.5),
        w_in_o=nrm((N_ODD, D_MODEL, IN_O), D_MODEL ** -0.5),
        cv_dw=nrm((N_ODD, CV_K, CV_WIDTH), CV_K ** -0.5),
        cv_b=nrm((N_ODD, CV_WIDTH), 0.01),
        cv_ln=gain_bias + nrm((N_ODD, 2, CV_WIDTH), 0.01),
        w_out_o=nrm((N_ODD, OUT_O, D_MODEL), OUT_O ** -0.5),
        w_xq=nrm((DEPTH, D_MODEL, MEM_HEADS * MEM_HD), D_MODEL ** -0.5),
        w_xkv=nrm((DEPTH, D_MODEL, 2 * MEM_HEADS * MEM_HD), D_MODEL ** -0.5),
        g_xqk=1.0 + nrm((DEPTH, 2, MEM_HD), 0.01),
        w_xo=nrm((DEPTH, MEM_HEADS * MEM_HD, D_MODEL), (MEM_HEADS * MEM_HD) ** -0.5),
        w_up=nrm((DEPTH, D_MODEL, 2 * D_FF), D_MODEL ** -0.5),
        ffn_dw=nrm((DEPTH, FFN_K, 2 * D_FF), FFN_K ** -0.5),
        ffn_db=nrm((DEPTH, 2 * D_FF), 0.01),
        w_down=nrm((DEPTH, D_FF, D_MODEL), D_FF ** -0.5),
    )


def reference(x_prompt, x_sample, mem_prompt, page_table, cache_nsa, cache_nsa_win, cache_sb, state_conv, state_ffn, cache_mem, g_norm, w_in_e, g_qk_nsa, cmp_pe, w_cmp, gm_ws, gm_b, gm_ln, w_out_e, w_in_o, cv_dw, cv_b, cv_ln, w_out_o, w_xq, w_xkv, g_xqk, w_xo, w_up, ffn_dw, ffn_db, w_down):
    bp, sp, _ = x_prompt.shape
    ts = x_sample.shape[1]
    past = page_table.shape[1] * PAGE_SIZE
    wbuf = cache_nsa_win.shape[2]
    pos_p = jnp.arange(sp)
    pos_s = past + jnp.arange(ts)
    xp, xs = x_prompt, x_sample
    nsa_rows_p, nsa_rows_s, nsa_win_p, nsa_win_s = [], [], [], []
    gm_v_p, gm_v_s, sb_rows_p, sb_rows_s = [], [], [], []
    conv_p, conv_s, ffn_p, ffn_s, memkv_p = [], [], [], [], []
    for l in range(DEPTH):
        i = l // 2
        hp = rmsnorm(xp, g_norm[l, 0])
        hs = rmsnorm(xs, g_norm[l, 0])
        if l % 2 == 0:
            pe_i, wc_i, pool = cmp_pe[i], w_cmp[i], cache_nsa[i]
            qa, qb, ga, ra, wa, ua, va = even_inproj(hp, pos_p, w_in_e[i], g_qk_nsa[i], gm_ln[i])
            o_nsa = lax.map(lambda a: nsa_prompt_seq(*a, pe_i, wc_i), (qa, qb, ga, ra, wa))
            o_gm = spatial_gate(ua, va, gm_ws[i], gm_b[i])
            xp = xp + jnp.concatenate([o_nsa, o_gm], -1) @ w_out_e[i]
            nsa_rows_p.append(ra)
            nsa_win_p.append(wa[:, -min(WINDOW, sp):])
            gm_v_p.append(va[:, ((sp - 1) // CHUNK) * CHUNK:])
            qa, qb, ga, ra, wa, ua, va = even_inproj(hs, pos_s, w_in_e[i], g_qk_nsa[i], gm_ln[i])
            o_nsa = lax.map(lambda a: nsa_sample_seq(*a, pool, pe_i, wc_i), (page_table, qa, qb, ga, ra, cache_nsa_win[i], wa))
            o_gm = spatial_gate(ua, va, gm_ws[i], gm_b[i])
            xs = xs + jnp.concatenate([o_nsa, o_gm], -1) @ w_out_e[i]
            nsa_rows_s.append(ra)
            nsa_win_s.append(jnp.concatenate([cache_nsa_win[i], wa], 1)[:, -wbuf:])
            gm_v_s.append(va)
        else:
            pool = cache_sb[i]
            glu, q, rows = odd_inproj(hp, w_in_o[i])
            o_cv, cst = conv_module(glu, jnp.zeros((bp, CV_K - 1, CV_WIDTH), glu.dtype), cv_dw[i], cv_b[i], cv_ln[i])
            o_sb = lax.map(lambda a: sb_prompt_seq(*a), (q, rows))
            xp = xp + jnp.concatenate([o_cv, o_sb], -1) @ w_out_o[i]
            conv_p.append(cst)
            sb_rows_p.append(rows)
            glu, q, rows = odd_inproj(hs, w_in_o[i])
            o_cv, cst = conv_module(glu, state_conv[i], cv_dw[i], cv_b[i], cv_ln[i])
            o_sb = lax.map(lambda a: sb_sample_seq(*a, pool), (page_table, q, rows))
            xs = xs + jnp.concatenate([o_cv, o_sb], -1) @ w_out_o[i]
            conv_s.append(cst)
            sb_rows_s.append(rows)
        mkv = mem_kv(mem_prompt, g_norm[l, 3], w_xkv[l], g_xqk[l, 1])
        memkv_p.append(mkv)
        xp = xp + cross_attn(rmsnorm(xp, g_norm[l, 1]), mkv, w_xq[l], g_xqk[l, 0], w_xo[l])
        xs = xs + cross_attn(rmsnorm(xs, g_norm[l, 1]), cache_mem[l], w_xq[l], g_xqk[l, 0], w_xo[l])
        yp, st = conv_ffn(rmsnorm(xp, g_norm[l, 2]), jnp.zeros((bp, FFN_K - 1, 2 * D_FF), xp.dtype), w_up[l], ffn_dw[l], ffn_db[l], w_down[l])
        xp = xp + yp
        ffn_p.append(st)
        ys, st = conv_ffn(rmsnorm(xs, g_norm[l, 2]), state_ffn[l], w_up[l], ffn_dw[l], ffn_db[l], w_down[l])
        xs = xs + ys
        ffn_s.append(st)
    return (xp, xs, jnp.stack(nsa_rows_p), jnp.stack(nsa_rows_s), jnp.stack(nsa_win_p), jnp.stack(nsa_win_s), jnp.stack(gm_v_p), jnp.stack(gm_v_s), jnp.stack(sb_rows_p), jnp.stack(sb_rows_s), jnp.stack(conv_p), jnp.stack(conv_s), jnp.stack(ffn_p), jnp.stack(ffn_s), jnp.stack(memkv_p))
```

```python
import functools

import numpy as np
import jax
import jax.numpy as jnp
from jax import lax
from jax.experimental import pallas as pl
from jax.experimental.pallas import tpu as pltpu

F32 = jnp.float32
BF16 = jnp.bfloat16

D_MODEL = 1024
HEAD_DIM = 64
NSA_HEADS = 8
NSA_KV = 2
CMP_LEN = 32
CMP_STRIDE = 16
SEL_LEN = 64
N_SEL = 16
WINDOW = 512
FORCE_SCORE = 1.0e4
GM_WIDTH = 512
GM_GROUPS = 8
CHUNK = 128
CV_WIDTH = 512
CV_K = 31
SB_HEADS = 8
SB_WIDTH = 512
MEM_HEADS = 4
MEM_HD = 256
D_FF = 2816
FFN_K = 3
PAGE = 128
ROPE_THETA = 10000.0
EPS = 1e-6
TINY = 1e-30
IN_E_PAD = 2432
NEG = -1e30
SB_EXIT = -120.0

LANES = 128
V7X_VMEM_LIMIT = 56 * 1024 * 1024


def _params(sem):
    return pltpu.CompilerParams(dimension_semantics=sem, vmem_limit_bytes=V7X_VMEM_LIMIT)


def _const_spec(shape):
    nd = len(shape)
    return pl.BlockSpec(shape, lambda *_: (0,) * nd)


def _dot(a, b):
    return jnp.dot(a, b, preferred_element_type=F32)


def _dot_nt(a, b):
    return lax.dot_general(a, b, (((1,), (1,)), ((), ())), preferred_element_type=F32)


def _split_bf16(x):
    hi = x.astype(BF16)
    lo = (x - hi.astype(F32)).astype(BF16)
    return hi, lo


def _dot_hilo(x, w):
    hi, lo = _split_bf16(x)
    return _dot(hi, w) + _dot(lo, w)


def _rms_rows(x, g):
    return x * lax.rsqrt(jnp.mean(x * x, axis=-1, keepdims=True) + EPS) * g


def _seg_rms(x, ones_bd, g, width):
    ss = _dot((x * x).astype(BF16), ones_bd)
    return x * lax.rsqrt(ss * (1.0 / width) + EPS) * g


def _sigmoid(x):
    return 1.0 / (1.0 + jnp.exp(-x))


def _gelu_tanh(x):
    return 0.5 * x * (1.0 + jnp.tanh(0.7978845608028654 * (x + 0.044715 * (x * x * x))))


def _rope_blk(x, cos_t, sin_t, lo32):
    sw = jnp.where(lo32, pltpu.roll(x, 96, 1), pltpu.roll(x, 32, 1))
    return x * cos_t + sw * sin_t


def _inproj_even_kernel(x_ref, g_ref, w_ref, cos_ref, sin_ref, gq_ref, gk_ref, ln_ref, ones_ref,
                        wsg_ref, sbt_ref,
                        qp_ref, qr_ref, rows_ref, win_ref, gates_ref, v_ref, ogm_ref):
    x = x_ref[...]
    tm = x.shape[0]
    h = _rms_rows(x, g_ref[...]).astype(BF16)
    z = _dot(h, w_ref[...])
    lane = lax.broadcasted_iota(jnp.int32, (tm, LANES), 1)
    lo32 = (lane & 32) == 0
    lo64 = lane < 64
    cos_t = cos_ref[...]
    sin_t = sin_ref[...]
    ones128 = ones_ref[0:LANES, 0:LANES]
    gk = gk_ref[...]

    q = _seg_rms(z[:, 0:512], ones_ref[...], gq_ref[...], HEAD_DIM)
    qp_ref[...] = q
    for i in range(4):
        sl = slice(i * LANES, (i + 1) * LANES)
        qr_ref[:, sl] = _rope_blk(q[:, sl], cos_t, sin_t, lo32)

    rows_ref[:, 0:128] = _seg_rms(z[:, 512:640], ones128, gk[0:1], HEAD_DIM)
    rows_ref[:, 128:256] = z[:, 640:768]
    ks_k = _seg_rms(z[:, 768:896], ones128, gk[1:2], HEAD_DIM)
    rows_ref[:, 256:384] = _rope_blk(ks_k, cos_t, sin_t, lo32)
    rows_ref[:, 384:512] = z[:, 896:1024]
    kw_k = _seg_rms(z[:, 1024:1152], ones128, gk[2:3], HEAD_DIM)
    win_ref[:, 0:128] = _rope_blk(kw_k, cos_t, sin_t, lo32)
    win_ref[:, 128:256] = z[:, 1152:1280]

    gates_ref[...] = _sigmoid(z[:, 1280:1408])

    u = _gelu_tanh(z[:, 1408:1920])
    vv = _gelu_tanh(z[:, 1920:2432])
    ln = ln_ref[...]
    vc = vv - jnp.mean(vv, axis=-1, keepdims=True)
    v = vc * lax.rsqrt(jnp.mean(vc * vc, axis=-1, keepdims=True) + EPS) * ln[0:1] + ln[1:2]
    v_ref[...] = v
    vb = v.astype(BF16)
    for i in range(4):
        sl = slice(i * LANES, (i + 1) * LANES)
        m_lo = _dot(wsg_ref[2 * i], vb[:, sl])
        m_hi = _dot(wsg_ref[2 * i + 1], vb[:, sl])
        mixed = jnp.where(lo64, m_lo, m_hi) + sbt_ref[:, sl]
        ogm_ref[:, sl] = u[:, sl] * mixed


def _inproj_even(x, g, w, cos_t, sin_t, gq, gk, ln, ones_bd, wsg, sbt, tm):
    t = x.shape[0]
    row = lambda n: pl.BlockSpec((tm, n), lambda i: (i, 0))
    outs = [(512, F32), (512, F32), (512, F32), (256, F32), (128, F32), (512, F32), (512, F32)]
    return pl.pallas_call(
        _inproj_even_kernel,
        grid=(t // tm,),
        in_specs=[row(D_MODEL), _const_spec((1, D_MODEL)), _const_spec(w.shape), row(128), row(128),
                  _const_spec(gq.shape), _const_spec(gk.shape), _const_spec(ln.shape),
                  _const_spec(ones_bd.shape), _const_spec(wsg.shape), _const_spec(sbt.shape)],
        out_specs=[row(n) for n, _ in outs],
        out_shape=[jax.ShapeDtypeStruct((t, n), d) for n, d in outs],
        compiler_params=_params(("arbitrary",)),
        name="inproj_even",
    )(x, g, w, cos_t, sin_t, gq, gk, ln, ones_bd, wsg, sbt)


def _compress_kernel(pt_ref, *refs, npg, nsteps, nch):
    page_refs = refs[:npg]
    perm_ref, wl_ref, pe_ref, out_ref, xl_ref, a2_ref = refs[npg:]
    k = pl.program_id(1)

    @pl.when(k == 0)
    def _():
        for l in range(16):
            xl_ref[l, nch:nch + 16, :] = pe_ref[l]

    perm = perm_ref[...]
    for pp in range(npg // 2):
        pg = jnp.concatenate([page_refs[2 * pp][...], page_refs[2 * pp + 1][...]], axis=0).astype(BF16)
        xp = _dot(perm, pg).astype(BF16)
        row0 = pl.multiple_of((k * (npg // 2) + pp) * 16, 16)
        for l in range(16):
            xl_ref[l, pl.ds(row0, 16), :] = xp[l * 16:(l + 1) * 16]

    @pl.when(k == nsteps - 1)
    def _():
        acc = _dot(xl_ref[0], wl_ref[0])
        for l in range(1, 16):
            acc = acc + _dot(xl_ref[l], wl_ref[l])
        bias = (acc[nch:nch + 1, 0:256] + acc[nch + 2:nch + 3, 0:256]
                + acc[nch + 1:nch + 2, 256:512] + acc[nch + 3:nch + 4, 256:512])
        a2_ref[0:nch, :] = acc[0:nch, 256:512]
        a2_ref[nch:nch + 8, :] = jnp.zeros((8, 256), F32)
        out_ref[...] = acc[0:nch, 0:256] + a2_ref[1:nch + 1, :] + bias


def _compress(pool3, page_table, perm, wl, pe_rows, npg=8):
    nseq, npages = page_table.shape
    nsteps = npages // npg
    nch = npages * (PAGE // CMP_STRIDE)

    def page_spec(p):
        return pl.BlockSpec((None, PAGE, 256), lambda s, k, pt: (pt[s, k * npg + p], 0, 0))

    grid_spec = pltpu.PrefetchScalarGridSpec(
        num_scalar_prefetch=1,
        grid=(nseq, nsteps),
        in_specs=[page_spec(p) for p in range(npg)] + [
            pl.BlockSpec(perm.shape, lambda s, k, pt: (0, 0)),
            pl.BlockSpec(wl.shape, lambda s, k, pt: (0, 0, 0)),
            pl.BlockSpec(pe_rows.shape, lambda s, k, pt: (0, 0, 0))],
        out_specs=pl.BlockSpec((None, nch, 256), lambda s, k, pt: (s, 0, 0)),
        scratch_shapes=[pltpu.VMEM((16, nch + 16, 256), BF16), pltpu.VMEM((nch + 8, 256), F32)],
    )
    return pl.pallas_call(
        functools.partial(_compress_kernel, npg=npg, nsteps=nsteps, nch=nch),
        grid_spec=grid_spec,
        out_shape=jax.ShapeDtypeStruct((nseq, nch, 256), F32),
        compiler_params=_params(("arbitrary", "arbitrary")),
        name="nsa_compress",
    )(page_table, *([pool3] * npg), perm, wl, pe_rows)


def _stack_q(q, lo64):
    blks = [q[:, i * LANES:(i + 1) * LANES] for i in range(4)]
    slabs = [jnp.where(lo64, b, 0.0) for b in blks] + [jnp.where(lo64, 0.0, b) for b in blks]
    return jnp.concatenate(slabs, axis=0).astype(BF16)


def _unstack_o(o, tq, lo64):
    return [jnp.where(lo64, o[i * tq:(i + 1) * tq], o[(4 + i) * tq:(5 + i) * tq]) for i in range(4)]


def _gate_expand(gates, gexp):
    return _dot_hilo(gates, gexp)


def _masked_softmax_rows(s, mask):
    s = jnp.where(mask, s, NEG)
    m = jnp.max(s, axis=-1, keepdims=True)
    p = jnp.where(mask, jnp.exp(s - m), 0.0)
    return p / jnp.maximum(jnp.sum(p, axis=-1, keepdims=True), TINY)


def _cmp_attn_kernel(qp_ref, cmp_ref, gates_ref, msel_ref, gexp_ref, oc_ref, sel_ref, *,
                     tq, pos0, nc, ns, nsp, nch, k_eff):
    qb = pl.program_id(1)
    lane = lax.broadcasted_iota(jnp.int32, (tq, LANES), 1)
    lo64 = lane < 64
    qs = _stack_q(qp_ref[...], lo64)
    cm = cmp_ref[...]
    kc = cm[:, 0:128].astype(BF16)
    vc = cm[:, 128:256].astype(BF16)
    s_all = _dot_nt(qs, kc) * (HEAD_DIM ** -0.5)
    qpos = pos0 + qb * tq + lax.broadcasted_iota(jnp.int32, (tq, 1), 0)
    nidx = lax.broadcasted_iota(jnp.int32, (tq, nch), 1)
    cmask = (nidx * CMP_STRIDE + (CMP_LEN - 1) <= qpos) & (nidx < nc)

    msel = msel_ref[...]
    jidx = lax.broadcasted_iota(jnp.int32, (tq, nsp), 1)
    jf = jidx.astype(F32)
    causal_ok = jidx * SEL_LEN <= qpos
    forced = (jidx == jnp.right_shift(qpos, 6)) | (jidx == 0)
    o_slabs = []
    for kv in range(NSA_KV):
        pg = jnp.zeros((tq, nch), F32)
        for g in range(4):
            r0 = (kv * 4 + g) * tq
            p = _masked_softmax_rows(s_all[r0:r0 + tq], cmask)
            pg = pg + p
            o_slabs.append(_dot(p.astype(BF16), vc))
        p_slc = _dot_hilo(pg, msel)
        score = jnp.where(causal_ok, jnp.where(forced, FORCE_SCORE, p_slc), -1.0)
        score = jnp.where(jidx < ns, score, -3.0e38)
        work = score
        sel = jnp.zeros((tq, nsp), F32)
        for _ in range(k_eff):
            m = jnp.max(work, axis=-1, keepdims=True)
            idx = jnp.min(jnp.where(work == m, jf, 1.0e9), axis=-1, keepdims=True)
            hit = jf == idx
            sel = jnp.where(hit, 1.0, sel)
            work = jnp.where(hit, -jnp.inf, work)
        sel_ref[:, kv * nsp:(kv + 1) * nsp] = jnp.where(score >= 0.0, sel, 0.0)
    o = jnp.concatenate(o_slabs, axis=0)
    gx = _gate_expand(gates_ref[...], gexp_ref[...])
    for i, ob in enumerate(_unstack_o(o, tq, lo64)):
        sl = slice(i * LANES, (i + 1) * LANES)
        oc_ref[:, sl] = gx[:, sl] * ob


def _cmp_attn(qp, cmp, gates, msel, gexp, *, nseq, tq, pos0, nc, ns, k_eff):
    t = qp.shape[0]
    nqb = t // (nseq * tq)
    nch = cmp.shape[1]
    nsp = msel.shape[1]
    row = lambda n: pl.BlockSpec((tq, n), lambda s, b: (s * nqb + b, 0))
    return pl.pallas_call(
        functools.partial(_cmp_attn_kernel, tq=tq, pos0=pos0, nc=nc, ns=ns, nsp=nsp, nch=nch, k_eff=k_eff),
        grid=(nseq, nqb),
        in_specs=[row(512), pl.BlockSpec((None, nch, 256), lambda s, b: (s, 0, 0)), row(128),
                  _const_spec(msel.shape), _const_spec(gexp.shape)],
        out_specs=[row(512), row(2 * nsp)],
        out_shape=[jax.ShapeDtypeStruct((t, 512), F32), jax.ShapeDtypeStruct((t, 2 * nsp), F32)],
        compiler_params=_params(("arbitrary", "arbitrary")),
        name="nsa_cmp_select",
    )(qp, cmp, gates, msel, gexp)


def _win_attn_kernel(*refs, tq, pos0, kblocks):
    nkb = len(kblocks)
    qr_ref, gates_ref, gexp_ref, oin_ref = refs[:4]
    k_refs = refs[4:4 + nkb]
    out_ref = refs[4 + nkb]
    qb = pl.program_id(1)
    lane = lax.broadcasted_iota(jnp.int32, (tq, LANES), 1)
    lo64 = lane < 64
    qs = _stack_q(qr_ref[...], lo64)
    kv = jnp.concatenate([r[...] for r in k_refs], axis=0)
    kk = kv[:, 0:128].astype(BF16)
    vv = kv[:, 128:256].astype(BF16)
    qpos = pos0 + qb * tq + lax.broadcasted_iota(jnp.int32, (tq, 1), 0)
    kpos = jnp.concatenate(
        [c0 + c1 * qb + lax.broadcasted_iota(jnp.int32, (tq, n), 1) for (c0, c1, n) in kblocks], axis=1)
    mask = (kpos <= qpos) & (kpos > qpos - WINDOW) & (kpos >= 0)
    s_all = _dot_nt(qs, kk) * (HEAD_DIM ** -0.5)
    o_slabs = []
    for r in range(8):
        p = _masked_softmax_rows(s_all[r * tq:(r + 1) * tq], mask)
        o_slabs.append(_dot(p.astype(BF16), vv))
    o = jnp.concatenate(o_slabs, axis=0)
    gx = _gate_expand(gates_ref[...], gexp_ref[...])
    for i, ob in enumerate(_unstack_o(o, tq, lo64)):
        sl = slice(i * LANES, (i + 1) * LANES)
        out_ref[:, sl] = oin_ref[:, sl] + gx[:, sl] * ob


def _win_attn(qr, gates, gexp, o_in, key_arrays, key_specs, kblocks, *, nseq, tq, pos0):
    t = qr.shape[0]
    nqb = t // (nseq * tq)
    row = lambda n: pl.BlockSpec((tq, n), lambda s, b: (s * nqb + b, 0))
    return pl.pallas_call(
        functools.partial(_win_attn_kernel, tq=tq, pos0=pos0, kblocks=tuple(kblocks)),
        grid=(nseq, nqb),
        in_specs=[row(512), row(128), _const_spec(gexp.shape), row(512)] + list(key_specs),
        out_specs=row(512),
        out_shape=jax.ShapeDtypeStruct((t, 512), F32),
        compiler_params=_params(("arbitrary", "arbitrary")),
        name="nsa_window",
    )(qr, gates, gexp, o_in, *key_arrays)


def _sel_attn_kernel(pt_ref, *refs, tq, pos0, npg, nsteps, nsp, c0, c1):
    qr_ref, sel_ref, gates_ref, gexp_ref, oin_ref, tail_ref = refs[:6]
    page_refs = refs[6:6 + npg]
    out_ref, qs_ref, m_ref, l_ref, acc_ref = refs[6 + npg:]
    qb = pl.program_id(1)
    k = pl.program_id(2)
    npast = c0 + c1 * qb
    lane = lax.broadcasted_iota(jnp.int32, (tq, LANES), 1)
    lo64 = lane < 64

    @pl.when(k == 0)
    def _():
        qs_ref[...] = _stack_q(qr_ref[...], lo64)
        m_ref[...] = jnp.full(m_ref.shape, NEG, F32)
        l_ref[...] = jnp.zeros(l_ref.shape, F32)
        acc_ref[...] = jnp.zeros(acc_ref.shape, F32)

    def update(kv, masks):
        kk = kv[:, 0:128].astype(BF16)
        vv = kv[:, 128:256].astype(BF16)
        s_all = _dot_nt(qs_ref[...], kk) * (HEAD_DIM ** -0.5)
        for r in range(8):
            rows = slice(r * tq, (r + 1) * tq)
            mask = masks[r // 4]
            s = jnp.where(mask, s_all[rows], NEG)
            m_old = m_ref[rows]
            m_new = jnp.maximum(m_old, jnp.max(s, axis=-1, keepdims=True))
            p = jnp.where(mask, jnp.exp(s - m_new), 0.0)
            alpha = jnp.exp(m_old - m_new)
            l_ref[rows] = alpha * l_ref[rows] + jnp.sum(p, axis=-1, keepdims=True)
            acc_ref[rows] = alpha * acc_ref[rows] + _dot(p.astype(BF16), vv)
            m_ref[rows] = m_new

    def block_masks(j0, nk, extra):
        jrow = lax.broadcasted_iota(jnp.int32, (nsp, nk), 0)
        col = lax.broadcasted_iota(jnp.int32, (nsp, nk), 1)
        e = jnp.where(jrow == j0 + jnp.right_shift(col, 6), 1.0, 0.0).astype(BF16)
        out = []
        for kvi in range(NSA_KV):
            sel = sel_ref[:, kvi * nsp:(kvi + 1) * nsp].astype(BF16)
            mk = _dot(sel, e) > 0.5
            out.append(mk & extra if extra is not None else mk)
        return out

    @pl.when((k < nsteps) & (k * npg < npast))
    def _():
        kv = jnp.concatenate([r[...] for r in page_refs], axis=0)
        nk = npg * PAGE
        kpage = k * npg + jnp.right_shift(lax.broadcasted_iota(jnp.int32, (tq, nk), 1), 7)
        update(kv, block_masks(k * npg * (PAGE // SEL_LEN), nk, kpage < npast))

    @pl.when(k == nsteps)
    def _():
        qpos = pos0 + qb * tq + lax.broadcasted_iota(jnp.int32, (tq, 1), 0)
        kpos = npast * PAGE + lax.broadcasted_iota(jnp.int32, (tq, PAGE), 1)
        update(tail_ref[...], block_masks(npast * (PAGE // SEL_LEN), PAGE, kpos <= qpos))
        l = l_ref[...]
        o = acc_ref[...] / jnp.maximum(l, TINY)
        gx = _gate_expand(gates_ref[...], gexp_ref[...])
        for i, ob in enumerate(_unstack_o(o, tq, lo64)):
            sl = slice(i * LANES, (i + 1) * LANES)
            out_ref[:, sl] = oin_ref[:, sl] + gx[:, sl] * ob


def _sel_attn(qr, sel, gates, gexp, o_in, tail3, pool3, page_table, *, nseq, tq, pos0, c0, c1, npg=4):
    t = qr.shape[0]
    nqb = t // (nseq * tq)
    npages = page_table.shape[1]
    nsteps = -(-npages // npg)
    nsp = sel.shape[1] // 2
    row = lambda n: pl.BlockSpec((tq, n), lambda s, b, k, pt: (s * nqb + b, 0))

    def page_spec(p):
        def imap(s, b, k, pt):
            last = jnp.maximum(c0 + c1 * b - 1, 0)
            logical = jnp.minimum(jnp.minimum(k * npg + p, last), npages - 1)
            return (pt[s, logical], 0, 1)
        return pl.BlockSpec((None, PAGE, 256), imap)

    grid_spec = pltpu.PrefetchScalarGridSpec(
        num_scalar_prefetch=1,
        grid=(nseq, nqb, nsteps + 1),
        in_specs=[row(512), row(2 * nsp), row(128),
                  pl.BlockSpec(gexp.shape, lambda s, b, k, pt: (0, 0)), row(512),
                  pl.BlockSpec((None, PAGE, 256), lambda s, b, k, pt: (s * nqb + b, 0, 1))]
                 + [page_spec(p) for p in range(npg)],
        out_specs=row(512),
        scratch_shapes=[pltpu.VMEM((8 * tq, LANES), BF16), pltpu.VMEM((8 * tq, 1), F32),
                        pltpu.VMEM((8 * tq, 1), F32), pltpu.VMEM((8 * tq, LANES), F32)],
    )
    return pl.pallas_call(
        functools.partial(_sel_attn_kernel, tq=tq, pos0=pos0, npg=npg, nsteps=nsteps, nsp=nsp, c0=c0, c1=c1),
        grid_spec=grid_spec,
        out_shape=jax.ShapeDtypeStruct((t, 512), F32),
        compiler_params=_params(("arbitrary", "arbitrary", "arbitrary")),
        name="nsa_selected",
    )(page_table, qr, sel, gates, gexp, o_in, tail3, *([pool3] * npg))


def _rms_matmul_kernel(x_ref, g_ref, w_ref, *rest, n_norm):
    if n_norm:
        gseg_ref, out_ref = rest
    else:
        (out_ref,) = rest
    h = _rms_rows(x_ref[...], g_ref[...]).astype(BF16)
    z = _dot(h, w_ref[...])
    if n_norm:
        for c0 in range(0, n_norm, MEM_HD):
            out_ref[:, c0:c0 + MEM_HD] = _rms_rows(z[:, c0:c0 + MEM_HD], gseg_ref[...])
        if n_norm < z.shape[1]:
            out_ref[:, n_norm:] = z[:, n_norm:]
    else:
        out_ref[...] = z


def _rms_matmul(x, g, w, tm, tn, n_norm=0, gseg=None):
    t, n = x.shape[0], w.shape[1]
    extra, extra_specs = [], []
    if n_norm:
        assert tn == n
        extra = [gseg]
        extra_specs = [_const_spec(gseg.shape)]
    return pl.pallas_call(
        functools.partial(_rms_matmul_kernel, n_norm=n_norm),
        grid=(t // tm, n // tn),
        in_specs=[pl.BlockSpec((tm, D_MODEL), lambda i, j: (i, 0)), _const_spec((1, D_MODEL)),
                  pl.BlockSpec((D_MODEL, tn), lambda i, j: (0, j))] + extra_specs,
        out_specs=pl.BlockSpec((tm, tn), lambda i, j: (i, j)),
        out_shape=jax.ShapeDtypeStruct((t, n), F32),
        compiler_params=_params(("arbitrary", "arbitrary")),
        name="rms_matmul",
    )(x, g, w, *extra)


def _matmul_res_kernel(*refs, na):
    a_refs = refs[:na]
    w_ref, x_ref, out_ref = refs[na:]
    acc = x_ref[...]
    k0 = 0
    for a_ref in a_refs:
        kw = a_ref.shape[1]
        acc = acc + _dot(a_ref[...].astype(BF16), w_ref[k0:k0 + kw, :])
        k0 += kw
    out_ref[...] = acc


def _matmul_res(acts, w, x, tm):
    t = x.shape[0]
    return pl.pallas_call(
        functools.partial(_matmul_res_kernel, na=len(acts)),
        grid=(t // tm,),
        in_specs=[pl.BlockSpec((tm, a.shape[1]), lambda i: (i, 0)) for a in acts]
                 + [_const_spec(w.shape), pl.BlockSpec((tm, D_MODEL), lambda i: (i, 0))],
        out_specs=pl.BlockSpec((tm, D_MODEL), lambda i: (i, 0)),
        out_shape=jax.ShapeDtypeStruct((t, D_MODEL), F32),
        compiler_params=_params(("arbitrary",)),
        name="matmul_residual",
    )(*acts, w, x)


def _cross_attn_kernel(q_ref, mkv_ref, out_ref):
    for h in range(MEM_HEADS):
        sl = slice(h * MEM_HD, (h + 1) * MEM_HD)
        q = q_ref[:, sl].astype(BF16)
        kk = mkv_ref[:, sl].astype(BF16)
        vv = mkv_ref[:, D_MODEL + h * MEM_HD:D_MODEL + (h + 1) * MEM_HD].astype(BF16)
        s = _dot_nt(q, kk) * (MEM_HD ** -0.5)
        m = jnp.max(s, axis=-1, keepdims=True)
        p = jnp.exp(s - m)
        p = p / jnp.sum(p, axis=-1, keepdims=True)
        out_ref[:, sl] = _dot(p.astype(BF16), vv)


def _cross_attn(q, mkv, *, nseq, tq):
    t = q.shape[0]
    nqb = t // (nseq * tq)
    mlen = mkv.shape[1]
    return pl.pallas_call(
        _cross_attn_kernel,
        grid=(nseq, nqb),
        in_specs=[pl.BlockSpec((tq, D_MODEL), lambda s, b: (s * nqb + b, 0)),
                  pl.BlockSpec((None, mlen, 2 * D_MODEL), lambda s, b: (s, 0, 0))],
        out_specs=pl.BlockSpec((tq, D_MODEL), lambda s, b: (s * nqb + b, 0)),
        out_shape=jax.ShapeDtypeStruct((t, D_MODEL), F32),
        compiler_params=_params(("arbitrary", "arbitrary")),
        name="cross_attn",
    )(q, mkv)


def _convffn_kernel(x_ref, g_ref, wa_ref, wg_ref, dwa_ref, dwg_ref, wd_ref,
                    out_ref, sta_ref, stg_ref,
                    hn_ref, acc_ref, sa_ref, sg_ref, ca_ref, cg_ref, *, tm, nff):
    i = pl.program_id(0)
    j = pl.program_id(1)

    @pl.when(j == 0)
    def _():
        hn_ref[...] = _rms_rows(x_ref[...], g_ref[...]).astype(BF16)
        acc_ref[...] = jnp.zeros(acc_ref.shape, F32)

    @pl.when(i == 0)
    def _():
        ca_ref[j] = jnp.zeros(ca_ref.shape[1:], F32)
        cg_ref[j] = jnp.zeros(cg_ref.shape[1:], F32)

    hn = hn_ref[...]

    def conv(w_ref, s_ref, c_ref, dw_ref, st_ref):
        u = _dot(hn, w_ref[...])
        s_ref[0:8, :] = c_ref[j]
        s_ref[8:8 + tm, :] = u
        c_ref[j] = u[tm - 8:tm]
        st_ref[...] = u[tm - 8:tm]
        dw = dw_ref[...]
        return dw[0:1] * s_ref[6:6 + tm, :] + dw[1:2] * s_ref[7:7 + tm, :] + dw[2:3] * u + dw[3:4]

    a = conv(wa_ref, sa_ref, ca_ref, dwa_ref, sta_ref)
    g = conv(wg_ref, sg_ref, cg_ref, dwg_ref, stg_ref)
    y = (g * _sigmoid(g)) * a
    acc_ref[...] += _dot(y.astype(BF16), wd_ref[...])

    @pl.when(j == nff - 1)
    def _():
        out_ref[...] = x_ref[...] + acc_ref[...]


def _convffn(x, g, w_up, dwb, w_down, tm, tf):
    t = x.shape[0]
    nff = D_FF // tf
    nt = t // tm
    return pl.pallas_call(
        functools.partial(_convffn_kernel, tm=tm, nff=nff),
        grid=(nt, nff),
        in_specs=[pl.BlockSpec((tm, D_MODEL), lambda i, j: (i, 0)), _const_spec((1, D_MODEL)),
                  pl.BlockSpec((D_MODEL, tf), lambda i, j: (0, j)),
                  pl.BlockSpec((D_MODEL, tf), lambda i, j: (0, nff + j)),
                  pl.BlockSpec((8, tf), lambda i, j: (0, j)),
                  pl.BlockSpec((8, tf), lambda i, j: (0, nff + j)),
                  pl.BlockSpec((tf, D_MODEL), lambda i, j: (j, 0))],
        out_specs=[pl.BlockSpec((tm, D_MODEL), lambda i, j: (i, 0)),
                   pl.BlockSpec((None, 8, tf), lambda i, j: (i, 0, j)),
                   pl.BlockSpec((None, 8, tf), lambda i, j: (i, 0, j))],
        out_shape=[jax.ShapeDtypeStruct((t, D_MODEL), F32),
                   jax.ShapeDtypeStruct((nt, 8, D_FF), F32),
                   jax.ShapeDtypeStruct((nt, 8, D_FF), F32)],
        scratch_shapes=[pltpu.VMEM((tm, D_MODEL), BF16), pltpu.VMEM((tm, D_MODEL), F32),
                        pltpu.VMEM((tm + 8, tf), F32), pltpu.VMEM((tm + 8, tf), F32),
                        pltpu.VMEM((nff, 8, tf), F32), pltpu.VMEM((nff, 8, tf), F32)],
        compiler_params=_params(("arbitrary", "arbitrary")),
        name="convffn",
    )(x, g, w_up, w_up, dwb, dwb, w_down)


def _convgate_down_kernel(a0_ref, a1_ref, a2_ref, g0_ref, g1_ref, g2_ref, dwa_ref, dwg_ref, wd_ref, x_ref,
                          out_ref, acc_ref, *, nff):
    j = pl.program_id(0)

    @pl.when(j == 0)
    def _():
        acc_ref[...] = jnp.zeros(acc_ref.shape, F32)

    dwa = dwa_ref[...]
    dwg = dwg_ref[...]
    a = dwa[0:1] * a0_ref[...] + dwa[1:2] * a1_ref[...] + dwa[2:3] * a2_ref[...] + dwa[3:4]
    g = dwg[0:1] * g0_ref[...] + dwg[1:2] * g1_ref[...] + dwg[2:3] * g2_ref[...] + dwg[3:4]
    y = (g * _sigmoid(g)) * a
    acc_ref[...] += _dot(y.astype(BF16), wd_ref[...])

    @pl.when(j == nff - 1)
    def _():
        out_ref[...] = x_ref[...] + acc_ref[...]


def _convgate_down(f0, f1, f2, dwb, w_down, x, tf):
    t = x.shape[0]
    nff = D_FF // tf
    fa = pl.BlockSpec((t, tf), lambda j: (0, j))
    fg = pl.BlockSpec((t, tf), lambda j: (0, nff + j))
    return pl.pallas_call(
        functools.partial(_convgate_down_kernel, nff=nff),
        grid=(nff,),
        in_specs=[fa, fa, fa, fg, fg, fg,
                  pl.BlockSpec((8, tf), lambda j: (0, j)), pl.BlockSpec((8, tf), lambda j: (0, nff + j)),
                  pl.BlockSpec((tf, D_MODEL), lambda j: (j, 0)), _const_spec((t, D_MODEL))],
        out_specs=_const_spec((t, D_MODEL)),
        out_shape=jax.ShapeDtypeStruct((t, D_MODEL), F32),
        scratch_shapes=[pltpu.VMEM((t, D_MODEL), F32)],
        compiler_params=_params(("arbitrary",)),
        name="convgate_down",
    )(f0, f1, f2, f0, f1, f2, dwb, dwb, w_down, x)


def _inproj_odd_kernel(x_ref, g_ref, w_ref, glu_ref, q_ref, rows_ref):
    h = _rms_rows(x_ref[...], g_ref[...]).astype(BF16)
    z = _dot(h, w_ref[...])
    glu_ref[...] = z[:, 0:512] * _sigmoid(z[:, 512:1024])
    q_ref[...] = z[:, 1024:1536]
    rows_ref[...] = z[:, 1536:2560]


def _inproj_odd(x, g, w, tm):
    t = x.shape[0]
    row = lambda n: pl.BlockSpec((tm, n), lambda i: (i, 0))
    return pl.pallas_call(
        _inproj_odd_kernel,
        grid=(t // tm,),
        in_specs=[row(D_MODEL), _const_spec((1, D_MODEL)), _const_spec(w.shape)],
        out_specs=[row(512), row(512), row(1024)],
        out_shape=[jax.ShapeDtypeStruct((t, 512), F32), jax.ShapeDtypeStruct((t, 512), F32),
                   jax.ShapeDtypeStruct((t, 1024), F32)],
        compiler_params=_params(("arbitrary",)),
        name="inproj_odd",
    )(x, g, w)


def _conv_module_kernel(prev_ref, cur_ref, dw_ref, aux_ref, out_ref, s_ref, *, tm, zero_first, rb):
    i = pl.program_id(1)
    prev = prev_ref[...]
    if zero_first:
        prev = jnp.where(i == 0, 0.0, prev)
    s_ref[0:32, :] = prev
    s_ref[32:32 + tm, :] = cur_ref[...]
    aux = aux_ref[...]
    for r0 in range(0, tm, rb):
        acc = jnp.zeros((rb, CV_WIDTH), F32) + aux[0:1]
        for d in range(CV_K):
            acc = acc + dw_ref[CV_K - 1 - d:CV_K - d, :] * s_ref[32 - d + r0:32 - d + r0 + rb, :]
        c = acc - jnp.mean(acc, axis=-1, keepdims=True)
        y = c * lax.rsqrt(jnp.mean(c * c, axis=-1, keepdims=True) + EPS) * aux[1:2] + aux[2:3]
        out_ref[r0:r0 + rb, :] = y * _sigmoid(y)


def _conv_module(prev3, cur3, dw, aux, *, tm, zero_first):
    nseq, tseg, _ = cur3.shape
    nb = tseg // tm
    if zero_first:
        prev_spec = pl.BlockSpec((None, 32, CV_WIDTH),
                                 lambda s, i: (s, jnp.maximum(i * (tm // 32) - 1, 0), 0))
    else:
        prev_spec = pl.BlockSpec((None, 32, CV_WIDTH), lambda s, i: (s, 0, 0))
    return pl.pallas_call(
        functools.partial(_conv_module_kernel, tm=tm, zero_first=zero_first, rb=min(tm, 32)),
        grid=(nseq, nb),
        in_specs=[prev_spec, pl.BlockSpec((None, tm, CV_WIDTH), lambda s, i: (s, i, 0)),
                  _const_spec(dw.shape), _const_spec(aux.shape)],
        out_specs=pl.BlockSpec((None, tm, CV_WIDTH), lambda s, i: (s, i, 0)),
        out_shape=jax.ShapeDtypeStruct((nseq, tseg, CV_WIDTH), F32),
        scratch_shapes=[pltpu.VMEM((tm + 32, CV_WIDTH), F32)],
        compiler_params=_params(("arbitrary", "arbitrary")),
        name="conv_module",
    )(prev3, cur3, dw, aux)


def _sb_kernel(pt_ref, q_ref, tail_ref, p1_ref, p2_ref, ucat_ref, pool_ref, out_ref,
               carry_ref, acc_ref, buf_ref, sem, *, tq, pos0, c0, c1):
    s = pl.program_id(0)
    qb = pl.program_id(1)
    npast = c0 + c1 * qb
    lane = lax.broadcasted_iota(jnp.int32, (tq, LANES), 1)
    lo64 = lane < 64
    q = q_ref[...]
    qs = []
    for i in range(4):
        blk = q[:, i * LANES:(i + 1) * LANES]
        qs.append(jnp.concatenate([jnp.where(lo64, blk, 0.0), jnp.where(lo64, 0.0, blk)], axis=0).astype(BF16))
    carry_ref[...] = jnp.zeros(carry_ref.shape, F32)
    acc_ref[...] = jnp.zeros(acc_ref.shape, F32)
    ucat = ucat_ref[...]

    def process(kv_ref, mask):
        for i in range(4):
            kb = kv_ref[:, i * LANES:(i + 1) * LANES].astype(BF16)
            vb = kv_ref[:, SB_WIDTH + i * LANES:SB_WIDTH + (i + 1) * LANES].astype(BF16)
            z = _dot_nt(qs[i], kb) * (HEAD_DIM ** -0.5)
            log_b = -(jnp.maximum(-z, 0.0) + jnp.log1p(jnp.exp(-jnp.abs(z))))
            l1m = log_b - z
            if mask is not None:
                l1m = jnp.where(mask, l1m, 0.0)
            ac = _dot_hilo(l1m, ucat)
            a = jnp.exp(log_b + ac[:, 0:PAGE] + carry_ref[i])
            if mask is not None:
                a = jnp.where(mask, a, 0.0)
            acc_ref[i] += _dot(a.astype(BF16), vb)
            carry_ref[i] += ac[:, PAGE:2 * PAGE]

    def live():
        return jnp.max(carry_ref[...]) >= SB_EXIT

    qpos = pos0 + qb * tq + lax.broadcasted_iota(jnp.int32, (tq, 1), 0)
    qpos2 = jnp.concatenate([qpos, qpos], axis=0)
    kpos = npast * PAGE + lax.broadcasted_iota(jnp.int32, (2 * tq, PAGE), 1)
    process(tail_ref, kpos < qpos2)

    @pl.when(npast >= 1)
    def _():
        process(p1_ref, None)

    @pl.when((npast >= 2) & live())
    def _():
        process(p2_ref, None)

    def body(state):
        p, _ = state
        cp = pltpu.make_async_copy(pool_ref.at[pt_ref[s, p]], buf_ref, sem)
        cp.start()
        cp.wait()
        process(buf_ref, None)
        return p - 1, live().astype(jnp.int32)

    lax.while_loop(lambda st: (st[0] >= 0) & (st[1] > 0), body, (npast - 3, live().astype(jnp.int32)))

    for i in range(4):
        out_ref[:, i * LANES:(i + 1) * LANES] = jnp.where(lo64, acc_ref[i, 0:tq], acc_ref[i, tq:2 * tq])


def _sb_attn(q, tail3, pool3, page_table, ucat, *, nseq, tq, pos0, c0, c1):
    t = q.shape[0]
    nqb = t // (nseq * tq)
    npages = page_table.shape[1]

    def page_spec(back):
        def imap(s, b, pt):
            logical = jnp.clip(c0 + c1 * b - back, 0, npages - 1)
            return (pt[s, logical], 0, 0)
        return pl.BlockSpec((None, PAGE, 2 * SB_WIDTH), imap)

    grid_spec = pltpu.PrefetchScalarGridSpec(
        num_scalar_prefetch=1,
        grid=(nseq, nqb),
        in_specs=[pl.BlockSpec((tq, SB_WIDTH), lambda s, b, pt: (s * nqb + b, 0)),
                  pl.BlockSpec((None, PAGE, 2 * SB_WIDTH), lambda s, b, pt: (s * nqb + b, 0, 0)),
                  page_spec(1), page_spec(2),
                  pl.BlockSpec(ucat.shape, lambda s, b, pt: (0, 0)),
                  pl.BlockSpec(memory_space=pl.ANY)],
        out_specs=pl.BlockSpec((tq, SB_WIDTH), lambda s, b, pt: (s * nqb + b, 0)),
        scratch_shapes=[pltpu.VMEM((4, 2 * tq, PAGE), F32), pltpu.VMEM((4, 2 * tq, LANES), F32),
                        pltpu.VMEM((PAGE, 2 * SB_WIDTH), F32), pltpu.SemaphoreType.DMA(())],
    )
    return pl.pallas_call(
        functools.partial(_sb_kernel, tq=tq, pos0=pos0, c0=c0, c1=c1),
        grid_spec=grid_spec,
        out_shape=jax.ShapeDtypeStruct((t, SB_WIDTH), F32),
        compiler_params=_params(("arbitrary", "arbitrary")),
        name="stick_breaking",
    )(page_table, q, tail3, pool3, pool3, ucat, pool3)


def _q_perm():
    idx = np.zeros((512,), np.int32)
    for i in range(4):
        for half in range(2):
            for d in range(HEAD_DIM):
                idx[i * 128 + half * 64 + d] = (half * 4 + i) * HEAD_DIM + d
    return idx


def _gate_perm():
    idx = np.zeros((24,), np.int32)
    for c in range(3):
        for i in range(4):
            for half in range(2):
                idx[c * 8 + 2 * i + half] = (half * 4 + i) * 3 + c
    return idx


def _gate_expand_mats():
    g = np.zeros((3, 128, 512), np.float32)
    for c in range(3):
        for i in range(4):
            for half in range(2):
                g[c, c * 8 + 2 * i + half, i * 128 + half * 64:i * 128 + (half + 1) * 64] = 1.0
    return jnp.asarray(g, BF16)


def _block_ones(n, w):
    r = np.arange(n) // w
    return jnp.asarray((r[:, None] == r[None, :]).astype(np.float32), BF16)


def _rope_tables(pos):
    half = HEAD_DIM // 2
    inv = jnp.power(ROPE_THETA, -jnp.arange(half, dtype=F32) / half)
    ang = pos.astype(F32)[:, None] * inv[None, :]
    c, s = jnp.cos(ang), jnp.sin(ang)
    return jnp.concatenate([c, c, c, c], -1), jnp.concatenate([-s, s, -s, s], -1)


def _sel_sum_matrix(nch, nsp, nc):
    n = np.arange(nch)[:, None]
    j = np.arange(nsp)[None, :]
    step = SEL_LEN // CMP_STRIDE
    lo = 1 - CMP_LEN // CMP_STRIDE
    m = (n >= step * j + lo) & (n <= step * j + step - 1) & (n < nc)
    return jnp.asarray(m.astype(np.float32), BF16)


def _chunk_perm():
    p = np.zeros((256, 256), np.float32)
    for l in range(16):
        for c in range(16):
            p[l * 16 + c, c * 16 + l] = 1.0
    return jnp.asarray(p, BF16)


def _compress_weights(cmp_pe, w_cmp):
    wl = jnp.zeros((16, 4, HEAD_DIM, 2, 4, HEAD_DIM), F32)
    for part in range(4):
        for half in range(2):
            wl = wl.at[:, part, :, half, part, :].set(w_cmp[part // 2, half * 16:(half + 1) * 16])
    wl = wl.reshape(16, 256, 512).astype(BF16)
    pe1 = jnp.concatenate([cmp_pe[0, 0:16]] * 2 + [cmp_pe[1, 0:16]] * 2, -1)
    pe2 = jnp.concatenate([cmp_pe[0, 16:32]] * 2 + [cmp_pe[1, 16:32]] * 2, -1)
    h1, l1 = _split_bf16(pe1)
    h2, l2 = _split_bf16(pe2)
    pe_rows = jnp.zeros((16, 16, 256), BF16)
    pe_rows = pe_rows.at[:, 0].set(h1).at[:, 1].set(h2).at[:, 2].set(l1).at[:, 3].set(l2)
    return wl, pe_rows


def _sb_ucat():
    j = np.arange(PAGE)[:, None]
    s = np.arange(PAGE)[None, :]
    u = (j > s).astype(np.float32)
    return jnp.asarray(np.concatenate([u, np.ones((PAGE, PAGE), np.float32)], 1), BF16)


def _pad_rows(a, n):
    return jnp.pad(a, ((0, 0), (0, n - a.shape[1]), (0, 0)))


def _even_layer(xp, xs, i, g0, page_table, cache_nsa, cache_nsa_win, w_in_e, g_qk_nsa, cmp_pe, w_cmp,
                gm_ws, gm_b, gm_ln, w_out_e, past):
    sp = xp.shape[0]
    nseq = page_table.shape[0]
    ts = xs.shape[0] // nseq
    qperm, gperm = _q_perm(), _gate_perm()
    w = w_in_e[i]
    wq = w[:, 0:512][:, qperm]
    wg = jnp.pad(w[:, 1280:1304][:, gperm], ((0, 0), (0, 104)))
    w_all = jnp.concatenate([wq, w[:, 512:1280], wg, w[:, 1304:2328]], -1).astype(BF16)
    gq = jnp.tile(g_qk_nsa[i, 0], 8)[None]
    gk = jnp.stack([jnp.tile(g_qk_nsa[i, r], 2) for r in (1, 2, 3)])
    ones_bd = _block_ones(512, HEAD_DIM)
    gexp = _gate_expand_mats()
    w_out = jnp.concatenate([w_out_e[i][0:512][qperm], w_out_e[i][512:1024]], 0).astype(BF16)
    wl, pe_rows = _compress_weights(cmp_pe[i], w_cmp[i])
    perm = _chunk_perm()

    def inproj(x, pos, tm, rchunk):
        cos_t, sin_t = _rope_tables(pos)
        ws_t = jnp.tril(gm_ws[i])[:, :rchunk, :rchunk]
        eye = jnp.eye(tm // rchunk, dtype=F32)
        wsg = jnp.einsum("ab,gts->gatbs", eye, ws_t).reshape(GM_GROUPS, tm, tm).astype(BF16)
        sbt = jnp.tile(jnp.repeat(gm_b[i][:, :rchunk].T, GM_WIDTH // GM_GROUPS, axis=1), (tm // rchunk, 1))
        return _inproj_even(x, g0, w_all, cos_t, sin_t, gq, gk, gm_ln[i], ones_bd, wsg, sbt, tm)

    tqp = 128
    qp, qr, rows, win, gates, v, ogm = inproj(xp, jnp.arange(sp), 256, CHUNK)
    npp = sp // PAGE
    pt_p = jnp.arange(npp, dtype=jnp.int32)[None]
    pool_p = rows.reshape(npp, PAGE, 512)
    cmp_p = _compress(pool_p, pt_p, perm, wl, pe_rows)
    nc = (sp - CMP_LEN) // CMP_STRIDE + 1
    ns = -(-sp // SEL_LEN)
    nsp = -(-ns // LANES) * LANES
    assert nc + 1 <= cmp_p.shape[1]
    msel = _sel_sum_matrix(cmp_p.shape[1], nsp, nc)
    o, sel = _cmp_attn(qp, cmp_p, gates, msel, gexp[0], nseq=1, tq=tqp, pos0=0, nc=nc, ns=ns,
                       k_eff=min(N_SEL, ns))
    nwb = WINDOW // tqp
    kspecs = [pl.BlockSpec((tqp, 256), (lambda s, b, j=j: (jnp.maximum(b - nwb + j, 0), 0)))
              for j in range(nwb + 1)]
    kblocks = [((j - nwb) * tqp, tqp, tqp) for j in range(nwb + 1)]
    o = _win_attn(qr, gates, gexp[2], o, [win] * (nwb + 1), kspecs, kblocks, nseq=1, tq=tqp, pos0=0)
    o = _sel_attn(qr, sel, gates, gexp[1], o, pool_p, pool_p, pt_p, nseq=1, tq=tqp, pos0=0, c0=0, c1=1)
    xp = _matmul_res([o, ogm], w_out, xp, 512)
    outs_p = (rows, win[sp - min(WINDOW, sp):], v[((sp - 1) // CHUNK) * CHUNK:])

    pos_s = jnp.tile(past + jnp.arange(ts), nseq)
    qp, qr, rows, win, gates, v, ogm = inproj(xs, pos_s, nseq * ts, ts)
    pool_s = cache_nsa[i].reshape(cache_nsa.shape[1], PAGE, 512)
    cmp_s = _compress(pool_s, page_table, perm, wl, pe_rows)
    ltot = past + ts
    nc = (ltot - CMP_LEN) // CMP_STRIDE + 1
    ns = -(-ltot // SEL_LEN)
    nsp = -(-ns // LANES) * LANES
    assert nc + 1 <= cmp_s.shape[1] and ts <= PAGE and past % PAGE == 0
    msel = _sel_sum_matrix(cmp_s.shape[1], nsp, nc)
    o, sel = _cmp_attn(qp, cmp_s, gates, msel, gexp[0], nseq=nseq, tq=ts, pos0=past, nc=nc, ns=ns,
                       k_eff=min(N_SEL, ns))
    wb = cache_nsa_win.shape[2]
    win_old = cache_nsa_win[i].reshape(nseq, wb, 256)
    win_new = _pad_rows(win.reshape(nseq, ts, 256), PAGE)
    kspecs = [pl.BlockSpec((None, wb, 256), lambda s, b: (s, 0, 0)),
              pl.BlockSpec((None, PAGE, 256), lambda s, b: (s, 0, 0))]
    kblocks = [(past - wb, 0, wb), (past, 0, PAGE)]
    o = _win_attn(qr, gates, gexp[2], o, [win_old, win_new], kspecs, kblocks, nseq=nseq, tq=ts, pos0=past)
    tail_s = _pad_rows(rows.reshape(nseq, ts, 512), PAGE)
    o = _sel_attn(qr, sel, gates, gexp[1], o, tail_s, pool_s, page_table, nseq=nseq, tq=ts, pos0=past,
                  c0=past // PAGE, c1=0)
    xs = _matmul_res([o, ogm], w_out, xs, nseq * ts)
    win_s = jnp.concatenate([win_old, win.reshape(nseq, ts, 256)], 1)[:, -wb:]
    outs_s = (rows, win_s, v)
    return xp, xs, outs_p, outs_s


def _odd_layer(xp, xs, i, g0, page_table, cache_sb, state_conv, w_in_o, cv_dw, cv_b, cv_ln, w_out_o, past):
    sp = xp.shape[0]
    nseq = page_table.shape[0]
    ts = xs.shape[0] // nseq
    w_in = w_in_o[i].astype(BF16)
    w_out = w_out_o[i].astype(BF16)
    dw = jnp.pad(cv_dw[i], ((0, 1), (0, 0)))
    aux = jnp.concatenate([cv_b[i][None], cv_ln[i], jnp.zeros((5, CV_WIDTH), F32)], 0)
    ucat = _sb_ucat()

    glu, q, rows = _inproj_odd(xp, g0, w_in, 256)
    o_cv = _conv_module(glu[None], glu[None], dw, aux, tm=256, zero_first=True)[0]
    npp = sp // PAGE
    pt_p = jnp.arange(npp, dtype=jnp.int32)[None]
    pool_p = rows.reshape(npp, PAGE, 2 * SB_WIDTH)
    o_sb = _sb_attn(q, pool_p, pool_p, pt_p, ucat, nseq=1, tq=PAGE, pos0=0, c0=0, c1=1)
    xp = _matmul_res([o_cv, o_sb], w_out, xp, 512)
    outs_p = (rows, glu[sp - (CV_K - 1):])

    glu, q, rows = _inproj_odd(xs, g0, w_in, nseq * ts)
    glu3 = glu.reshape(nseq, ts, CV_WIDTH)
    prev = jnp.pad(state_conv[i], ((0, 0), (32 - (CV_K - 1), 0), (0, 0)))
    o_cv = _conv_module(prev, glu3, dw, aux, tm=ts, zero_first=False).reshape(nseq * ts, CV_WIDTH)
    pool_s = cache_sb[i].reshape(cache_sb.shape[1], PAGE, 2 * SB_WIDTH)
    tail_s = _pad_rows(rows.reshape(nseq, ts, 2 * SB_WIDTH), PAGE)
    o_sb = _sb_attn(q, tail_s, pool_s, page_table, ucat, nseq=nseq, tq=ts, pos0=past, c0=past // PAGE, c1=0)
    xs = _matmul_res([o_cv, o_sb], w_out, xs, nseq * ts)
    conv_s = jnp.concatenate([state_conv[i], glu3], 1)[:, -(CV_K - 1):]
    outs_s = (rows, conv_s)
    return xp, xs, outs_p, outs_s


def kernel(x_prompt, x_sample, mem_prompt, page_table, cache_nsa, cache_nsa_win, cache_sb, state_conv, state_ffn, cache_mem, g_norm, w_in_e, g_qk_nsa, cmp_pe, w_cmp, gm_ws, gm_b, gm_ln, w_out_e, w_in_o, cv_dw, cv_b, cv_ln, w_out_o, w_xq, w_xkv, g_xqk, w_xo, w_up, ffn_dw, ffn_db, w_down):
    bp, sp, _ = x_prompt.shape
    nseq, ts, _ = x_sample.shape
    assert bp == 1
    depth = g_norm.shape[0]
    past = page_table.shape[1] * PAGE
    mlen = mem_prompt.shape[1]
    xp = x_prompt.reshape(sp, D_MODEL)
    xs = x_sample.reshape(nseq * ts, D_MODEL)
    o = {k: [] for k in ("nsa_rows_p", "nsa_rows_s", "nsa_win_p", "nsa_win_s", "gm_v_p", "gm_v_s",
                         "sb_rows_p", "sb_rows_s", "conv_p", "conv_s", "ffn_p", "ffn_s", "memkv_p")}
    for l in range(depth):
        i = l // 2
        g0 = g_norm[l, 0][None]
        if l % 2 == 0:
            xp, xs, (rows_p, win_p, v_p), (rows_s, win_s, v_s) = _even_layer(
                xp, xs, i, g0, page_table, cache_nsa, cache_nsa_win, w_in_e, g_qk_nsa, cmp_pe, w_cmp,
                gm_ws, gm_b, gm_ln, w_out_e, past)
            o["nsa_rows_p"].append(rows_p.reshape(1, sp, 4, NSA_KV, HEAD_DIM))
            o["nsa_rows_s"].append(rows_s.reshape(nseq, ts, 4, NSA_KV, HEAD_DIM))
            o["nsa_win_p"].append(win_p.reshape(1, -1, 2, NSA_KV, HEAD_DIM))
            o["nsa_win_s"].append(win_s.reshape(nseq, -1, 2, NSA_KV, HEAD_DIM))
            o["gm_v_p"].append(v_p[None])
            o["gm_v_s"].append(v_s.reshape(nseq, ts, GM_WIDTH))
        else:
            xp, xs, (rows_p, conv_p), (rows_s, conv_s) = _odd_layer(
                xp, xs, i, g0, page_table, cache_sb, state_conv, w_in_o, cv_dw, cv_b, cv_ln, w_out_o, past)
            o["sb_rows_p"].append(rows_p.reshape(1, sp, 2, SB_HEADS, HEAD_DIM))
            o["sb_rows_s"].append(rows_s.reshape(nseq, ts, 2, SB_HEADS, HEAD_DIM))
            o["conv_p"].append(conv_p[None])
            o["conv_s"].append(conv_s)

        w_xq_b = w_xq[l].astype(BF16)
        w_xo_b = w_xo[l].astype(BF16)
        g_xk = g_xqk[l, 1][None]
        g_xq = g_xqk[l, 0][None]
        mkv = _rms_matmul(mem_prompt.reshape(mlen, D_MODEL), g_norm[l, 3][None], w_xkv[l].astype(BF16),
                          mlen, 2 * D_MODEL, D_MODEL, g_xk)
        o["memkv_p"].append(mkv.reshape(1, mlen, 2, MEM_HEADS, MEM_HD))
        g1 = g_norm[l, 1][None]
        qx = _rms_matmul(xp, g1, w_xq_b, 512, D_MODEL, D_MODEL, g_xq)
        xp = _matmul_res([_cross_attn(qx, mkv[None], nseq=1, tq=512)], w_xo_b, xp, 512)
        qx = _rms_matmul(xs, g1, w_xq_b, nseq * ts, D_MODEL, D_MODEL, g_xq)
        mkv_s = cache_mem[l].reshape(nseq, mlen, 2 * D_MODEL)
        xs = _matmul_res([_cross_attn(qx, mkv_s, nseq=nseq, tq=ts)], w_xo_b, xs, nseq * ts)

        g2 = g_norm[l, 2][None]
        w_up_b = w_up[l].astype(BF16)
        w_down_b = w_down[l].astype(BF16)
        dwb = jnp.concatenate([ffn_dw[l], ffn_db[l][None], jnp.zeros((4, 2 * D_FF), F32)], 0)
        xp, sta, stg = _convffn(xp, g2, w_up_b, dwb, w_down_b, 512, D_FF // 2)
        o["ffn_p"].append(jnp.concatenate([sta[-1, 8 - (FFN_K - 1):], stg[-1, 8 - (FFN_K - 1):]], -1)[None])
        up = _rms_matmul(xs, g2, w_up_b, nseq * ts, D_FF // 2)
        full = jnp.concatenate([state_ffn[l], up.reshape(nseq, ts, 2 * D_FF)], 1)
        taps = [full[:, k:k + ts].reshape(nseq * ts, 2 * D_FF) for k in range(FFN_K)]
        xs = _convgate_down(taps[0], taps[1], taps[2], dwb, w_down_b, xs, D_FF // 2)
        o["ffn_s"].append(full[:, -(FFN_K - 1):])

    return (xp.reshape(1, sp, D_MODEL), xs.reshape(nseq, ts, D_MODEL),
            jnp.stack(o["nsa_rows_p"]), jnp.stack(o["nsa_rows_s"]),
            jnp.stack(o["nsa_win_p"]), jnp.stack(o["nsa_win_s"]),
            jnp.stack(o["gm_v_p"]), jnp.stack(o["gm_v_s"]),
            jnp.stack(o["sb_rows_p"]), jnp.stack(o["sb_rows_s"]),
            jnp.stack(o["conv_p"]), jnp.stack(o["conv_s"]),
            jnp.stack(o["ffn_p"]), jnp.stack(o["ffn_s"]),
            jnp.stack(o["memkv_p"]))
```

```python
import functools

import numpy as np
import jax
import jax.numpy as jnp
from jax import lax
from jax.experimental import pallas as pl
from jax.experimental.pallas import tpu as pltpu

F32 = jnp.float32
BF16 = jnp.bfloat16

D_MODEL = 1024
HEAD_DIM = 64
NSA_HEADS = 8
NSA_KV = 2
CMP_LEN = 32
CMP_STRIDE = 16
SEL_LEN = 64
N_SEL = 16
WINDOW = 512
FORCE_SCORE = 1.0e4
GM_WIDTH = 512
GM_GROUPS = 8
CHUNK = 128
CV_WIDTH = 512
CV_K = 31
SB_HEADS = 8
SB_WIDTH = 512
MEM_HEADS = 4
MEM_HD = 256
D_FF = 2816
FFN_K = 3
PAGE = 128
ROPE_THETA = 10000.0
EPS = 1e-6
TINY = 1e-30
IN_E_PAD = 2432
NEG = -1e30
LOG2E = 1.4426950408889634
SB_EXIT = -120.0

LANES = 128
V7X_VMEM_LIMIT = 56 * 1024 * 1024


def _params(sem):
    return pltpu.CompilerParams(dimension_semantics=sem, vmem_limit_bytes=V7X_VMEM_LIMIT)


def _const_spec(shape):
    nd = len(shape)
    return pl.BlockSpec(shape, lambda *_: (0,) * nd)


def _dot(a, b):
    return jnp.dot(a, b, preferred_element_type=F32)


def _dot_nt(a, b):
    return lax.dot_general(a, b, (((1,), (1,)), ((), ())), preferred_element_type=F32)


def _split_bf16(x):
    hi = x.astype(BF16)
    lo = (x - hi.astype(F32)).astype(BF16)
    return hi, lo


def _dot_hilo(x, w):
    hi, lo = _split_bf16(x)
    return _dot(hi, w) + _dot(lo, w)


def _rms_rows(x, g):
    return x * lax.rsqrt(jnp.mean(x * x, axis=-1, keepdims=True) + EPS) * g


def _seg_rms(x, ones_bd, g, width):
    ss = _dot((x * x).astype(BF16), ones_bd)
    return x * lax.rsqrt(ss * (1.0 / width) + EPS) * g


def _sigmoid(x):
    return 1.0 / (1.0 + jnp.exp(-x))


def _gelu_tanh(x):
    return 0.5 * x * (1.0 + jnp.tanh(0.7978845608028654 * (x + 0.044715 * (x * x * x))))


def _rope_blk(x, cos_t, sin_t, lo32):
    sw = jnp.where(lo32, pltpu.roll(x, 96, 1), pltpu.roll(x, 32, 1))
    return x * cos_t + sw * sin_t


def _inproj_even_kernel(x_ref, g_ref, w_ref, cos_ref, sin_ref, gq_ref, gk_ref, ln_ref, ones_ref,
                        wsg_ref, sbt_ref,
                        qp_ref, qr_ref, rows_ref, win_ref, gates_ref, v_ref, ogm_ref):
    x = x_ref[...]
    tm = x.shape[0]
    h = _rms_rows(x, g_ref[...]).astype(BF16)
    z = _dot(h, w_ref[...])
    lane = lax.broadcasted_iota(jnp.int32, (tm, LANES), 1)
    lo32 = (lane & 32) == 0
    lo64 = lane < 64
    cos_t = cos_ref[...]
    sin_t = sin_ref[...]
    ones128 = ones_ref[0:LANES, 0:LANES]
    gk = gk_ref[...]

    q = _seg_rms(z[:, 0:512], ones_ref[...], gq_ref[...], HEAD_DIM)
    qp_ref[...] = q
    for i in range(4):
        sl = slice(i * LANES, (i + 1) * LANES)
        qr_ref[:, sl] = _rope_blk(q[:, sl], cos_t, sin_t, lo32)

    rows_ref[:, 0:128] = _seg_rms(z[:, 512:640], ones128, gk[0:1], HEAD_DIM)
    rows_ref[:, 128:256] = z[:, 640:768]
    ks_k = _seg_rms(z[:, 768:896], ones128, gk[1:2], HEAD_DIM)
    rows_ref[:, 256:384] = _rope_blk(ks_k, cos_t, sin_t, lo32)
    rows_ref[:, 384:512] = z[:, 896:1024]
    kw_k = _seg_rms(z[:, 1024:1152], ones128, gk[2:3], HEAD_DIM)
    win_ref[:, 0:128] = _rope_blk(kw_k, cos_t, sin_t, lo32)
    win_ref[:, 128:256] = z[:, 1152:1280]

    gates_ref[...] = _sigmoid(z[:, 1280:1408])

    u = _gelu_tanh(z[:, 1408:1920])
    vv = _gelu_tanh(z[:, 1920:2432])
    ln = ln_ref[...]
    vc = vv - jnp.mean(vv, axis=-1, keepdims=True)
    v = vc * lax.rsqrt(jnp.mean(vc * vc, axis=-1, keepdims=True) + EPS) * ln[0:1] + ln[1:2]
    v_ref[...] = v
    vb = v.astype(BF16)
    for i in range(4):
        sl = slice(i * LANES, (i + 1) * LANES)
        m_lo = _dot(wsg_ref[2 * i], vb[:, sl])
        m_hi = _dot(wsg_ref[2 * i + 1], vb[:, sl])
        mixed = jnp.where(lo64, m_lo, m_hi) + sbt_ref[:, sl]
        ogm_ref[:, sl] = u[:, sl] * mixed


def _inproj_even(x, g, w, cos_t, sin_t, gq, gk, ln, ones_bd, wsg, sbt, tm):
    t = x.shape[0]
    row = lambda n: pl.BlockSpec((tm, n), lambda i: (i, 0))
    outs = [(512, F32), (512, F32), (512, F32), (256, F32), (128, F32), (512, F32), (512, F32)]
    return pl.pallas_call(
        _inproj_even_kernel,
        grid=(t // tm,),
        in_specs=[row(D_MODEL), _const_spec((1, D_MODEL)), _const_spec(w.shape), row(128), row(128),
                  _const_spec(gq.shape), _const_spec(gk.shape), _const_spec(ln.shape),
                  _const_spec(ones_bd.shape), _const_spec(wsg.shape), _const_spec(sbt.shape)],
        out_specs=[row(n) for n, _ in outs],
        out_shape=[jax.ShapeDtypeStruct((t, n), d) for n, d in outs],
        compiler_params=_params(("arbitrary",)),
        name="inproj_even",
    )(x, g, w, cos_t, sin_t, gq, gk, ln, ones_bd, wsg, sbt)


def _compress_kernel(pt_ref, *refs, npg, nsteps, nch):
    page_refs = refs[:npg]
    perm_ref, wl_ref, pe_ref, out_ref, xl_ref, a2_ref = refs[npg:]
    k = pl.program_id(1)

    @pl.when(k == 0)
    def _():
        for l in range(16):
            xl_ref[l, nch:nch + 16, :] = pe_ref[l]

    perm = perm_ref[...]
    for pp in range(npg // 2):
        ra, rb = page_refs[2 * pp], page_refs[2 * pp + 1]
        if len(ra.shape) == 2:
            pg = jnp.concatenate([ra[...], rb[...]], axis=0).astype(BF16)
            xp = _dot(perm, pg)
        else:
            pg = jnp.concatenate([ra[...].reshape(256, PAGE), rb[...].reshape(256, PAGE)], axis=1).astype(BF16)
            xp = _dot_nt(perm, pg)
        xp = xp.astype(BF16)
        row0 = pl.multiple_of((k * (npg // 2) + pp) * 16, 16)
        for l in range(16):
            xl_ref[l, pl.ds(row0, 16), :] = xp[l * 16:(l + 1) * 16]

    @pl.when(k == nsteps - 1)
    def _():
        acc = _dot(xl_ref[0], wl_ref[0])
        for l in range(1, 16):
            acc = acc + _dot(xl_ref[l], wl_ref[l])
        bias = (acc[nch:nch + 1, 0:256] + acc[nch + 2:nch + 3, 0:256]
                + acc[nch + 1:nch + 2, 256:512] + acc[nch + 3:nch + 4, 256:512])
        a2_ref[0:nch, :] = acc[0:nch, 256:512]
        a2_ref[nch:nch + 8, :] = jnp.zeros((8, 256), F32)
        out_ref[...] = acc[0:nch, 0:256] + a2_ref[1:nch + 1, :] + bias


def _page_spec(pool, layer, half, phys_fn):
    h = 0 if half is None else half
    if pool.ndim == 3:
        w = pool.shape[2] if half is None else pool.shape[2] // 2
        return pl.BlockSpec((None, PAGE, w), lambda *a: (phys_fn(*a), 0, h))
    ty = pool.shape[2] if half is None else pool.shape[2] // 2
    return pl.BlockSpec((None, None, ty) + tuple(pool.shape[3:]), lambda *a: (layer, phys_fn(*a), h, 0, 0, 0))


def _compress(pool, layer, page_table, perm, wl, pe_rows):
    nseq, npages = page_table.shape
    npg = 16 if npages % 16 == 0 else 8
    nsteps = npages // npg
    nch = npages * (PAGE // CMP_STRIDE)

    def page_spec(p):
        return _page_spec(pool, layer, 0, lambda s, k, pt: pt[s, k * npg + p])

    grid_spec = pltpu.PrefetchScalarGridSpec(
        num_scalar_prefetch=1,
        grid=(nseq, nsteps),
        in_specs=[page_spec(p) for p in range(npg)] + [
            pl.BlockSpec(perm.shape, lambda s, k, pt: (0, 0)),
            pl.BlockSpec(wl.shape, lambda s, k, pt: (0, 0, 0)),
            pl.BlockSpec(pe_rows.shape, lambda s, k, pt: (0, 0, 0))],
        out_specs=pl.BlockSpec((None, nch, 256), lambda s, k, pt: (s, 0, 0)),
        scratch_shapes=[pltpu.VMEM((16, nch + 16, 256), BF16), pltpu.VMEM((nch + 8, 256), F32)],
    )
    return pl.pallas_call(
        functools.partial(_compress_kernel, npg=npg, nsteps=nsteps, nch=nch),
        grid_spec=grid_spec,
        out_shape=jax.ShapeDtypeStruct((nseq, nch, 256), F32),
        compiler_params=_params(("arbitrary", "arbitrary")),
        name="nsa_compress",
    )(page_table, *([pool] * npg), perm, wl, pe_rows)


def _stack_q(q, lo64):
    blks = [q[:, i * LANES:(i + 1) * LANES] for i in range(4)]
    slabs = [jnp.where(lo64, b, 0.0) for b in blks] + [jnp.where(lo64, 0.0, b) for b in blks]
    return jnp.concatenate(slabs, axis=0).astype(BF16)


def _unstack_o(o, tq, lo64):
    return [jnp.where(lo64, o[i * tq:(i + 1) * tq], o[(4 + i) * tq:(5 + i) * tq]) for i in range(4)]


def _gate_expand(gates, gexp):
    return _dot_hilo(gates, gexp)


def _masked_softmax_rows(s, mask):
    s = jnp.where(mask, s, NEG)
    m = jnp.max(s, axis=-1, keepdims=True)
    p = jnp.where(mask, jnp.exp(s - m), 0.0)
    return p / jnp.maximum(jnp.sum(p, axis=-1, keepdims=True), TINY)


def _cmp_attn_kernel(qp_ref, cmp_ref, gates_ref, msel_ref, gexp_ref, oc_ref, sel_ref, *,
                     tq, pos0, nc, ns, nsp, nch, k_eff):
    qb = pl.program_id(1)
    lane = lax.broadcasted_iota(jnp.int32, (tq, LANES), 1)
    lo64 = lane < 64
    qs = _stack_q(qp_ref[...], lo64)
    cm = cmp_ref[...]
    kc = cm[:, 0:128].astype(BF16)
    vc = cm[:, 128:256].astype(BF16)
    s_all = _dot_nt(qs, kc) * (HEAD_DIM ** -0.5)
    qpos = pos0 + qb * tq + lax.broadcasted_iota(jnp.int32, (tq, 1), 0)
    nidx = lax.broadcasted_iota(jnp.int32, (tq, nch), 1)
    cmask = (nidx * CMP_STRIDE + (CMP_LEN - 1) <= qpos) & (nidx < nc)

    msel = msel_ref[...]
    jidx = lax.broadcasted_iota(jnp.int32, (tq, nsp), 1)
    jf = jidx.astype(F32)
    causal_ok = jidx * SEL_LEN <= qpos
    forced = (jidx == jnp.right_shift(qpos, 6)) | (jidx == 0)
    o_slabs = []
    for kv in range(NSA_KV):
        pg = jnp.zeros((tq, nch), F32)
        for g in range(4):
            r0 = (kv * 4 + g) * tq
            p = _masked_softmax_rows(s_all[r0:r0 + tq], cmask)
            pg = pg + p
            o_slabs.append(_dot(p.astype(BF16), vc))
        p_slc = _dot_hilo(pg, msel)
        score = jnp.where(causal_ok, jnp.where(forced, FORCE_SCORE, p_slc), -1.0)
        score = jnp.where(jidx < ns, score, -3.0e38)
        work = score
        sel = jnp.zeros((tq, nsp), F32)
        for _ in range(k_eff):
            m = jnp.max(work, axis=-1, keepdims=True)
            idx = jnp.min(jnp.where(work == m, jf, 1.0e9), axis=-1, keepdims=True)
            hit = jf == idx
            sel = jnp.where(hit, 1.0, sel)
            work = jnp.where(hit, -jnp.inf, work)
        sel_ref[:, kv * nsp:(kv + 1) * nsp] = jnp.where(score >= 0.0, sel, 0.0)
    o = jnp.concatenate(o_slabs, axis=0)
    gx = _gate_expand(gates_ref[...], gexp_ref[...])
    for i, ob in enumerate(_unstack_o(o, tq, lo64)):
        sl = slice(i * LANES, (i + 1) * LANES)
        oc_ref[:, sl] = gx[:, sl] * ob


def _cmp_attn(qp, cmp, gates, msel, gexp, *, nseq, tq, pos0, nc, ns, k_eff):
    t = qp.shape[0]
    nqb = t // (nseq * tq)
    nch = cmp.shape[1]
    nsp = msel.shape[1]
    row = lambda n: pl.BlockSpec((tq, n), lambda s, b: (s * nqb + b, 0))
    return pl.pallas_call(
        functools.partial(_cmp_attn_kernel, tq=tq, pos0=pos0, nc=nc, ns=ns, nsp=nsp, nch=nch, k_eff=k_eff),
        grid=(nseq, nqb),
        in_specs=[row(512), pl.BlockSpec((None, nch, 256), lambda s, b: (s, 0, 0)), row(128),
                  _const_spec(msel.shape), _const_spec(gexp.shape)],
        out_specs=[row(512), row(2 * nsp)],
        out_shape=[jax.ShapeDtypeStruct((t, 512), F32), jax.ShapeDtypeStruct((t, 2 * nsp), F32)],
        compiler_params=_params(("arbitrary", "arbitrary")),
        name="nsa_cmp_select",
    )(qp, cmp, gates, msel, gexp)


def _win_attn_kernel(*refs, tq, pos0, kblocks):
    nkb = len(kblocks)
    qr_ref, gates_ref, gexp_ref, oin_ref = refs[:4]
    k_refs = refs[4:4 + nkb]
    out_ref = refs[4 + nkb]
    qb = pl.program_id(1)
    lane = lax.broadcasted_iota(jnp.int32, (tq, LANES), 1)
    lo64 = lane < 64
    qs = _stack_q(qr_ref[...], lo64)
    kv = jnp.concatenate([r[...] for r in k_refs], axis=0)
    kk = kv[:, 0:128].astype(BF16)
    vv = kv[:, 128:256].astype(BF16)
    qpos = pos0 + qb * tq + lax.broadcasted_iota(jnp.int32, (tq, 1), 0)
    kpos = jnp.concatenate(
        [c0 + c1 * qb + lax.broadcasted_iota(jnp.int32, (tq, n), 1) for (c0, c1, n) in kblocks], axis=1)
    mask = (kpos <= qpos) & (kpos > qpos - WINDOW) & (kpos >= 0)
    s_all = _dot_nt(qs, kk) * (HEAD_DIM ** -0.5)
    o_slabs = []
    for r in range(8):
        p = _masked_softmax_rows(s_all[r * tq:(r + 1) * tq], mask)
        o_slabs.append(_dot(p.astype(BF16), vv))
    o = jnp.concatenate(o_slabs, axis=0)
    gx = _gate_expand(gates_ref[...], gexp_ref[...])
    for i, ob in enumerate(_unstack_o(o, tq, lo64)):
        sl = slice(i * LANES, (i + 1) * LANES)
        out_ref[:, sl] = oin_ref[:, sl] + gx[:, sl] * ob


def _win_attn(qr, gates, gexp, o_in, key_arrays, key_specs, kblocks, *, nseq, tq, pos0):
    t = qr.shape[0]
    nqb = t // (nseq * tq)
    row = lambda n: pl.BlockSpec((tq, n), lambda s, b: (s * nqb + b, 0))
    return pl.pallas_call(
        functools.partial(_win_attn_kernel, tq=tq, pos0=pos0, kblocks=tuple(kblocks)),
        grid=(nseq, nqb),
        in_specs=[row(512), row(128), _const_spec(gexp.shape), row(512)] + list(key_specs),
        out_specs=row(512),
        out_shape=jax.ShapeDtypeStruct((t, 512), F32),
        compiler_params=_params(("arbitrary", "arbitrary")),
        name="nsa_window",
    )(qr, gates, gexp, o_in, *key_arrays)


def _sel_schedule(nqb, npg, c0, c1):
    page_steps = lambda b: (c0 + c1 * b + npg - 1) // npg
    if c1 == 0 or nqb % 2:
        return nqb, page_steps(0 if c1 == 0 else nqb - 1) + 1, lambda p, k: (p, k, page_steps(p))
    total = max(page_steps(b) + page_steps(nqb - 1 - b) + 2 for b in range(nqb // 2))

    def sched(p, k):
        n1 = page_steps(p) + 1
        first = k < n1
        b = jnp.where(first, p, nqb - 1 - p)
        return b, jnp.where(first, k, k - n1), page_steps(b)

    return nqb // 2, total, sched


def _sel_attn_kernel(pt_ref, *refs, tq, pos0, npg, nsp, c0, c1, sched):
    qr_ref, sel_ref, gates_ref, gexp_ref, oin_ref, tail_ref, e_ref, etail_ref = refs[:8]
    page_refs = refs[8:8 + npg]
    out_ref, qs_ref, m_ref, acc_ref = refs[8 + npg:]
    qb, k, na = sched(pl.program_id(1), pl.program_id(2))
    npast = c0 + c1 * qb
    lane = lax.broadcasted_iota(jnp.int32, (tq, LANES), 1)
    lo64 = lane < 64

    @pl.when(k == 0)
    def _():
        qs_ref[...] = _stack_q(qr_ref[...] * (LOG2E * HEAD_DIM ** -0.5), lo64)
        m_ref[...] = jnp.full(m_ref.shape, NEG, F32)
        acc_ref[...] = jnp.zeros(acc_ref.shape, F32)

    def update(s_all, pv, biases):
        nk = s_all.shape[1]
        for kvi in range(NSA_KV):
            rows = slice(kvi * 4 * tq, (kvi + 1) * 4 * tq)
            s = (s_all[rows].reshape(4, tq, nk) + biases[kvi][None]).reshape(4 * tq, nk)
            m_old = m_ref[rows]
            m_new = jnp.maximum(m_old, jnp.max(s, axis=-1, keepdims=True))
            p = jnp.exp2(s - m_new)
            acc_ref[rows] = jnp.exp2(m_old - m_new) * acc_ref[rows] + pv(p.astype(BF16), kvi)
            m_ref[rows] = m_new

    def with_ones(v, axis):
        first = lax.broadcasted_iota(jnp.int32, v.shape, axis) < HEAD_DIM
        return jnp.where(first, v, 1.0).astype(BF16), jnp.where(first, 1.0, v).astype(BF16)

    def block_biases(e, extra):
        out = []
        for kvi in range(NSA_KV):
            sel = sel_ref[:, kvi * nsp:(kvi + 1) * nsp].astype(BF16)
            b = (_dot(sel, e) - 1.0) * (-NEG)
            out.append(b if extra is None else jnp.where(extra, b, NEG))
        return out

    def rows_update(kv, biases):
        vs = with_ones(kv[:, 128:256], 1)
        update(_dot_nt(qs_ref[...], kv[:, 0:128].astype(BF16)), lambda p, kvi: _dot(p, vs[kvi]), biases)

    @pl.when(k < na)
    def _():
        nk = npg * PAGE
        if c1 == 0 and c0 % npg == 0:
            extra = None
        else:
            extra = k * npg + jnp.right_shift(lax.broadcasted_iota(jnp.int32, (tq, nk), 1), 7) < npast
        biases = block_biases(e_ref[...], extra)
        if len(page_refs[0].shape) == 2:
            rows_update(jnp.concatenate([r[...] for r in page_refs], axis=0), biases)
        else:
            kt = jnp.concatenate([r[0].reshape(LANES, PAGE) for r in page_refs], axis=1).astype(BF16)
            vts = with_ones(jnp.concatenate([r[1].reshape(LANES, PAGE) for r in page_refs], axis=1), 0)
            update(_dot(qs_ref[...], kt), lambda p, kvi: _dot_nt(p, vts[kvi]), biases)

    @pl.when(k == na)
    def _():
        qpos = pos0 + qb * tq + lax.broadcasted_iota(jnp.int32, (tq, 1), 0)
        kpos = npast * PAGE + lax.broadcasted_iota(jnp.int32, (tq, PAGE), 1)
        rows_update(tail_ref[...], block_biases(etail_ref[...], kpos <= qpos))
        acc = acc_ref[...]
        o = jnp.where(m_ref[...] > 0.5 * NEG, acc / jnp.maximum(pltpu.roll(acc, HEAD_DIM, 1), TINY), 0.0)
        gx = _gate_expand(gates_ref[...], gexp_ref[...])
        for i, ob in enumerate(_unstack_o(o, tq, lo64)):
            sl = slice(i * LANES, (i + 1) * LANES)
            out_ref[:, sl] = oin_ref[:, sl] + gx[:, sl] * ob


def _sel_attn(qr, sel, gates, gexp, o_in, tail3, pool, layer, page_table, *, nseq, tq, pos0, c0, c1, npg):
    t = qr.shape[0]
    nqb = t // (nseq * tq)
    npages = page_table.shape[1]
    assert npages % npg == 0
    nsp = sel.shape[1] // 2
    nrows, nsteps, sched = _sel_schedule(nqb, npg, c0, c1)
    blk = lambda p, k: sched(p, k)[0]
    row = lambda n: pl.BlockSpec((tq, n), lambda s, p, k, pt: (s * nqb + blk(p, k), 0))
    ncol = (npages + 1) * PAGE
    e_all = (jnp.arange(ncol)[None, :] // SEL_LEN == jnp.arange(nsp)[:, None]).astype(BF16)

    def page_spec(slot):
        def phys(s, p, k, pt):
            b, kl, _ = sched(p, k)
            last = jnp.maximum(c0 + c1 * b - 1, 0)
            return pt[s, jnp.minimum(jnp.minimum(kl * npg + slot, last), npages - 1)]
        return _page_spec(pool, layer, 1, phys)

    def e_index(s, p, k, pt):
        _, kl, na = sched(p, k)
        return (0, jnp.clip(kl, 0, jnp.maximum(na - 1, 0)))

    grid_spec = pltpu.PrefetchScalarGridSpec(
        num_scalar_prefetch=1,
        grid=(nseq, nrows, nsteps),
        in_specs=[row(512), row(2 * nsp), row(128),
                  pl.BlockSpec(gexp.shape, lambda s, p, k, pt: (0, 0)), row(512),
                  pl.BlockSpec((None, PAGE, 256), lambda s, p, k, pt: (s * nqb + blk(p, k), 0, 1)),
                  pl.BlockSpec((nsp, npg * PAGE), e_index),
                  pl.BlockSpec((nsp, PAGE), lambda s, p, k, pt: (0, c0 + c1 * blk(p, k)))]
                 + [page_spec(slot) for slot in range(npg)],
        out_specs=row(512),
        scratch_shapes=[pltpu.VMEM((8 * tq, LANES), BF16), pltpu.VMEM((8 * tq, 1), F32),
                        pltpu.VMEM((8 * tq, LANES), F32)],
    )
    return pl.pallas_call(
        functools.partial(_sel_attn_kernel, tq=tq, pos0=pos0, npg=npg, nsp=nsp, c0=c0, c1=c1, sched=sched),
        grid_spec=grid_spec,
        out_shape=jax.ShapeDtypeStruct((t, 512), F32),
        compiler_params=_params(("arbitrary", "arbitrary", "arbitrary")),
        name="nsa_selected",
    )(page_table, qr, sel, gates, gexp, o_in, tail3, e_all, e_all, *([pool] * npg))


def _rms_matmul_kernel(x_ref, g_ref, w_ref, *rest, n_norm):
    if n_norm:
        gseg_ref, out_ref = rest
    else:
        (out_ref,) = rest
    h = _rms_rows(x_ref[...], g_ref[...]).astype(BF16)
    z = _dot(h, w_ref[...])
    if n_norm:
        for c0 in range(0, n_norm, MEM_HD):
            out_ref[:, c0:c0 + MEM_HD] = _rms_rows(z[:, c0:c0 + MEM_HD], gseg_ref[...])
        if n_norm < z.shape[1]:
            out_ref[:, n_norm:] = z[:, n_norm:]
    else:
        out_ref[...] = z


def _rms_matmul(x, g, w, tm, tn, n_norm=0, gseg=None):
    t, n = x.shape[0], w.shape[1]
    extra, extra_specs = [], []
    if n_norm:
        assert tn == n
        extra = [gseg]
        extra_specs = [_const_spec(gseg.shape)]
    return pl.pallas_call(
        functools.partial(_rms_matmul_kernel, n_norm=n_norm),
        grid=(t // tm, n // tn),
        in_specs=[pl.BlockSpec((tm, D_MODEL), lambda i, j: (i, 0)), _const_spec((1, D_MODEL)),
                  pl.BlockSpec((D_MODEL, tn), lambda i, j: (0, j))] + extra_specs,
        out_specs=pl.BlockSpec((tm, tn), lambda i, j: (i, j)),
        out_shape=jax.ShapeDtypeStruct((t, n), F32),
        compiler_params=_params(("arbitrary", "arbitrary")),
        name="rms_matmul",
    )(x, g, w, *extra)


def _matmul_res_kernel(*refs, na):
    a_refs = refs[:na]
    w_ref, x_ref, out_ref = refs[na:]
    acc = x_ref[...]
    k0 = 0
    for a_ref in a_refs:
        kw = a_ref.shape[1]
        acc = acc + _dot(a_ref[...].astype(BF16), w_ref[k0:k0 + kw, :])
        k0 += kw
    out_ref[...] = acc


def _matmul_res(acts, w, x, tm):
    t = x.shape[0]
    return pl.pallas_call(
        functools.partial(_matmul_res_kernel, na=len(acts)),
        grid=(t // tm,),
        in_specs=[pl.BlockSpec((tm, a.shape[1]), lambda i: (i, 0)) for a in acts]
                 + [_const_spec(w.shape), pl.BlockSpec((tm, D_MODEL), lambda i: (i, 0))],
        out_specs=pl.BlockSpec((tm, D_MODEL), lambda i: (i, 0)),
        out_shape=jax.ShapeDtypeStruct((t, D_MODEL), F32),
        compiler_params=_params(("arbitrary",)),
        name="matmul_residual",
    )(*acts, w, x)


def _cross_attn_kernel(q_ref, mkv_ref, out_ref):
    for h in range(MEM_HEADS):
        sl = slice(h * MEM_HD, (h + 1) * MEM_HD)
        q = q_ref[:, sl].astype(BF16)
        kk = mkv_ref[:, sl].astype(BF16)
        vv = mkv_ref[:, D_MODEL + h * MEM_HD:D_MODEL + (h + 1) * MEM_HD].astype(BF16)
        s = _dot_nt(q, kk) * (MEM_HD ** -0.5)
        m = jnp.max(s, axis=-1, keepdims=True)
        p = jnp.exp(s - m)
        p = p / jnp.sum(p, axis=-1, keepdims=True)
        out_ref[:, sl] = _dot(p.astype(BF16), vv)


def _cross_attn(q, mkv, *, nseq, tq):
    t = q.shape[0]
    nqb = t // (nseq * tq)
    mlen = mkv.shape[1]
    return pl.pallas_call(
        _cross_attn_kernel,
        grid=(nseq, nqb),
        in_specs=[pl.BlockSpec((tq, D_MODEL), lambda s, b: (s * nqb + b, 0)),
                  pl.BlockSpec((None, mlen, 2 * D_MODEL), lambda s, b: (s, 0, 0))],
        out_specs=pl.BlockSpec((tq, D_MODEL), lambda s, b: (s * nqb + b, 0)),
        out_shape=jax.ShapeDtypeStruct((t, D_MODEL), F32),
        compiler_params=_params(("arbitrary", "arbitrary")),
        name="cross_attn",
    )(q, mkv)


def _convffn_kernel(x_ref, g_ref, wa_ref, wg_ref, dwa_ref, dwg_ref, wd_ref,
                    out_ref, sta_ref, stg_ref,
                    hn_ref, acc_ref, sa_ref, sg_ref, ca_ref, cg_ref, *, tm, nff):
    i = pl.program_id(0)
    j = pl.program_id(1)

    @pl.when(j == 0)
    def _():
        hn_ref[...] = _rms_rows(x_ref[...], g_ref[...]).astype(BF16)
        acc_ref[...] = jnp.zeros(acc_ref.shape, F32)

    @pl.when(i == 0)
    def _():
        ca_ref[j] = jnp.zeros(ca_ref.shape[1:], F32)
        cg_ref[j] = jnp.zeros(cg_ref.shape[1:], F32)

    hn = hn_ref[...]

    def conv(w_ref, s_ref, c_ref, dw_ref, st_ref):
        u = _dot(hn, w_ref[...])
        s_ref[0:8, :] = c_ref[j]
        s_ref[8:8 + tm, :] = u
        c_ref[j] = u[tm - 8:tm]
        st_ref[...] = u[tm - 8:tm]
        dw = dw_ref[...]
        return dw[0:1] * s_ref[6:6 + tm, :] + dw[1:2] * s_ref[7:7 + tm, :] + dw[2:3] * u + dw[3:4]

    a = conv(wa_ref, sa_ref, ca_ref, dwa_ref, sta_ref)
    g = conv(wg_ref, sg_ref, cg_ref, dwg_ref, stg_ref)
    y = (g * _sigmoid(g)) * a
    acc_ref[...] += _dot(y.astype(BF16), wd_ref[...])

    @pl.when(j == nff - 1)
    def _():
        out_ref[...] = x_ref[...] + acc_ref[...]


def _convffn(x, g, w_up, dwb, w_down, tm, tf):
    t = x.shape[0]
    nff = D_FF // tf
    nt = t // tm
    return pl.pallas_call(
        functools.partial(_convffn_kernel, tm=tm, nff=nff),
        grid=(nt, nff),
        in_specs=[pl.BlockSpec((tm, D_MODEL), lambda i, j: (i, 0)), _const_spec((1, D_MODEL)),
                  pl.BlockSpec((D_MODEL, tf), lambda i, j: (0, j)),
                  pl.BlockSpec((D_MODEL, tf), lambda i, j: (0, nff + j)),
                  pl.BlockSpec((8, tf), lambda i, j: (0, j)),
                  pl.BlockSpec((8, tf), lambda i, j: (0, nff + j)),
                  pl.BlockSpec((tf, D_MODEL), lambda i, j: (j, 0))],
        out_specs=[pl.BlockSpec((tm, D_MODEL), lambda i, j: (i, 0)),
                   pl.BlockSpec((None, 8, tf), lambda i, j: (i, 0, j)),
                   pl.BlockSpec((None, 8, tf), lambda i, j: (i, 0, j))],
        out_shape=[jax.ShapeDtypeStruct((t, D_MODEL), F32),
                   jax.ShapeDtypeStruct((nt, 8, D_FF), F32),
                   jax.ShapeDtypeStruct((nt, 8, D_FF), F32)],
        scratch_shapes=[pltpu.VMEM((tm, D_MODEL), BF16), pltpu.VMEM((tm, D_MODEL), F32),
                        pltpu.VMEM((tm + 8, tf), F32), pltpu.VMEM((tm + 8, tf), F32),
                        pltpu.VMEM((nff, 8, tf), F32), pltpu.VMEM((nff, 8, tf), F32)],
        compiler_params=_params(("arbitrary", "arbitrary")),
        name="convffn",
    )(x, g, w_up, w_up, dwb, dwb, w_down)


def _convgate_down_kernel(a0_ref, a1_ref, a2_ref, g0_ref, g1_ref, g2_ref, dwa_ref, dwg_ref, wd_ref, x_ref,
                          out_ref, acc_ref, *, nff):
    j = pl.program_id(0)

    @pl.when(j == 0)
    def _():
        acc_ref[...] = jnp.zeros(acc_ref.shape, F32)

    dwa = dwa_ref[...]
    dwg = dwg_ref[...]
    a = dwa[0:1] * a0_ref[...] + dwa[1:2] * a1_ref[...] + dwa[2:3] * a2_ref[...] + dwa[3:4]
    g = dwg[0:1] * g0_ref[...] + dwg[1:2] * g1_ref[...] + dwg[2:3] * g2_ref[...] + dwg[3:4]
    y = (g * _sigmoid(g)) * a
    acc_ref[...] += _dot(y.astype(BF16), wd_ref[...])

    @pl.when(j == nff - 1)
    def _():
        out_ref[...] = x_ref[...] + acc_ref[...]


def _convgate_down(f0, f1, f2, dwb, w_down, x, tf):
    t = x.shape[0]
    nff = D_FF // tf
    fa = pl.BlockSpec((t, tf), lambda j: (0, j))
    fg = pl.BlockSpec((t, tf), lambda j: (0, nff + j))
    return pl.pallas_call(
        functools.partial(_convgate_down_kernel, nff=nff),
        grid=(nff,),
        in_specs=[fa, fa, fa, fg, fg, fg,
                  pl.BlockSpec((8, tf), lambda j: (0, j)), pl.BlockSpec((8, tf), lambda j: (0, nff + j)),
                  pl.BlockSpec((tf, D_MODEL), lambda j: (j, 0)), _const_spec((t, D_MODEL))],
        out_specs=_const_spec((t, D_MODEL)),
        out_shape=jax.ShapeDtypeStruct((t, D_MODEL), F32),
        scratch_shapes=[pltpu.VMEM((t, D_MODEL), F32)],
        compiler_params=_params(("arbitrary",)),
        name="convgate_down",
    )(f0, f1, f2, f0, f1, f2, dwb, dwb, w_down, x)


def _inproj_odd_kernel(x_ref, g_ref, w_ref, glu_ref, q_ref, rows_ref):
    h = _rms_rows(x_ref[...], g_ref[...]).astype(BF16)
    z = _dot(h, w_ref[...])
    glu_ref[...] = z[:, 0:512] * _sigmoid(z[:, 512:1024])
    q_ref[...] = z[:, 1024:1536]
    rows_ref[...] = z[:, 1536:2560]


def _inproj_odd(x, g, w, tm):
    t = x.shape[0]
    row = lambda n: pl.BlockSpec((tm, n), lambda i: (i, 0))
    return pl.pallas_call(
        _inproj_odd_kernel,
        grid=(t // tm,),
        in_specs=[row(D_MODEL), _const_spec((1, D_MODEL)), _const_spec(w.shape)],
        out_specs=[row(512), row(512), row(1024)],
        out_shape=[jax.ShapeDtypeStruct((t, 512), F32), jax.ShapeDtypeStruct((t, 512), F32),
                   jax.ShapeDtypeStruct((t, 1024), F32)],
        compiler_params=_params(("arbitrary",)),
        name="inproj_odd",
    )(x, g, w)


def _conv_module_kernel(prev_ref, cur_ref, dw_ref, aux_ref, out_ref, s_ref, *, tm, zero_first, rb):
    i = pl.program_id(1)
    prev = prev_ref[...]
    if zero_first:
        prev = jnp.where(i == 0, 0.0, prev)
    s_ref[0:32, :] = prev
    s_ref[32:32 + tm, :] = cur_ref[...]
    aux = aux_ref[...]
    for r0 in range(0, tm, rb):
        acc = jnp.zeros((rb, CV_WIDTH), F32) + aux[0:1]
        for d in range(CV_K):
            acc = acc + dw_ref[CV_K - 1 - d:CV_K - d, :] * s_ref[32 - d + r0:32 - d + r0 + rb, :]
        c = acc - jnp.mean(acc, axis=-1, keepdims=True)
        y = c * lax.rsqrt(jnp.mean(c * c, axis=-1, keepdims=True) + EPS) * aux[1:2] + aux[2:3]
        out_ref[r0:r0 + rb, :] = y * _sigmoid(y)


def _conv_module(prev3, cur3, dw, aux, *, tm, zero_first):
    nseq, tseg, _ = cur3.shape
    nb = tseg // tm
    if zero_first:
        prev_spec = pl.BlockSpec((None, 32, CV_WIDTH),
                                 lambda s, i: (s, jnp.maximum(i * (tm // 32) - 1, 0), 0))
    else:
        prev_spec = pl.BlockSpec((None, 32, CV_WIDTH), lambda s, i: (s, 0, 0))
    return pl.pallas_call(
        functools.partial(_conv_module_kernel, tm=tm, zero_first=zero_first, rb=min(tm, 32)),
        grid=(nseq, nb),
        in_specs=[prev_spec, pl.BlockSpec((None, tm, CV_WIDTH), lambda s, i: (s, i, 0)),
                  _const_spec(dw.shape), _const_spec(aux.shape)],
        out_specs=pl.BlockSpec((None, tm, CV_WIDTH), lambda s, i: (s, i, 0)),
        out_shape=jax.ShapeDtypeStruct((nseq, tseg, CV_WIDTH), F32),
        scratch_shapes=[pltpu.VMEM((tm + 32, CV_WIDTH), F32)],
        compiler_params=_params(("arbitrary", "arbitrary")),
        name="conv_module",
    )(prev3, cur3, dw, aux)


def _sb_kernel(pt_ref, q_ref, tail_ref, p1_ref, p2_ref, ucat_ref, pool_ref, out_ref,
               carry_ref, acc_ref, buf_ref, sem, *, tq, pos0, c0, c1, layer):
    s = pl.program_id(0)
    qb = pl.program_id(1)
    npast = c0 + c1 * qb
    lane = lax.broadcasted_iota(jnp.int32, (tq, LANES), 1)
    lo64 = lane < 64
    q = q_ref[...]
    qs = []
    for i in range(4):
        blk = q[:, i * LANES:(i + 1) * LANES]
        qs.append(jnp.concatenate([jnp.where(lo64, blk, 0.0), jnp.where(lo64, 0.0, blk)], axis=0).astype(BF16))
    carry_ref[...] = jnp.zeros(carry_ref.shape, F32)
    acc_ref[...] = jnp.zeros(acc_ref.shape, F32)
    ucat = ucat_ref[...]

    def process(kv_ref, mask):
        for i in range(4):
            if len(kv_ref.shape) == 2:
                kb = kv_ref[:, i * LANES:(i + 1) * LANES].astype(BF16)
                vb = kv_ref[:, SB_WIDTH + i * LANES:SB_WIDTH + (i + 1) * LANES].astype(BF16)
                z = _dot_nt(qs[i], kb)
                pv = lambda a: _dot(a, vb)
            else:
                kt = jnp.concatenate([kv_ref[0, 2 * i], kv_ref[0, 2 * i + 1]], axis=0).astype(BF16)
                vt = jnp.concatenate([kv_ref[1, 2 * i], kv_ref[1, 2 * i + 1]], axis=0).astype(BF16)
                z = _dot(qs[i], kt)
                pv = lambda a: _dot_nt(a, vt)
            z = z * (HEAD_DIM ** -0.5)
            log_b = -(jnp.maximum(-z, 0.0) + jnp.log1p(jnp.exp(-jnp.abs(z))))
            l1m = log_b - z
            if mask is not None:
                l1m = jnp.where(mask, l1m, 0.0)
            ac = _dot_hilo(l1m, ucat)
            a = jnp.exp(log_b + ac[:, 0:PAGE] + carry_ref[i])
            if mask is not None:
                a = jnp.where(mask, a, 0.0)
            acc_ref[i] += pv(a.astype(BF16))
            carry_ref[i] += ac[:, PAGE:2 * PAGE]

    def live():
        return jnp.max(carry_ref[...]) >= SB_EXIT

    qpos = pos0 + qb * tq + lax.broadcasted_iota(jnp.int32, (tq, 1), 0)
    qpos2 = jnp.concatenate([qpos, qpos], axis=0)
    kpos = npast * PAGE + lax.broadcasted_iota(jnp.int32, (2 * tq, PAGE), 1)
    process(tail_ref, kpos < qpos2)

    @pl.when(npast >= 1)
    def _():
        process(p1_ref, None)

    @pl.when((npast >= 2) & live())
    def _():
        process(p2_ref, None)

    def body(state):
        p, _ = state
        src = pool_ref.at[pt_ref[s, p]] if layer is None else pool_ref.at[layer, pt_ref[s, p]]
        cp = pltpu.make_async_copy(src, buf_ref, sem)
        cp.start()
        cp.wait()
        process(buf_ref, None)
        return p - 1, live().astype(jnp.int32)

    lax.while_loop(lambda st: (st[0] >= 0) & (st[1] > 0), body, (npast - 3, live().astype(jnp.int32)))

    for i in range(4):
        out_ref[:, i * LANES:(i + 1) * LANES] = jnp.where(lo64, acc_ref[i, 0:tq], acc_ref[i, tq:2 * tq])


def _sb_attn(q, tail3, pool, layer, page_table, ucat, *, nseq, tq, pos0, c0, c1):
    t = q.shape[0]
    nqb = t // (nseq * tq)
    npages = page_table.shape[1]
    page_shape = tuple(pool.shape[1:]) if pool.ndim == 3 else tuple(pool.shape[2:])

    def page_spec(back):
        return _page_spec(pool, layer, None,
                          lambda s, b, pt: pt[s, jnp.clip(c0 + c1 * b - back, 0, npages - 1)])

    grid_spec = pltpu.PrefetchScalarGridSpec(
        num_scalar_prefetch=1,
        grid=(nseq, nqb),
        in_specs=[pl.BlockSpec((tq, SB_WIDTH), lambda s, b, pt: (s * nqb + b, 0)),
                  pl.BlockSpec((None, PAGE, 2 * SB_WIDTH), lambda s, b, pt: (s * nqb + b, 0, 0)),
                  page_spec(1), page_spec(2),
                  pl.BlockSpec(ucat.shape, lambda s, b, pt: (0, 0)),
                  pl.BlockSpec(memory_space=pl.ANY)],
        out_specs=pl.BlockSpec((tq, SB_WIDTH), lambda s, b, pt: (s * nqb + b, 0)),
        scratch_shapes=[pltpu.VMEM((4, 2 * tq, PAGE), F32), pltpu.VMEM((4, 2 * tq, LANES), F32),
                        pltpu.VMEM(page_shape, F32), pltpu.SemaphoreType.DMA(())],
    )
    return pl.pallas_call(
        functools.partial(_sb_kernel, tq=tq, pos0=pos0, c0=c0, c1=c1, layer=None if pool.ndim == 3 else layer),
        grid_spec=grid_spec,
        out_shape=jax.ShapeDtypeStruct((t, SB_WIDTH), F32),
        compiler_params=_params(("arbitrary", "arbitrary")),
        name="stick_breaking",
    )(page_table, q, tail3, pool, pool, ucat, pool)


def _q_perm():
    idx = np.zeros((512,), np.int32)
    for i in range(4):
        for half in range(2):
            for d in range(HEAD_DIM):
                idx[i * 128 + half * 64 + d] = (half * 4 + i) * HEAD_DIM + d
    return idx


def _gate_perm():
    idx = np.zeros((24,), np.int32)
    for c in range(3):
        for i in range(4):
            for half in range(2):
                idx[c * 8 + 2 * i + half] = (half * 4 + i) * 3 + c
    return idx


def _gate_expand_mats():
    g = np.zeros((3, 128, 512), np.float32)
    for c in range(3):
        for i in range(4):
            for half in range(2):
                g[c, c * 8 + 2 * i + half, i * 128 + half * 64:i * 128 + (half + 1) * 64] = 1.0
    return jnp.asarray(g, BF16)


def _block_ones(n, w):
    r = np.arange(n) // w
    return jnp.asarray((r[:, None] == r[None, :]).astype(np.float32), BF16)


def _rope_tables(pos):
    half = HEAD_DIM // 2
    inv = jnp.power(ROPE_THETA, -jnp.arange(half, dtype=F32) / half)
    ang = pos.astype(F32)[:, None] * inv[None, :]
    c, s = jnp.cos(ang), jnp.sin(ang)
    return jnp.concatenate([c, c, c, c], -1), jnp.concatenate([-s, s, -s, s], -1)


def _sel_sum_matrix(nch, nsp, nc):
    n = np.arange(nch)[:, None]
    j = np.arange(nsp)[None, :]
    step = SEL_LEN // CMP_STRIDE
    lo = 1 - CMP_LEN // CMP_STRIDE
    m = (n >= step * j + lo) & (n <= step * j + step - 1) & (n < nc)
    return jnp.asarray(m.astype(np.float32), BF16)


def _chunk_perm():
    p = np.zeros((256, 256), np.float32)
    for l in range(16):
        for c in range(16):
            p[l * 16 + c, c * 16 + l] = 1.0
    return jnp.asarray(p, BF16)


def _compress_weights(cmp_pe, w_cmp):
    wl = jnp.zeros((16, 4, HEAD_DIM, 2, 4, HEAD_DIM), F32)
    for part in range(4):
        for half in range(2):
            wl = wl.at[:, part, :, half, part, :].set(w_cmp[part // 2, half * 16:(half + 1) * 16])
    wl = wl.reshape(16, 256, 512).astype(BF16)
    pe1 = jnp.concatenate([cmp_pe[0, 0:16]] * 2 + [cmp_pe[1, 0:16]] * 2, -1)
    pe2 = jnp.concatenate([cmp_pe[0, 16:32]] * 2 + [cmp_pe[1, 16:32]] * 2, -1)
    h1, l1 = _split_bf16(pe1)
    h2, l2 = _split_bf16(pe2)
    pe_rows = jnp.zeros((16, 16, 256), BF16)
    pe_rows = pe_rows.at[:, 0].set(h1).at[:, 1].set(h2).at[:, 2].set(l1).at[:, 3].set(l2)
    return wl, pe_rows


def _sb_ucat():
    j = np.arange(PAGE)[:, None]
    s = np.arange(PAGE)[None, :]
    u = (j > s).astype(np.float32)
    return jnp.asarray(np.concatenate([u, np.ones((PAGE, PAGE), np.float32)], 1), BF16)


def _pad_rows(a, n):
    return jnp.pad(a, ((0, 0), (0, n - a.shape[1]), (0, 0)))


def _even_layer(xp, xs, i, g0, page_table, cache_nsa, cache_nsa_win, w_in_e, g_qk_nsa, cmp_pe, w_cmp,
                gm_ws, gm_b, gm_ln, w_out_e, past):
    sp = xp.shape[0]
    nseq = page_table.shape[0]
    ts = xs.shape[0] // nseq
    qperm, gperm = _q_perm(), _gate_perm()
    w = w_in_e[i]
    wq = w[:, 0:512][:, qperm]
    wg = jnp.pad(w[:, 1280:1304][:, gperm], ((0, 0), (0, 104)))
    w_all = jnp.concatenate([wq, w[:, 512:1280], wg, w[:, 1304:2328]], -1).astype(BF16)
    gq = jnp.tile(g_qk_nsa[i, 0], 8)[None]
    gk = jnp.stack([jnp.tile(g_qk_nsa[i, r], 2) for r in (1, 2, 3)])
    ones_bd = _block_ones(512, HEAD_DIM)
    gexp = _gate_expand_mats()
    w_out = jnp.concatenate([w_out_e[i][0:512][qperm], w_out_e[i][512:1024]], 0).astype(BF16)
    wl, pe_rows = _compress_weights(cmp_pe[i], w_cmp[i])
    perm = _chunk_perm()

    def inproj(x, pos, tm, rchunk):
        cos_t, sin_t = _rope_tables(pos)
        ws_t = jnp.tril(gm_ws[i])[:, :rchunk, :rchunk]
        eye = jnp.eye(tm // rchunk, dtype=F32)
        wsg = jnp.einsum("ab,gts->gatbs", eye, ws_t).reshape(GM_GROUPS, tm, tm).astype(BF16)
        sbt = jnp.tile(jnp.repeat(gm_b[i][:, :rchunk].T, GM_WIDTH // GM_GROUPS, axis=1), (tm // rchunk, 1))
        return _inproj_even(x, g0, w_all, cos_t, sin_t, gq, gk, gm_ln[i], ones_bd, wsg, sbt, tm)

    tqp = 128
    qp, qr, rows, win, gates, v, ogm = inproj(xp, jnp.arange(sp), 256, CHUNK)
    npp = sp // PAGE
    pt_p = jnp.arange(npp, dtype=jnp.int32)[None]
    pool_p = rows.reshape(npp, PAGE, 512)
    cmp_p = _compress(pool_p, None, pt_p, perm, wl, pe_rows)
    nc = (sp - CMP_LEN) // CMP_STRIDE + 1
    ns = -(-sp // SEL_LEN)
    nsp = -(-ns // LANES) * LANES
    assert nc + 1 <= cmp_p.shape[1]
    msel = _sel_sum_matrix(cmp_p.shape[1], nsp, nc)
    o, sel = _cmp_attn(qp, cmp_p, gates, msel, gexp[0], nseq=1, tq=tqp, pos0=0, nc=nc, ns=ns,
                       k_eff=min(N_SEL, ns))
    nwb = WINDOW // tqp
    kspecs = [pl.BlockSpec((tqp, 256), (lambda s, b, j=j: (jnp.maximum(b - nwb + j, 0), 0)))
              for j in range(nwb + 1)]
    kblocks = [((j - nwb) * tqp, tqp, tqp) for j in range(nwb + 1)]
    o = _win_attn(qr, gates, gexp[2], o, [win] * (nwb + 1), kspecs, kblocks, nseq=1, tq=tqp, pos0=0)
    o = _sel_attn(qr, sel, gates, gexp[1], o, pool_p, pool_p, None, pt_p, nseq=1, tq=tqp, pos0=0, c0=0, c1=1,
                  npg=8)
    xp = _matmul_res([o, ogm], w_out, xp, 512)
    outs_p = (rows, win[sp - min(WINDOW, sp):], v[((sp - 1) // CHUNK) * CHUNK:])

    pos_s = jnp.tile(past + jnp.arange(ts), nseq)
    qp, qr, rows, win, gates, v, ogm = inproj(xs, pos_s, nseq * ts, ts)
    pool_s = jnp.transpose(cache_nsa, (0, 1, 3, 4, 5, 2))
    cmp_s = _compress(pool_s, i, page_table, perm, wl, pe_rows)
    ltot = past + ts
    nc = (ltot - CMP_LEN) // CMP_STRIDE + 1
    ns = -(-ltot // SEL_LEN)
    nsp = -(-ns // LANES) * LANES
    assert nc + 1 <= cmp_s.shape[1] and ts <= PAGE and past % PAGE == 0
    msel = _sel_sum_matrix(cmp_s.shape[1], nsp, nc)
    o, sel = _cmp_attn(qp, cmp_s, gates, msel, gexp[0], nseq=nseq, tq=ts, pos0=past, nc=nc, ns=ns,
                       k_eff=min(N_SEL, ns))
    wb = cache_nsa_win.shape[2]
    win_old = cache_nsa_win[i].reshape(nseq, wb, 256)
    win_new = _pad_rows(win.reshape(nseq, ts, 256), PAGE)
    kspecs = [pl.BlockSpec((None, wb, 256), lambda s, b: (s, 0, 0)),
              pl.BlockSpec((None, PAGE, 256), lambda s, b: (s, 0, 0))]
    kblocks = [(past - wb, 0, wb), (past, 0, PAGE)]
    o = _win_attn(qr, gates, gexp[2], o, [win_old, win_new], kspecs, kblocks, nseq=nseq, tq=ts, pos0=past)
    tail_s = _pad_rows(rows.reshape(nseq, ts, 512), PAGE)
    o = _sel_attn(qr, sel, gates, gexp[1], o, tail_s, pool_s, i, page_table, nseq=nseq, tq=ts, pos0=past,
                  c0=past // PAGE, c1=0, npg=16 if (past // PAGE) % 16 == 0 else 8)
    xs = _matmul_res([o, ogm], w_out, xs, nseq * ts)
    win_s = jnp.concatenate([win_old, win.reshape(nseq, ts, 256)], 1)[:, -wb:]
    outs_s = (rows, win_s, v)
    return xp, xs, outs_p, outs_s


def _odd_layer(xp, xs, i, g0, page_table, cache_sb, state_conv, w_in_o, cv_dw, cv_b, cv_ln, w_out_o, past):
    sp = xp.shape[0]
    nseq = page_table.shape[0]
    ts = xs.shape[0] // nseq
    w_in = w_in_o[i].astype(BF16)
    w_out = w_out_o[i].astype(BF16)
    dw = jnp.pad(cv_dw[i], ((0, 1), (0, 0)))
    aux = jnp.concatenate([cv_b[i][None], cv_ln[i], jnp.zeros((5, CV_WIDTH), F32)], 0)
    ucat = _sb_ucat()

    glu, q, rows = _inproj_odd(xp, g0, w_in, 256)
    o_cv = _conv_module(glu[None], glu[None], dw, aux, tm=256, zero_first=True)[0]
    npp = sp // PAGE
    pt_p = jnp.arange(npp, dtype=jnp.int32)[None]
    pool_p = rows.reshape(npp, PAGE, 2 * SB_WIDTH)
    o_sb = _sb_attn(q, pool_p, pool_p, None, pt_p, ucat, nseq=1, tq=PAGE, pos0=0, c0=0, c1=1)
    xp = _matmul_res([o_cv, o_sb], w_out, xp, 512)
    outs_p = (rows, glu[sp - (CV_K - 1):])

    glu, q, rows = _inproj_odd(xs, g0, w_in, nseq * ts)
    glu3 = glu.reshape(nseq, ts, CV_WIDTH)
    prev = jnp.pad(state_conv[i], ((0, 0), (32 - (CV_K - 1), 0), (0, 0)))
    o_cv = _conv_module(prev, glu3, dw, aux, tm=ts, zero_first=False).reshape(nseq * ts, CV_WIDTH)
    pool_s = jnp.transpose(cache_sb, (0, 1, 3, 4, 5, 2))
    tail_s = _pad_rows(rows.reshape(nseq, ts, 2 * SB_WIDTH), PAGE)
    o_sb = _sb_attn(q, tail_s, pool_s, i, page_table, ucat, nseq=nseq, tq=ts, pos0=past, c0=past // PAGE, c1=0)
    xs = _matmul_res([o_cv, o_sb], w_out, xs, nseq * ts)
    conv_s = jnp.concatenate([state_conv[i], glu3], 1)[:, -(CV_K - 1):]
    outs_s = (rows, conv_s)
    return xp, xs, outs_p, outs_s


def kernel(x_prompt, x_sample, mem_prompt, page_table, cache_nsa, cache_nsa_win, cache_sb, state_conv, state_ffn, cache_mem, g_norm, w_in_e, g_qk_nsa, cmp_pe, w_cmp, gm_ws, gm_b, gm_ln, w_out_e, w_in_o, cv_dw, cv_b, cv_ln, w_out_o, w_xq, w_xkv, g_xqk, w_xo, w_up, ffn_dw, ffn_db, w_down):
    bp, sp, _ = x_prompt.shape
    nseq, ts, _ = x_sample.shape
    assert bp == 1
    depth = g_norm.shape[0]
    past = page_table.shape[1] * PAGE
    mlen = mem_prompt.shape[1]
    xp = x_prompt.reshape(sp, D_MODEL)
    xs = x_sample.reshape(nseq * ts, D_MODEL)
    o = {k: [] for k in ("nsa_rows_p", "nsa_rows_s", "nsa_win_p", "nsa_win_s", "gm_v_p", "gm_v_s",
                         "sb_rows_p", "sb_rows_s", "conv_p", "conv_s", "ffn_p", "ffn_s", "memkv_p")}
    for l in range(depth):
        i = l // 2
        g0 = g_norm[l, 0][None]
        if l % 2 == 0:
            xp, xs, (rows_p, win_p, v_p), (rows_s, win_s, v_s) = _even_layer(
                xp, xs, i, g0, page_table, cache_nsa, cache_nsa_win, w_in_e, g_qk_nsa, cmp_pe, w_cmp,
                gm_ws, gm_b, gm_ln, w_out_e, past)
            o["nsa_rows_p"].append(rows_p.reshape(1, sp, 4, NSA_KV, HEAD_DIM))
            o["nsa_rows_s"].append(rows_s.reshape(nseq, ts, 4, NSA_KV, HEAD_DIM))
            o["nsa_win_p"].append(win_p.reshape(1, -1, 2, NSA_KV, HEAD_DIM))
            o["nsa_win_s"].append(win_s.reshape(nseq, -1, 2, NSA_KV, HEAD_DIM))
            o["gm_v_p"].append(v_p[None])
            o["gm_v_s"].append(v_s.reshape(nseq, ts, GM_WIDTH))
        else:
            xp, xs, (rows_p, conv_p), (rows_s, conv_s) = _odd_layer(
                xp, xs, i, g0, page_table, cache_sb, state_conv, w_in_o, cv_dw, cv_b, cv_ln, w_out_o, past)
            o["sb_rows_p"].append(rows_p.reshape(1, sp, 2, SB_HEADS, HEAD_DIM))
            o["sb_rows_s"].append(rows_s.reshape(nseq, ts, 2, SB_HEADS, HEAD_DIM))
            o["conv_p"].append(conv_p[None])
            o["conv_s"].append(conv_s)

        w_xq_b = w_xq[l].astype(BF16)
        w_xo_b = w_xo[l].astype(BF16)
        g_xk = g_xqk[l, 1][None]
        g_xq = g_xqk[l, 0][None]
        mkv = _rms_matmul(mem_prompt.reshape(mlen, D_MODEL), g_norm[l, 3][None], w_xkv[l].astype(BF16),
                          mlen, 2 * D_MODEL, D_MODEL, g_xk)
        o["memkv_p"].append(mkv.reshape(1, mlen, 2, MEM_HEADS, MEM_HD))
        g1 = g_norm[l, 1][None]
        qx = _rms_matmul(xp, g1, w_xq_b, 512, D_MODEL, D_MODEL, g_xq)
        xp = _matmul_res([_cross_attn(qx, mkv[None], nseq=1, tq=512)], w_xo_b, xp, 512)
        qx = _rms_matmul(xs, g1, w_xq_b, nseq * ts, D_MODEL, D_MODEL, g_xq)
        mkv_s = cache_mem[l].reshape(nseq, mlen, 2 * D_MODEL)
        xs = _matmul_res([_cross_attn(qx, mkv_s, nseq=nseq, tq=ts)], w_xo_b, xs, nseq * ts)

        g2 = g_norm[l, 2][None]
        w_up_b = w_up[l].astype(BF16)
        w_down_b = w_down[l].astype(BF16)
        dwb = jnp.concatenate([ffn_dw[l], ffn_db[l][None], jnp.zeros((4, 2 * D_FF), F32)], 0)
        xp, sta, stg = _convffn(xp, g2, w_up_b, dwb, w_down_b, 512, D_FF // 2)
        o["ffn_p"].append(jnp.concatenate([sta[-1, 8 - (FFN_K - 1):], stg[-1, 8 - (FFN_K - 1):]], -1)[None])
        up = _rms_matmul(xs, g2, w_up_b, nseq * ts, D_FF // 2)
        full = jnp.concatenate([state_ffn[l], up.reshape(nseq, ts, 2 * D_FF)], 1)
        taps = [full[:, k:k + ts].reshape(nseq * ts, 2 * D_FF) for k in range(FFN_K)]
        xs = _convgate_down(taps[0], taps[1], taps[2], dwb, w_down_b, xs, D_FF // 2)
        o["ffn_s"].append(full[:, -(FFN_K - 1):])

    return (xp.reshape(1, sp, D_MODEL), xs.reshape(nseq, ts, D_MODEL),
            jnp.stack(o["nsa_rows_p"]), jnp.stack(o["nsa_rows_s"]),
            jnp.stack(o["nsa_win_p"]), jnp.stack(o["nsa_win_s"]),
            jnp.stack(o["gm_v_p"]), jnp.stack(o["gm_v_s"]),
            jnp.stack(o["sb_rows_p"]), jnp.stack(o["sb_rows_s"]),
            jnp.stack(o["conv_p"]), jnp.stack(o["conv_s"]),
            jnp.stack(o["ffn_p"]), jnp.stack(o["ffn_s"]),
            jnp.stack(o["memkv_p"]))
```

```python
import functools

import numpy as np
import jax
import jax.numpy as jnp
from jax import lax
from jax.experimental import pallas as pl
from jax.experimental.pallas import tpu as pltpu

F32 = jnp.float32
BF16 = jnp.bfloat16

D_MODEL = 1024
HEAD_DIM = 64
NSA_HEADS = 8
NSA_KV = 2
CMP_LEN = 32
CMP_STRIDE = 16
SEL_LEN = 64
N_SEL = 16
WINDOW = 512
FORCE_SCORE = 1.0e4
GM_WIDTH = 512
GM_GROUPS = 8
CHUNK = 128
CV_WIDTH = 512
CV_K = 31
SB_HEADS = 8
SB_WIDTH = 512
MEM_HEADS = 4
MEM_HD = 256
D_FF = 2816
FFN_K = 3
PAGE = 128
ROPE_THETA = 10000.0
EPS = 1e-6
TINY = 1e-30
IN_E_PAD = 2432
NEG = -1e30
LOG2E = 1.4426950408889634
SB_EXIT = -120.0

LANES = 128
V7X_VMEM_LIMIT = 56 * 1024 * 1024


def _params(sem):
    return pltpu.CompilerParams(dimension_semantics=sem, vmem_limit_bytes=V7X_VMEM_LIMIT)


def _const_spec(shape):
    nd = len(shape)
    return pl.BlockSpec(shape, lambda *_: (0,) * nd)


def _dot(a, b):
    return jnp.dot(a, b, preferred_element_type=F32)


def _dot_nt(a, b):
    return lax.dot_general(a, b, (((1,), (1,)), ((), ())), preferred_element_type=F32)


def _split_bf16(x):
    hi = x.astype(BF16)
    lo = (x - hi.astype(F32)).astype(BF16)
    return hi, lo


def _dot_hilo(x, w):
    hi, lo = _split_bf16(x)
    return _dot(hi, w) + _dot(lo, w)


def _rms_rows(x, g):
    return x * lax.rsqrt(jnp.mean(x * x, axis=-1, keepdims=True) + EPS) * g


def _seg_rms(x, ones_bd, g, width):
    ss = _dot((x * x).astype(BF16), ones_bd)
    return x * lax.rsqrt(ss * (1.0 / width) + EPS) * g


def _sigmoid(x):
    return 1.0 / (1.0 + jnp.exp(-x))


def _gelu_tanh(x):
    return 0.5 * x * (1.0 + jnp.tanh(0.7978845608028654 * (x + 0.044715 * (x * x * x))))


def _rope_blk(x, cos_t, sin_t, lo32):
    sw = jnp.where(lo32, pltpu.roll(x, 96, 1), pltpu.roll(x, 32, 1))
    return x * cos_t + sw * sin_t


def _inproj_even_kernel(x_ref, g_ref, w_ref, cos_ref, sin_ref, gq_ref, gk_ref, ln_ref, ones_ref,
                        wsg_ref, sbt_ref,
                        qp_ref, qr_ref, rows_ref, win_ref, gates_ref, v_ref, ogm_ref):
    x = x_ref[...]
    tm = x.shape[0]
    h = _rms_rows(x, g_ref[...]).astype(BF16)
    z = _dot(h, w_ref[...])
    lane = lax.broadcasted_iota(jnp.int32, (tm, LANES), 1)
    lo32 = (lane & 32) == 0
    lo64 = lane < 64
    cos_t = cos_ref[...]
    sin_t = sin_ref[...]
    ones128 = ones_ref[0:LANES, 0:LANES]
    gk = gk_ref[...]

    q = _seg_rms(z[:, 0:512], ones_ref[...], gq_ref[...], HEAD_DIM)
    qp_ref[...] = q
    for i in range(4):
        sl = slice(i * LANES, (i + 1) * LANES)
        qr_ref[:, sl] = _rope_blk(q[:, sl], cos_t, sin_t, lo32)

    rows_ref[:, 0:128] = _seg_rms(z[:, 512:640], ones128, gk[0:1], HEAD_DIM)
    rows_ref[:, 128:256] = z[:, 640:768]
    ks_k = _seg_rms(z[:, 768:896], ones128, gk[1:2], HEAD_DIM)
    rows_ref[:, 256:384] = _rope_blk(ks_k, cos_t, sin_t, lo32)
    rows_ref[:, 384:512] = z[:, 896:1024]
    kw_k = _seg_rms(z[:, 1024:1152], ones128, gk[2:3], HEAD_DIM)
    win_ref[:, 0:128] = _rope_blk(kw_k, cos_t, sin_t, lo32)
    win_ref[:, 128:256] = z[:, 1152:1280]

    gates_ref[...] = _sigmoid(z[:, 1280:1408])

    u = _gelu_tanh(z[:, 1408:1920])
    vv = _gelu_tanh(z[:, 1920:2432])
    ln = ln_ref[...]
    vc = vv - jnp.mean(vv, axis=-1, keepdims=True)
    v = vc * lax.rsqrt(jnp.mean(vc * vc, axis=-1, keepdims=True) + EPS) * ln[0:1] + ln[1:2]
    v_ref[...] = v
    vb = v.astype(BF16)
    for i in range(4):
        sl = slice(i * LANES, (i + 1) * LANES)
        m_lo = _dot(wsg_ref[2 * i], vb[:, sl])
        m_hi = _dot(wsg_ref[2 * i + 1], vb[:, sl])
        mixed = jnp.where(lo64, m_lo, m_hi) + sbt_ref[:, sl]
        ogm_ref[:, sl] = u[:, sl] * mixed


def _inproj_even(x, g, w, cos_t, sin_t, gq, gk, ln, ones_bd, wsg, sbt, tm):
    t = x.shape[0]
    row = lambda n: pl.BlockSpec((tm, n), lambda i: (i, 0))
    outs = [(512, F32), (512, F32), (512, F32), (256, F32), (128, F32), (512, F32), (512, F32)]
    return pl.pallas_call(
        _inproj_even_kernel,
        grid=(t // tm,),
        in_specs=[row(D_MODEL), _const_spec((1, D_MODEL)), _const_spec(w.shape), row(128), row(128),
                  _const_spec(gq.shape), _const_spec(gk.shape), _const_spec(ln.shape),
                  _const_spec(ones_bd.shape), _const_spec(wsg.shape), _const_spec(sbt.shape)],
        out_specs=[row(n) for n, _ in outs],
        out_shape=[jax.ShapeDtypeStruct((t, n), d) for n, d in outs],
        compiler_params=_params(("arbitrary",)),
        name="inproj_even",
    )(x, g, w, cos_t, sin_t, gq, gk, ln, ones_bd, wsg, sbt)


def _compress_kernel(pt_ref, *refs, npg, nsteps, nch):
    page_refs = refs[:npg]
    perm_ref, wl_ref, pe_ref, out_ref, xl_ref, a2_ref = refs[npg:]
    k = pl.program_id(1)

    @pl.when(k == 0)
    def _():
        for l in range(16):
            xl_ref[l, nch:nch + 16, :] = pe_ref[l]

    perm = perm_ref[...]
    for pp in range(npg // 2):
        ra, rb = page_refs[2 * pp], page_refs[2 * pp + 1]
        if len(ra.shape) == 2:
            pg = jnp.concatenate([ra[...], rb[...]], axis=0).astype(BF16)
            xp = _dot(perm, pg)
        else:
            pg = jnp.concatenate([ra[...].reshape(256, PAGE), rb[...].reshape(256, PAGE)], axis=1).astype(BF16)
            xp = _dot_nt(perm, pg)
        xp = xp.astype(BF16)
        row0 = pl.multiple_of((k * (npg // 2) + pp) * 16, 16)
        for l in range(16):
            xl_ref[l, pl.ds(row0, 16), :] = xp[l * 16:(l + 1) * 16]

    @pl.when(k == nsteps - 1)
    def _():
        acc = _dot(xl_ref[0], wl_ref[0])
        for l in range(1, 16):
            acc = acc + _dot(xl_ref[l], wl_ref[l])
        bias = (acc[nch:nch + 1, 0:256] + acc[nch + 2:nch + 3, 0:256]
                + acc[nch + 1:nch + 2, 256:512] + acc[nch + 3:nch + 4, 256:512])
        a2_ref[0:nch, :] = acc[0:nch, 256:512]
        a2_ref[nch:nch + 8, :] = jnp.zeros((8, 256), F32)
        out_ref[...] = acc[0:nch, 0:256] + a2_ref[1:nch + 1, :] + bias


def _page_spec(pool, layer, half, phys_fn):
    h = 0 if half is None else half
    if pool.ndim == 3:
        w = pool.shape[2] if half is None else pool.shape[2] // 2
        return pl.BlockSpec((None, PAGE, w), lambda *a: (phys_fn(*a), 0, h))
    ty = pool.shape[2] if half is None else pool.shape[2] // 2
    return pl.BlockSpec((None, None, ty) + tuple(pool.shape[3:]), lambda *a: (layer, phys_fn(*a), h, 0, 0, 0))


def _compress(pool, layer, page_table, perm, wl, pe_rows):
    nseq, npages = page_table.shape
    npg = 16 if npages % 16 == 0 else 8
    nsteps = npages // npg
    nch = npages * (PAGE // CMP_STRIDE)

    def page_spec(p):
        return _page_spec(pool, layer, 0, lambda s, k, pt: pt[s, k * npg + p])

    grid_spec = pltpu.PrefetchScalarGridSpec(
        num_scalar_prefetch=1,
        grid=(nseq, nsteps),
        in_specs=[page_spec(p) for p in range(npg)] + [
            pl.BlockSpec(perm.shape, lambda s, k, pt: (0, 0)),
            pl.BlockSpec(wl.shape, lambda s, k, pt: (0, 0, 0)),
            pl.BlockSpec(pe_rows.shape, lambda s, k, pt: (0, 0, 0))],
        out_specs=pl.BlockSpec((None, nch, 256), lambda s, k, pt: (s, 0, 0)),
        scratch_shapes=[pltpu.VMEM((16, nch + 16, 256), BF16), pltpu.VMEM((nch + 8, 256), F32)],
    )
    return pl.pallas_call(
        functools.partial(_compress_kernel, npg=npg, nsteps=nsteps, nch=nch),
        grid_spec=grid_spec,
        out_shape=jax.ShapeDtypeStruct((nseq, nch, 256), F32),
        compiler_params=_params(("arbitrary", "arbitrary")),
        name="nsa_compress",
    )(page_table, *([pool] * npg), perm, wl, pe_rows)


def _stack_q(q, lo64):
    blks = [q[:, i * LANES:(i + 1) * LANES] for i in range(4)]
    slabs = [jnp.where(lo64, b, 0.0) for b in blks] + [jnp.where(lo64, 0.0, b) for b in blks]
    return jnp.concatenate(slabs, axis=0).astype(BF16)


def _unstack_o(o, tq, lo64):
    return [jnp.where(lo64, o[i * tq:(i + 1) * tq], o[(4 + i) * tq:(5 + i) * tq]) for i in range(4)]


def _gate_expand(gates, gexp):
    return _dot_hilo(gates, gexp)


def _masked_softmax_rows(s, mask):
    s = jnp.where(mask, s, NEG)
    m = jnp.max(s, axis=-1, keepdims=True)
    p = jnp.where(mask, jnp.exp(s - m), 0.0)
    return p / jnp.maximum(jnp.sum(p, axis=-1, keepdims=True), TINY)


def _masked_softmax_stacked(s_all, mask, tq):
    n = s_all.shape[1]
    return _masked_softmax_rows(s_all.reshape(8, tq, n), mask[None]).reshape(8 * tq, n)


def _cmp_attn_kernel(qp_ref, cmp_ref, gates_ref, msel_ref, gexp_ref, oc_ref, sel_ref, *,
                     tq, pos0, nc, ns, nsp, nch, k_eff):
    qb = pl.program_id(1)
    lane = lax.broadcasted_iota(jnp.int32, (tq, LANES), 1)
    lo64 = lane < 64
    qs = _stack_q(qp_ref[...], lo64)
    cm = cmp_ref[...]
    kc = cm[:, 0:128].astype(BF16)
    vc = cm[:, 128:256].astype(BF16)
    s_all = _dot_nt(qs, kc) * (HEAD_DIM ** -0.5)
    qpos = pos0 + qb * tq + lax.broadcasted_iota(jnp.int32, (tq, 1), 0)
    nidx = lax.broadcasted_iota(jnp.int32, (tq, nch), 1)
    cmask = (nidx * CMP_STRIDE + (CMP_LEN - 1) <= qpos) & (nidx < nc)

    blocks_on_rows = tq % LANES == 0
    msel = msel_ref[...]
    ax = 0 if blocks_on_rows else 1
    if blocks_on_rows:
        qpos_b = pos0 + qb * tq + lax.broadcasted_iota(jnp.int32, (1, tq), 1)
    else:
        qpos_b = qpos
    jidx = lax.broadcasted_iota(jnp.int32, (nsp, tq) if blocks_on_rows else (tq, nsp), ax)
    jf = jidx.astype(F32)
    causal_ok = jidx * SEL_LEN <= qpos_b
    forced = (jidx == jnp.right_shift(qpos_b, 6)) | (jidx == 0)
    p_all = _masked_softmax_stacked(s_all, cmask, tq)
    o = _dot(p_all.astype(BF16), vc)
    for kv in range(NSA_KV):
        pg = jnp.sum(p_all[kv * 4 * tq:(kv + 1) * 4 * tq].reshape(4, tq, nch), axis=0)
        if blocks_on_rows:
            hi, lo = _split_bf16(pg)
            p_slc = _dot_nt(msel, hi) + _dot_nt(msel, lo)
        else:
            p_slc = _dot_hilo(pg, msel)
        score = jnp.where(causal_ok, jnp.where(forced, FORCE_SCORE, p_slc), -1.0)
        score = jnp.where(jidx < ns, score, -3.0e38)
        work = score
        for _ in range(k_eff):
            m = jnp.max(work, axis=ax, keepdims=True)
            idx = jnp.min(jnp.where(work == m, jf, 1.0e9), axis=ax, keepdims=True)
            work = jnp.where(jf == idx, -jnp.inf, work)
        picked = jnp.where((work == -jnp.inf) & (score >= 0.0), 1.0, 0.0)
        sel_ref[:, kv * nsp:(kv + 1) * nsp] = picked.T if blocks_on_rows else picked
    gx = _gate_expand(gates_ref[...], gexp_ref[...])
    for i, ob in enumerate(_unstack_o(o, tq, lo64)):
        sl = slice(i * LANES, (i + 1) * LANES)
        oc_ref[:, sl] = gx[:, sl] * ob


def _cmp_attn(qp, cmp, gates, msel, gexp, *, nseq, tq, pos0, nc, ns, k_eff):
    t = qp.shape[0]
    nqb = t // (nseq * tq)
    nch = cmp.shape[1]
    nsp = msel.shape[1]
    if tq % LANES == 0:
        msel = msel.T
    row = lambda n: pl.BlockSpec((tq, n), lambda s, b: (s * nqb + b, 0))
    return pl.pallas_call(
        functools.partial(_cmp_attn_kernel, tq=tq, pos0=pos0, nc=nc, ns=ns, nsp=nsp, nch=nch, k_eff=k_eff),
        grid=(nseq, nqb),
        in_specs=[row(512), pl.BlockSpec((None, nch, 256), lambda s, b: (s, 0, 0)), row(128),
                  _const_spec(msel.shape), _const_spec(gexp.shape)],
        out_specs=[row(512), row(2 * nsp)],
        out_shape=[jax.ShapeDtypeStruct((t, 512), F32), jax.ShapeDtypeStruct((t, 2 * nsp), F32)],
        compiler_params=_params(("arbitrary", "arbitrary")),
        name="nsa_cmp_select",
    )(qp, cmp, gates, msel, gexp)


def _win_attn_kernel(*refs, tq, pos0, kblocks):
    nkb = len(kblocks)
    qr_ref, gates_ref, gexp_ref, oin_ref = refs[:4]
    k_refs = refs[4:4 + nkb]
    out_ref = refs[4 + nkb]
    qb = pl.program_id(1)
    lane = lax.broadcasted_iota(jnp.int32, (tq, LANES), 1)
    lo64 = lane < 64
    qs = _stack_q(qr_ref[...], lo64)
    kv = jnp.concatenate([r[...] for r in k_refs], axis=0)
    kk = kv[:, 0:128].astype(BF16)
    vv = kv[:, 128:256].astype(BF16)
    qpos = pos0 + qb * tq + lax.broadcasted_iota(jnp.int32, (tq, 1), 0)
    kpos = jnp.concatenate(
        [c0 + c1 * qb + lax.broadcasted_iota(jnp.int32, (tq, n), 1) for (c0, c1, n) in kblocks], axis=1)
    mask = (kpos <= qpos) & (kpos > qpos - WINDOW) & (kpos >= 0)
    s_all = _dot_nt(qs, kk) * (HEAD_DIM ** -0.5)
    o = _dot(_masked_softmax_stacked(s_all, mask, tq).astype(BF16), vv)
    gx = _gate_expand(gates_ref[...], gexp_ref[...])
    for i, ob in enumerate(_unstack_o(o, tq, lo64)):
        sl = slice(i * LANES, (i + 1) * LANES)
        out_ref[:, sl] = oin_ref[:, sl] + gx[:, sl] * ob


def _win_attn(qr, gates, gexp, o_in, key_arrays, key_specs, kblocks, *, nseq, tq, pos0):
    t = qr.shape[0]
    nqb = t // (nseq * tq)
    row = lambda n: pl.BlockSpec((tq, n), lambda s, b: (s * nqb + b, 0))
    return pl.pallas_call(
        functools.partial(_win_attn_kernel, tq=tq, pos0=pos0, kblocks=tuple(kblocks)),
        grid=(nseq, nqb),
        in_specs=[row(512), row(128), _const_spec(gexp.shape), row(512)] + list(key_specs),
        out_specs=row(512),
        out_shape=jax.ShapeDtypeStruct((t, 512), F32),
        compiler_params=_params(("arbitrary", "arbitrary")),
        name="nsa_window",
    )(qr, gates, gexp, o_in, *key_arrays)


def _sel_schedule(nqb, npg, c0, c1):
    page_steps = lambda b: (c0 + c1 * b + npg - 1) // npg
    if c1 == 0 or nqb % 2:
        return nqb, page_steps(0 if c1 == 0 else nqb - 1) + 1, lambda p, k: (p, k, page_steps(p))
    total = max(page_steps(b) + page_steps(nqb - 1 - b) + 2 for b in range(nqb // 2))

    def sched(p, k):
        n1 = page_steps(p) + 1
        first = k < n1
        b = jnp.where(first, p, nqb - 1 - p)
        return b, jnp.where(first, k, k - n1), page_steps(b)

    return nqb // 2, total, sched


def _sel_attn_kernel(pt_ref, *refs, tq, pos0, npg, nsp, c0, c1, sched):
    qr_ref, sel_ref, gates_ref, gexp_ref, oin_ref, tail_ref, e_ref, etail_ref = refs[:8]
    page_refs = refs[8:8 + npg]
    out_ref, qs_ref, m_ref, acc_ref = refs[8 + npg:]
    qb, k, na = sched(pl.program_id(1), pl.program_id(2))
    npast = c0 + c1 * qb
    lane = lax.broadcasted_iota(jnp.int32, (tq, LANES), 1)
    lo64 = lane < 64

    @pl.when(k == 0)
    def _():
        qs_ref[...] = _stack_q(qr_ref[...] * (LOG2E * HEAD_DIM ** -0.5), lo64)
        m_ref[...] = jnp.full(m_ref.shape, NEG, F32)
        acc_ref[...] = jnp.zeros(acc_ref.shape, F32)

    def update(s_all, pv, biases):
        nk = s_all.shape[1]
        for kvi in range(NSA_KV):
            rows = slice(kvi * 4 * tq, (kvi + 1) * 4 * tq)
            s = (s_all[rows].reshape(4, tq, nk) + biases[kvi][None]).reshape(4 * tq, nk)
            m_old = m_ref[rows]
            m_new = jnp.maximum(m_old, jnp.max(s, axis=-1, keepdims=True))
            p = jnp.exp2(s - m_new)
            acc_ref[rows] = jnp.exp2(m_old - m_new) * acc_ref[rows] + pv(p.astype(BF16), kvi)
            m_ref[rows] = m_new

    def with_ones(v, axis):
        first = lax.broadcasted_iota(jnp.int32, v.shape, axis) < HEAD_DIM
        return jnp.where(first, v, 1.0).astype(BF16), jnp.where(first, 1.0, v).astype(BF16)

    def block_biases(e, extra):
        out = []
        for kvi in range(NSA_KV):
            sel = sel_ref[:, kvi * nsp:(kvi + 1) * nsp].astype(BF16)
            b = (_dot(sel, e) - 1.0) * (-NEG)
            out.append(b if extra is None else jnp.where(extra, b, NEG))
        return out

    def rows_update(kv, biases):
        vs = with_ones(kv[:, 128:256], 1)
        update(_dot_nt(qs_ref[...], kv[:, 0:128].astype(BF16)), lambda p, kvi: _dot(p, vs[kvi]), biases)

    @pl.when(k < na)
    def _():
        nk = npg * PAGE
        if c1 == 0 and c0 % npg == 0:
            extra = None
        else:
            extra = k * npg + jnp.right_shift(lax.broadcasted_iota(jnp.int32, (tq, nk), 1), 7) < npast
        biases = block_biases(e_ref[...], extra)
        if len(page_refs[0].shape) == 2:
            rows_update(jnp.concatenate([r[...] for r in page_refs], axis=0), biases)
        else:
            kt = jnp.concatenate([r[0].reshape(LANES, PAGE) for r in page_refs], axis=1).astype(BF16)
            vts = with_ones(jnp.concatenate([r[1].reshape(LANES, PAGE) for r in page_refs], axis=1), 0)
            update(_dot(qs_ref[...], kt), lambda p, kvi: _dot_nt(p, vts[kvi]), biases)

    @pl.when(k == na)
    def _():
        qpos = pos0 + qb * tq + lax.broadcasted_iota(jnp.int32, (tq, 1), 0)
        kpos = npast * PAGE + lax.broadcasted_iota(jnp.int32, (tq, PAGE), 1)
        rows_update(tail_ref[...], block_biases(etail_ref[...], kpos <= qpos))
        acc = acc_ref[...]
        o = jnp.where(m_ref[...] > 0.5 * NEG, acc / jnp.maximum(pltpu.roll(acc, HEAD_DIM, 1), TINY), 0.0)
        gx = _gate_expand(gates_ref[...], gexp_ref[...])
        for i, ob in enumerate(_unstack_o(o, tq, lo64)):
            sl = slice(i * LANES, (i + 1) * LANES)
            out_ref[:, sl] = oin_ref[:, sl] + gx[:, sl] * ob


def _sel_attn(qr, sel, gates, gexp, o_in, tail3, pool, layer, page_table, *, nseq, tq, pos0, c0, c1, npg):
    t = qr.shape[0]
    nqb = t // (nseq * tq)
    npages = page_table.shape[1]
    assert npages % npg == 0
    nsp = sel.shape[1] // 2
    nrows, nsteps, sched = _sel_schedule(nqb, npg, c0, c1)
    blk = lambda p, k: sched(p, k)[0]
    row = lambda n: pl.BlockSpec((tq, n), lambda s, p, k, pt: (s * nqb + blk(p, k), 0))
    ncol = (npages + 1) * PAGE
    e_all = (jnp.arange(ncol)[None, :] // SEL_LEN == jnp.arange(nsp)[:, None]).astype(BF16)

    def page_spec(slot):
        def phys(s, p, k, pt):
            b, kl, _ = sched(p, k)
            last = jnp.maximum(c0 + c1 * b - 1, 0)
            return pt[s, jnp.minimum(jnp.minimum(kl * npg + slot, last), npages - 1)]
        return _page_spec(pool, layer, 1, phys)

    def e_index(s, p, k, pt):
        _, kl, na = sched(p, k)
        return (0, jnp.clip(kl, 0, jnp.maximum(na - 1, 0)))

    grid_spec = pltpu.PrefetchScalarGridSpec(
        num_scalar_prefetch=1,
        grid=(nseq, nrows, nsteps),
        in_specs=[row(512), row(2 * nsp), row(128),
                  pl.BlockSpec(gexp.shape, lambda s, p, k, pt: (0, 0)), row(512),
                  pl.BlockSpec((None, PAGE, 256), lambda s, p, k, pt: (s * nqb + blk(p, k), 0, 1)),
                  pl.BlockSpec((nsp, npg * PAGE), e_index),
                  pl.BlockSpec((nsp, PAGE), lambda s, p, k, pt: (0, c0 + c1 * blk(p, k)))]
                 + [page_spec(slot) for slot in range(npg)],
        out_specs=row(512),
        scratch_shapes=[pltpu.VMEM((8 * tq, LANES), BF16), pltpu.VMEM((8 * tq, 1), F32),
                        pltpu.VMEM((8 * tq, LANES), F32)],
    )
    return pl.pallas_call(
        functools.partial(_sel_attn_kernel, tq=tq, pos0=pos0, npg=npg, nsp=nsp, c0=c0, c1=c1, sched=sched),
        grid_spec=grid_spec,
        out_shape=jax.ShapeDtypeStruct((t, 512), F32),
        compiler_params=_params(("arbitrary", "arbitrary", "arbitrary")),
        name="nsa_selected",
    )(page_table, qr, sel, gates, gexp, o_in, tail3, e_all, e_all, *([pool] * npg))


def _rms_matmul_kernel(x_ref, g_ref, w_ref, *rest, n_norm):
    if n_norm:
        gseg_ref, out_ref = rest
    else:
        (out_ref,) = rest
    h = _rms_rows(x_ref[...], g_ref[...]).astype(BF16)
    z = _dot(h, w_ref[...])
    if n_norm:
        for c0 in range(0, n_norm, MEM_HD):
            out_ref[:, c0:c0 + MEM_HD] = _rms_rows(z[:, c0:c0 + MEM_HD], gseg_ref[...])
        if n_norm < z.shape[1]:
            out_ref[:, n_norm:] = z[:, n_norm:]
    else:
        out_ref[...] = z


def _rms_matmul(x, g, w, tm, tn, n_norm=0, gseg=None):
    t, n = x.shape[0], w.shape[1]
    extra, extra_specs = [], []
    if n_norm:
        assert tn == n
        extra = [gseg]
        extra_specs = [_const_spec(gseg.shape)]
    return pl.pallas_call(
        functools.partial(_rms_matmul_kernel, n_norm=n_norm),
        grid=(t // tm, n // tn),
        in_specs=[pl.BlockSpec((tm, D_MODEL), lambda i, j: (i, 0)), _const_spec((1, D_MODEL)),
                  pl.BlockSpec((D_MODEL, tn), lambda i, j: (0, j))] + extra_specs,
        out_specs=pl.BlockSpec((tm, tn), lambda i, j: (i, j)),
        out_shape=jax.ShapeDtypeStruct((t, n), F32),
        compiler_params=_params(("arbitrary", "arbitrary")),
        name="rms_matmul",
    )(x, g, w, *extra)


def _matmul_res_kernel(*refs, na):
    a_refs = refs[:na]
    w_ref, x_ref, out_ref = refs[na:]
    acc = x_ref[...]
    k0 = 0
    for a_ref in a_refs:
        kw = a_ref.shape[1]
        acc = acc + _dot(a_ref[...].astype(BF16), w_ref[k0:k0 + kw, :])
        k0 += kw
    out_ref[...] = acc


def _matmul_res(acts, w, x, tm):
    t = x.shape[0]
    return pl.pallas_call(
        functools.partial(_matmul_res_kernel, na=len(acts)),
        grid=(t // tm,),
        in_specs=[pl.BlockSpec((tm, a.shape[1]), lambda i: (i, 0)) for a in acts]
                 + [_const_spec(w.shape), pl.BlockSpec((tm, D_MODEL), lambda i: (i, 0))],
        out_specs=pl.BlockSpec((tm, D_MODEL), lambda i: (i, 0)),
        out_shape=jax.ShapeDtypeStruct((t, D_MODEL), F32),
        compiler_params=_params(("arbitrary",)),
        name="matmul_residual",
    )(*acts, w, x)


def _cross_attn_kernel(q_ref, mkv_ref, out_ref):
    for h in range(MEM_HEADS):
        sl = slice(h * MEM_HD, (h + 1) * MEM_HD)
        q = q_ref[:, sl].astype(BF16)
        kk = mkv_ref[:, sl].astype(BF16)
        vv = mkv_ref[:, D_MODEL + h * MEM_HD:D_MODEL + (h + 1) * MEM_HD].astype(BF16)
        s = _dot_nt(q, kk) * (MEM_HD ** -0.5)
        m = jnp.max(s, axis=-1, keepdims=True)
        p = jnp.exp(s - m)
        p = p / jnp.sum(p, axis=-1, keepdims=True)
        out_ref[:, sl] = _dot(p.astype(BF16), vv)


def _cross_attn(q, mkv, *, nseq, tq):
    t = q.shape[0]
    nqb = t // (nseq * tq)
    mlen = mkv.shape[1]
    return pl.pallas_call(
        _cross_attn_kernel,
        grid=(nseq, nqb),
        in_specs=[pl.BlockSpec((tq, D_MODEL), lambda s, b: (s * nqb + b, 0)),
                  pl.BlockSpec((None, mlen, 2 * D_MODEL), lambda s, b: (s, 0, 0))],
        out_specs=pl.BlockSpec((tq, D_MODEL), lambda s, b: (s * nqb + b, 0)),
        out_shape=jax.ShapeDtypeStruct((t, D_MODEL), F32),
        compiler_params=_params(("arbitrary", "arbitrary")),
        name="cross_attn",
    )(q, mkv)


def _convffn_kernel(x_ref, g_ref, wa_ref, wg_ref, dwa_ref, dwg_ref, wd_ref,
                    out_ref, sta_ref, stg_ref,
                    hn_ref, acc_ref, sa_ref, sg_ref, ca_ref, cg_ref, *, tm, nff):
    i = pl.program_id(0)
    j = pl.program_id(1)

    @pl.when(j == 0)
    def _():
        hn_ref[...] = _rms_rows(x_ref[...], g_ref[...]).astype(BF16)
        acc_ref[...] = jnp.zeros(acc_ref.shape, F32)

    @pl.when(i == 0)
    def _():
        ca_ref[j] = jnp.zeros(ca_ref.shape[1:], F32)
        cg_ref[j] = jnp.zeros(cg_ref.shape[1:], F32)

    hn = hn_ref[...]

    def conv(w_ref, s_ref, c_ref, dw_ref, st_ref):
        u = _dot(hn, w_ref[...])
        s_ref[0:8, :] = c_ref[j]
        s_ref[8:8 + tm, :] = u
        c_ref[j] = u[tm - 8:tm]
        st_ref[...] = u[tm - 8:tm]
        dw = dw_ref[...]
        return dw[0:1] * s_ref[6:6 + tm, :] + dw[1:2] * s_ref[7:7 + tm, :] + dw[2:3] * u + dw[3:4]

    a = conv(wa_ref, sa_ref, ca_ref, dwa_ref, sta_ref)
    g = conv(wg_ref, sg_ref, cg_ref, dwg_ref, stg_ref)
    y = (g * _sigmoid(g)) * a
    acc_ref[...] += _dot(y.astype(BF16), wd_ref[...])

    @pl.when(j == nff - 1)
    def _():
        out_ref[...] = x_ref[...] + acc_ref[...]


def _convffn(x, g, w_up, dwb, w_down, tm, tf):
    t = x.shape[0]
    nff = D_FF // tf
    nt = t // tm
    return pl.pallas_call(
        functools.partial(_convffn_kernel, tm=tm, nff=nff),
        grid=(nt, nff),
        in_specs=[pl.BlockSpec((tm, D_MODEL), lambda i, j: (i, 0)), _const_spec((1, D_MODEL)),
                  pl.BlockSpec((D_MODEL, tf), lambda i, j: (0, j)),
                  pl.BlockSpec((D_MODEL, tf), lambda i, j: (0, nff + j)),
                  pl.BlockSpec((8, tf), lambda i, j: (0, j)),
                  pl.BlockSpec((8, tf), lambda i, j: (0, nff + j)),
                  pl.BlockSpec((tf, D_MODEL), lambda i, j: (j, 0))],
        out_specs=[pl.BlockSpec((tm, D_MODEL), lambda i, j: (i, 0)),
                   pl.BlockSpec((None, 8, tf), lambda i, j: (i, 0, j)),
                   pl.BlockSpec((None, 8, tf), lambda i, j: (i, 0, j))],
        out_shape=[jax.ShapeDtypeStruct((t, D_MODEL), F32),
                   jax.ShapeDtypeStruct((nt, 8, D_FF), F32),
                   jax.ShapeDtypeStruct((nt, 8, D_FF), F32)],
        scratch_shapes=[pltpu.VMEM((tm, D_MODEL), BF16), pltpu.VMEM((tm, D_MODEL), F32),
                        pltpu.VMEM((tm + 8, tf), F32), pltpu.VMEM((tm + 8, tf), F32),
                        pltpu.VMEM((nff, 8, tf), F32), pltpu.VMEM((nff, 8, tf), F32)],
        compiler_params=_params(("arbitrary", "arbitrary")),
        name="convffn",
    )(x, g, w_up, w_up, dwb, dwb, w_down)


def _convgate_down_kernel(a0_ref, a1_ref, a2_ref, g0_ref, g1_ref, g2_ref, dwa_ref, dwg_ref, wd_ref, x_ref,
                          out_ref, acc_ref, *, nff):
    j = pl.program_id(0)

    @pl.when(j == 0)
    def _():
        acc_ref[...] = jnp.zeros(acc_ref.shape, F32)

    dwa = dwa_ref[...]
    dwg = dwg_ref[...]
    a = dwa[0:1] * a0_ref[...] + dwa[1:2] * a1_ref[...] + dwa[2:3] * a2_ref[...] + dwa[3:4]
    g = dwg[0:1] * g0_ref[...] + dwg[1:2] * g1_ref[...] + dwg[2:3] * g2_ref[...] + dwg[3:4]
    y = (g * _sigmoid(g)) * a
    acc_ref[...] += _dot(y.astype(BF16), wd_ref[...])

    @pl.when(j == nff - 1)
    def _():
        out_ref[...] = x_ref[...] + acc_ref[...]


def _convgate_down(f0, f1, f2, dwb, w_down, x, tf):
    t = x.shape[0]
    nff = D_FF // tf
    fa = pl.BlockSpec((t, tf), lambda j: (0, j))
    fg = pl.BlockSpec((t, tf), lambda j: (0, nff + j))
    return pl.pallas_call(
        functools.partial(_convgate_down_kernel, nff=nff),
        grid=(nff,),
        in_specs=[fa, fa, fa, fg, fg, fg,
                  pl.BlockSpec((8, tf), lambda j: (0, j)), pl.BlockSpec((8, tf), lambda j: (0, nff + j)),
                  pl.BlockSpec((tf, D_MODEL), lambda j: (j, 0)), _const_spec((t, D_MODEL))],
        out_specs=_const_spec((t, D_MODEL)),
        out_shape=jax.ShapeDtypeStruct((t, D_MODEL), F32),
        scratch_shapes=[pltpu.VMEM((t, D_MODEL), F32)],
        compiler_params=_params(("arbitrary",)),
        name="convgate_down",
    )(f0, f1, f2, f0, f1, f2, dwb, dwb, w_down, x)


def _inproj_odd_kernel(x_ref, g_ref, w_ref, glu_ref, q_ref, rows_ref):
    h = _rms_rows(x_ref[...], g_ref[...]).astype(BF16)
    z = _dot(h, w_ref[...])
    glu_ref[...] = z[:, 0:512] * _sigmoid(z[:, 512:1024])
    q_ref[...] = z[:, 1024:1536]
    rows_ref[...] = z[:, 1536:2560]


def _inproj_odd(x, g, w, tm):
    t = x.shape[0]
    row = lambda n: pl.BlockSpec((tm, n), lambda i: (i, 0))
    return pl.pallas_call(
        _inproj_odd_kernel,
        grid=(t // tm,),
        in_specs=[row(D_MODEL), _const_spec((1, D_MODEL)), _const_spec(w.shape)],
        out_specs=[row(512), row(512), row(1024)],
        out_shape=[jax.ShapeDtypeStruct((t, 512), F32), jax.ShapeDtypeStruct((t, 512), F32),
                   jax.ShapeDtypeStruct((t, 1024), F32)],
        compiler_params=_params(("arbitrary",)),
        name="inproj_odd",
    )(x, g, w)


def _conv_module_kernel(prev_ref, cur_ref, dw_ref, aux_ref, out_ref, s_ref, *, tm, zero_first, rb):
    i = pl.program_id(1)
    prev = prev_ref[...]
    if zero_first:
        prev = jnp.where(i == 0, 0.0, prev)
    s_ref[0:32, :] = prev
    s_ref[32:32 + tm, :] = cur_ref[...]
    aux = aux_ref[...]
    for r0 in range(0, tm, rb):
        acc = jnp.zeros((rb, CV_WIDTH), F32) + aux[0:1]
        for d in range(CV_K):
            acc = acc + dw_ref[CV_K - 1 - d:CV_K - d, :] * s_ref[32 - d + r0:32 - d + r0 + rb, :]
        c = acc - jnp.mean(acc, axis=-1, keepdims=True)
        y = c * lax.rsqrt(jnp.mean(c * c, axis=-1, keepdims=True) + EPS) * aux[1:2] + aux[2:3]
        out_ref[r0:r0 + rb, :] = y * _sigmoid(y)


def _conv_module(prev3, cur3, dw, aux, *, tm, zero_first):
    nseq, tseg, _ = cur3.shape
    nb = tseg // tm
    if zero_first:
        prev_spec = pl.BlockSpec((None, 32, CV_WIDTH),
                                 lambda s, i: (s, jnp.maximum(i * (tm // 32) - 1, 0), 0))
    else:
        prev_spec = pl.BlockSpec((None, 32, CV_WIDTH), lambda s, i: (s, 0, 0))
    return pl.pallas_call(
        functools.partial(_conv_module_kernel, tm=tm, zero_first=zero_first, rb=min(tm, 32)),
        grid=(nseq, nb),
        in_specs=[prev_spec, pl.BlockSpec((None, tm, CV_WIDTH), lambda s, i: (s, i, 0)),
                  _const_spec(dw.shape), _const_spec(aux.shape)],
        out_specs=pl.BlockSpec((None, tm, CV_WIDTH), lambda s, i: (s, i, 0)),
        out_shape=jax.ShapeDtypeStruct((nseq, tseg, CV_WIDTH), F32),
        scratch_shapes=[pltpu.VMEM((tm + 32, CV_WIDTH), F32)],
        compiler_params=_params(("arbitrary", "arbitrary")),
        name="conv_module",
    )(prev3, cur3, dw, aux)


def _sb_kernel(pt_ref, q_ref, tail_ref, p1_ref, p2_ref, ucat_ref, pool_ref, out_ref,
               carry_ref, acc_ref, buf_ref, sem, *, tq, pos0, c0, c1, layer):
    s = pl.program_id(0)
    qb = pl.program_id(1)
    npast = c0 + c1 * qb
    lane = lax.broadcasted_iota(jnp.int32, (tq, LANES), 1)
    lo64 = lane < 64
    q = q_ref[...]
    qs = []
    for i in range(4):
        blk = q[:, i * LANES:(i + 1) * LANES]
        qs.append(jnp.concatenate([jnp.where(lo64, blk, 0.0), jnp.where(lo64, 0.0, blk)], axis=0).astype(BF16))
    carry_ref[...] = jnp.zeros(carry_ref.shape, F32)
    acc_ref[...] = jnp.zeros(acc_ref.shape, F32)
    ucat = ucat_ref[...]

    def process(kv_ref, mask):
        zs, pvs = [], []
        for i in range(4):
            if len(kv_ref.shape) == 2:
                kb = kv_ref[:, i * LANES:(i + 1) * LANES].astype(BF16)
                vb = kv_ref[:, SB_WIDTH + i * LANES:SB_WIDTH + (i + 1) * LANES].astype(BF16)
                zs.append(_dot_nt(qs[i], kb))
                pvs.append(functools.partial(_dot, b=vb))
            else:
                kt = jnp.concatenate([kv_ref[0, 2 * i], kv_ref[0, 2 * i + 1]], axis=0).astype(BF16)
                vt = jnp.concatenate([kv_ref[1, 2 * i], kv_ref[1, 2 * i + 1]], axis=0).astype(BF16)
                zs.append(_dot(qs[i], kt))
                pvs.append(functools.partial(_dot_nt, b=vt))
        z = jnp.concatenate(zs, axis=0) * (HEAD_DIM ** -0.5)
        log_b = -(jnp.maximum(-z, 0.0) + jnp.log1p(jnp.exp(-jnp.abs(z))))
        l1m = log_b - z
        if mask is not None:
            l1m = jnp.where(mask[None], l1m.reshape(4, 2 * tq, PAGE), 0.0).reshape(8 * tq, PAGE)
        ac = _dot_hilo(l1m, ucat)
        a = jnp.exp(log_b + ac[:, 0:PAGE] + carry_ref[...])
        if mask is not None:
            a = jnp.where(mask[None], a.reshape(4, 2 * tq, PAGE), 0.0).reshape(8 * tq, PAGE)
        ab = a.astype(BF16)
        acc_ref[...] += jnp.concatenate(
            [pvs[i](ab[i * 2 * tq:(i + 1) * 2 * tq]) for i in range(4)], axis=0)
        carry_ref[...] += ac[:, PAGE:2 * PAGE]

    def live():
        return jnp.max(carry_ref[...]) >= SB_EXIT

    qpos = pos0 + qb * tq + lax.broadcasted_iota(jnp.int32, (tq, 1), 0)
    qpos2 = jnp.concatenate([qpos, qpos], axis=0)
    kpos = npast * PAGE + lax.broadcasted_iota(jnp.int32, (2 * tq, PAGE), 1)
    process(tail_ref, kpos < qpos2)

    @pl.when(npast >= 1)
    def _():
        process(p1_ref, None)

    @pl.when((npast >= 2) & live())
    def _():
        process(p2_ref, None)

    def body(state):
        p, _ = state
        src = pool_ref.at[pt_ref[s, p]] if layer is None else pool_ref.at[layer, pt_ref[s, p]]
        cp = pltpu.make_async_copy(src, buf_ref, sem)
        cp.start()
        cp.wait()
        process(buf_ref, None)
        return p - 1, live().astype(jnp.int32)

    lax.while_loop(lambda st: (st[0] >= 0) & (st[1] > 0), body, (npast - 3, live().astype(jnp.int32)))

    for i in range(4):
        r0 = 2 * i * tq
        out_ref[:, i * LANES:(i + 1) * LANES] = jnp.where(lo64, acc_ref[r0:r0 + tq], acc_ref[r0 + tq:r0 + 2 * tq])


def _sb_attn(q, tail3, pool, layer, page_table, ucat, *, nseq, tq, pos0, c0, c1):
    t = q.shape[0]
    nqb = t // (nseq * tq)
    npages = page_table.shape[1]
    page_shape = tuple(pool.shape[1:]) if pool.ndim == 3 else tuple(pool.shape[2:])

    def page_spec(back):
        return _page_spec(pool, layer, None,
                          lambda s, b, pt: pt[s, jnp.clip(c0 + c1 * b - back, 0, npages - 1)])

    grid_spec = pltpu.PrefetchScalarGridSpec(
        num_scalar_prefetch=1,
        grid=(nseq, nqb),
        in_specs=[pl.BlockSpec((tq, SB_WIDTH), lambda s, b, pt: (s * nqb + b, 0)),
                  pl.BlockSpec((None, PAGE, 2 * SB_WIDTH), lambda s, b, pt: (s * nqb + b, 0, 0)),
                  page_spec(1), page_spec(2),
                  pl.BlockSpec(ucat.shape, lambda s, b, pt: (0, 0)),
                  pl.BlockSpec(memory_space=pl.ANY)],
        out_specs=pl.BlockSpec((tq, SB_WIDTH), lambda s, b, pt: (s * nqb + b, 0)),
        scratch_shapes=[pltpu.VMEM((8 * tq, PAGE), F32), pltpu.VMEM((8 * tq, LANES), F32),
                        pltpu.VMEM(page_shape, F32), pltpu.SemaphoreType.DMA(())],
    )
    return pl.pallas_call(
        functools.partial(_sb_kernel, tq=tq, pos0=pos0, c0=c0, c1=c1, layer=None if pool.ndim == 3 else layer),
        grid_spec=grid_spec,
        out_shape=jax.ShapeDtypeStruct((t, SB_WIDTH), F32),
        compiler_params=_params(("arbitrary", "arbitrary")),
        name="stick_breaking",
    )(page_table, q, tail3, pool, pool, ucat, pool)


def _q_perm():
    idx = np.zeros((512,), np.int32)
    for i in range(4):
        for half in range(2):
            for d in range(HEAD_DIM):
                idx[i * 128 + half * 64 + d] = (half * 4 + i) * HEAD_DIM + d
    return idx


def _gate_perm():
    idx = np.zeros((24,), np.int32)
    for c in range(3):
        for i in range(4):
            for half in range(2):
                idx[c * 8 + 2 * i + half] = (half * 4 + i) * 3 + c
    return idx


def _gate_expand_mats():
    g = np.zeros((3, 128, 512), np.float32)
    for c in range(3):
        for i in range(4):
            for half in range(2):
                g[c, c * 8 + 2 * i + half, i * 128 + half * 64:i * 128 + (half + 1) * 64] = 1.0
    return jnp.asarray(g, BF16)


def _block_ones(n, w):
    r = np.arange(n) // w
    return jnp.asarray((r[:, None] == r[None, :]).astype(np.float32), BF16)


def _rope_tables(pos):
    half = HEAD_DIM // 2
    inv = jnp.power(ROPE_THETA, -jnp.arange(half, dtype=F32) / half)
    ang = pos.astype(F32)[:, None] * inv[None, :]
    c, s = jnp.cos(ang), jnp.sin(ang)
    return jnp.concatenate([c, c, c, c], -1), jnp.concatenate([-s, s, -s, s], -1)


def _sel_sum_matrix(nch, nsp, nc):
    n = np.arange(nch)[:, None]
    j = np.arange(nsp)[None, :]
    step = SEL_LEN // CMP_STRIDE
    lo = 1 - CMP_LEN // CMP_STRIDE
    m = (n >= step * j + lo) & (n <= step * j + step - 1) & (n < nc)
    return jnp.asarray(m.astype(np.float32), BF16)


def _chunk_perm():
    p = np.zeros((256, 256), np.float32)
    for l in range(16):
        for c in range(16):
            p[l * 16 + c, c * 16 + l] = 1.0
    return jnp.asarray(p, BF16)


def _compress_weights(cmp_pe, w_cmp):
    wl = jnp.zeros((16, 4, HEAD_DIM, 2, 4, HEAD_DIM), F32)
    for part in range(4):
        for half in range(2):
            wl = wl.at[:, part, :, half, part, :].set(w_cmp[part // 2, half * 16:(half + 1) * 16])
    wl = wl.reshape(16, 256, 512).astype(BF16)
    pe1 = jnp.concatenate([cmp_pe[0, 0:16]] * 2 + [cmp_pe[1, 0:16]] * 2, -1)
    pe2 = jnp.concatenate([cmp_pe[0, 16:32]] * 2 + [cmp_pe[1, 16:32]] * 2, -1)
    h1, l1 = _split_bf16(pe1)
    h2, l2 = _split_bf16(pe2)
    pe_rows = jnp.zeros((16, 16, 256), BF16)
    pe_rows = pe_rows.at[:, 0].set(h1).at[:, 1].set(h2).at[:, 2].set(l1).at[:, 3].set(l2)
    return wl, pe_rows


def _sb_ucat():
    j = np.arange(PAGE)[:, None]
    s = np.arange(PAGE)[None, :]
    u = (j > s).astype(np.float32)
    return jnp.asarray(np.concatenate([u, np.ones((PAGE, PAGE), np.float32)], 1), BF16)


def _pad_rows(a, n):
    return jnp.pad(a, ((0, 0), (0, n - a.shape[1]), (0, 0)))


def _even_layer(xp, xs, i, g0, page_table, cache_nsa, cache_nsa_win, w_in_e, g_qk_nsa, cmp_pe, w_cmp,
                gm_ws, gm_b, gm_ln, w_out_e, past):
    sp = xp.shape[0]
    nseq = page_table.shape[0]
    ts = xs.shape[0] // nseq
    qperm, gperm = _q_perm(), _gate_perm()
    w = w_in_e[i]
    wq = w[:, 0:512][:, qperm]
    wg = jnp.pad(w[:, 1280:1304][:, gperm], ((0, 0), (0, 104)))
    w_all = jnp.concatenate([wq, w[:, 512:1280], wg, w[:, 1304:2328]], -1).astype(BF16)
    gq = jnp.tile(g_qk_nsa[i, 0], 8)[None]
    gk = jnp.stack([jnp.tile(g_qk_nsa[i, r], 2) for r in (1, 2, 3)])
    ones_bd = _block_ones(512, HEAD_DIM)
    gexp = _gate_expand_mats()
    w_out = jnp.concatenate([w_out_e[i][0:512][qperm], w_out_e[i][512:1024]], 0).astype(BF16)
    wl, pe_rows = _compress_weights(cmp_pe[i], w_cmp[i])
    perm = _chunk_perm()

    def inproj(x, pos, tm, rchunk):
        cos_t, sin_t = _rope_tables(pos)
        ws_t = jnp.tril(gm_ws[i])[:, :rchunk, :rchunk]
        eye = jnp.eye(tm // rchunk, dtype=F32)
        wsg = jnp.einsum("ab,gts->gatbs", eye, ws_t).reshape(GM_GROUPS, tm, tm).astype(BF16)
        sbt = jnp.tile(jnp.repeat(gm_b[i][:, :rchunk].T, GM_WIDTH // GM_GROUPS, axis=1), (tm // rchunk, 1))
        return _inproj_even(x, g0, w_all, cos_t, sin_t, gq, gk, gm_ln[i], ones_bd, wsg, sbt, tm)

    tqp = 128
    qp, qr, rows, win, gates, v, ogm = inproj(xp, jnp.arange(sp), 256, CHUNK)
    npp = sp // PAGE
    pt_p = jnp.arange(npp, dtype=jnp.int32)[None]
    pool_p = rows.reshape(npp, PAGE, 512)
    cmp_p = _compress(pool_p, None, pt_p, perm, wl, pe_rows)
    nc = (sp - CMP_LEN) // CMP_STRIDE + 1
    ns = -(-sp // SEL_LEN)
    nsp = -(-ns // LANES) * LANES
    assert nc + 1 <= cmp_p.shape[1]
    msel = _sel_sum_matrix(cmp_p.shape[1], nsp, nc)
    o, sel = _cmp_attn(qp, cmp_p, gates, msel, gexp[0], nseq=1, tq=tqp, pos0=0, nc=nc, ns=ns,
                       k_eff=min(N_SEL, ns))
    nwb = WINDOW // tqp
    kspecs = [pl.BlockSpec((tqp, 256), (lambda s, b, j=j: (jnp.maximum(b - nwb + j, 0), 0)))
              for j in range(nwb + 1)]
    kblocks = [((j - nwb) * tqp, tqp, tqp) for j in range(nwb + 1)]
    o = _win_attn(qr, gates, gexp[2], o, [win] * (nwb + 1), kspecs, kblocks, nseq=1, tq=tqp, pos0=0)
    o = _sel_attn(qr, sel, gates, gexp[1], o, pool_p, pool_p, None, pt_p, nseq=1, tq=tqp, pos0=0, c0=0, c1=1,
                  npg=8)
    xp = _matmul_res([o, ogm], w_out, xp, 512)
    outs_p = (rows, win[sp - min(WINDOW, sp):], v[((sp - 1) // CHUNK) * CHUNK:])

    pos_s = jnp.tile(past + jnp.arange(ts), nseq)
    qp, qr, rows, win, gates, v, ogm = inproj(xs, pos_s, nseq * ts, ts)
    pool_s = jnp.transpose(cache_nsa, (0, 1, 3, 4, 5, 2))
    cmp_s = _compress(pool_s, i, page_table, perm, wl, pe_rows)
    ltot = past + ts
    nc = (ltot - CMP_LEN) // CMP_STRIDE + 1
    ns = -(-ltot // SEL_LEN)
    nsp = -(-ns // LANES) * LANES
    assert nc + 1 <= cmp_s.shape[1] and ts <= PAGE and past % PAGE == 0
    msel = _sel_sum_matrix(cmp_s.shape[1], nsp, nc)
    o, sel = _cmp_attn(qp, cmp_s, gates, msel, gexp[0], nseq=nseq, tq=ts, pos0=past, nc=nc, ns=ns,
                       k_eff=min(N_SEL, ns))
    wb = cache_nsa_win.shape[2]
    win_old = cache_nsa_win[i].reshape(nseq, wb, 256)
    win_new = _pad_rows(win.reshape(nseq, ts, 256), PAGE)
    kspecs = [pl.BlockSpec((None, wb, 256), lambda s, b: (s, 0, 0)),
              pl.BlockSpec((None, PAGE, 256), lambda s, b: (s, 0, 0))]
    kblocks = [(past - wb, 0, wb), (past, 0, PAGE)]
    o = _win_attn(qr, gates, gexp[2], o, [win_old, win_new], kspecs, kblocks, nseq=nseq, tq=ts, pos0=past)
    tail_s = _pad_rows(rows.reshape(nseq, ts, 512), PAGE)
    o = _sel_attn(qr, sel, gates, gexp[1], o, tail_s, pool_s, i, page_table, nseq=nseq, tq=ts, pos0=past,
                  c0=past // PAGE, c1=0, npg=16 if (past // PAGE) % 16 == 0 else 8)
    xs = _matmul_res([o, ogm], w_out, xs, nseq * ts)
    win_s = jnp.concatenate([win_old, win.reshape(nseq, ts, 256)], 1)[:, -wb:]
    outs_s = (rows, win_s, v)
    return xp, xs, outs_p, outs_s


def _odd_layer(xp, xs, i, g0, page_table, cache_sb, state_conv, w_in_o, cv_dw, cv_b, cv_ln, w_out_o, past):
    sp = xp.shape[0]
    nseq = page_table.shape[0]
    ts = xs.shape[0] // nseq
    w_in = w_in_o[i].astype(BF16)
    w_out = w_out_o[i].astype(BF16)
    dw = jnp.pad(cv_dw[i], ((0, 1), (0, 0)))
    aux = jnp.concatenate([cv_b[i][None], cv_ln[i], jnp.zeros((5, CV_WIDTH), F32)], 0)
    ucat = _sb_ucat()

    glu, q, rows = _inproj_odd(xp, g0, w_in, 256)
    o_cv = _conv_module(glu[None], glu[None], dw, aux, tm=256, zero_first=True)[0]
    npp = sp // PAGE
    pt_p = jnp.arange(npp, dtype=jnp.int32)[None]
    pool_p = rows.reshape(npp, PAGE, 2 * SB_WIDTH)
    o_sb = _sb_attn(q, pool_p, pool_p, None, pt_p, ucat, nseq=1, tq=PAGE, pos0=0, c0=0, c1=1)
    xp = _matmul_res([o_cv, o_sb], w_out, xp, 512)
    outs_p = (rows, glu[sp - (CV_K - 1):])

    glu, q, rows = _inproj_odd(xs, g0, w_in, nseq * ts)
    glu3 = glu.reshape(nseq, ts, CV_WIDTH)
    prev = jnp.pad(state_conv[i], ((0, 0), (32 - (CV_K - 1), 0), (0, 0)))
    o_cv = _conv_module(prev, glu3, dw, aux, tm=ts, zero_first=False).reshape(nseq * ts, CV_WIDTH)
    pool_s = jnp.transpose(cache_sb, (0, 1, 3, 4, 5, 2))
    tail_s = _pad_rows(rows.reshape(nseq, ts, 2 * SB_WIDTH), PAGE)
    o_sb = _sb_attn(q, tail_s, pool_s, i, page_table, ucat, nseq=nseq, tq=ts, pos0=past, c0=past // PAGE, c1=0)
    xs = _matmul_res([o_cv, o_sb], w_out, xs, nseq * ts)
    conv_s = jnp.concatenate([state_conv[i], glu3], 1)[:, -(CV_K - 1):]
    outs_s = (rows, conv_s)
    return xp, xs, outs_p, outs_s


def kernel(x_prompt, x_sample, mem_prompt, page_table, cache_nsa, cache_nsa_win, cache_sb, state_conv, state_ffn, cache_mem, g_norm, w_in_e, g_qk_nsa, cmp_pe, w_cmp, gm_ws, gm_b, gm_ln, w_out_e, w_in_o, cv_dw, cv_b, cv_ln, w_out_o, w_xq, w_xkv, g_xqk, w_xo, w_up, ffn_dw, ffn_db, w_down):
    bp, sp, _ = x_prompt.shape
    nseq, ts, _ = x_sample.shape
    assert bp == 1
    depth = g_norm.shape[0]
    past = page_table.shape[1] * PAGE
    mlen = mem_prompt.shape[1]
    xp = x_prompt.reshape(sp, D_MODEL)
    xs = x_sample.reshape(nseq * ts, D_MODEL)
    o = {k: [] for k in ("nsa_rows_p", "nsa_rows_s", "nsa_win_p", "nsa_win_s", "gm_v_p", "gm_v_s",
                         "sb_rows_p", "sb_rows_s", "conv_p", "conv_s", "ffn_p", "ffn_s", "memkv_p")}
    for l in range(depth):
        i = l // 2
        g0 = g_norm[l, 0][None]
        if l % 2 == 0:
            xp, xs, (rows_p, win_p, v_p), (rows_s, win_s, v_s) = _even_layer(
                xp, xs, i, g0, page_table, cache_nsa, cache_nsa_win, w_in_e, g_qk_nsa, cmp_pe, w_cmp,
                gm_ws, gm_b, gm_ln, w_out_e, past)
            o["nsa_rows_p"].append(rows_p.reshape(1, sp, 4, NSA_KV, HEAD_DIM))
            o["nsa_rows_s"].append(rows_s.reshape(nseq, ts, 4, NSA_KV, HEAD_DIM))
            o["nsa_win_p"].append(win_p.reshape(1, -1, 2, NSA_KV, HEAD_DIM))
            o["nsa_win_s"].append(win_s.reshape(nseq, -1, 2, NSA_KV, HEAD_DIM))
            o["gm_v_p"].append(v_p[None])
            o["gm_v_s"].append(v_s.reshape(nseq, ts, GM_WIDTH))
        else:
            xp, xs, (rows_p, conv_p), (rows_s, conv_s) = _odd_layer(
                xp, xs, i, g0, page_table, cache_sb, state_conv, w_in_o, cv_dw, cv_b, cv_ln, w_out_o, past)
            o["sb_rows_p"].append(rows_p.reshape(1, sp, 2, SB_HEADS, HEAD_DIM))
            o["sb_rows_s"].append(rows_s.reshape(nseq, ts, 2, SB_HEADS, HEAD_DIM))
            o["conv_p"].append(conv_p[None])
            o["conv_s"].append(conv_s)

        w_xq_b = w_xq[l].astype(BF16)
        w_xo_b = w_xo[l].astype(BF16)
        g_xk = g_xqk[l, 1][None]
        g_xq = g_xqk[l, 0][None]
        mkv = _rms_matmul(mem_prompt.reshape(mlen, D_MODEL), g_norm[l, 3][None], w_xkv[l].astype(BF16),
                          mlen, 2 * D_MODEL, D_MODEL, g_xk)
        o["memkv_p"].append(mkv.reshape(1, mlen, 2, MEM_HEADS, MEM_HD))
        g1 = g_norm[l, 1][None]
        qx = _rms_matmul(xp, g1, w_xq_b, 512, D_MODEL, D_MODEL, g_xq)
        xp = _matmul_res([_cross_attn(qx, mkv[None], nseq=1, tq=512)], w_xo_b, xp, 512)
        qx = _rms_matmul(xs, g1, w_xq_b, nseq * ts, D_MODEL, D_MODEL, g_xq)
        mkv_s = cache_mem[l].reshape(nseq, mlen, 2 * D_MODEL)
        xs = _matmul_res([_cross_attn(qx, mkv_s, nseq=nseq, tq=ts)], w_xo_b, xs, nseq * ts)

        g2 = g_norm[l, 2][None]
        w_up_b = w_up[l].astype(BF16)
        w_down_b = w_down[l].astype(BF16)
        dwb = jnp.concatenate([ffn_dw[l], ffn_db[l][None], jnp.zeros((4, 2 * D_FF), F32)], 0)
        xp, sta, stg = _convffn(xp, g2, w_up_b, dwb, w_down_b, 512, D_FF // 2)
        o["ffn_p"].append(jnp.concatenate([sta[-1, 8 - (FFN_K - 1):], stg[-1, 8 - (FFN_K - 1):]], -1)[None])
        up = _rms_matmul(xs, g2, w_up_b, nseq * ts, D_FF // 2)
        full = jnp.concatenate([state_ffn[l], up.reshape(nseq, ts, 2 * D_FF)], 1)
        taps = [full[:, k:k + ts].reshape(nseq * ts, 2 * D_FF) for k in range(FFN_K)]
        xs = _convgate_down(taps[0], taps[1], taps[2], dwb, w_down_b, xs, D_FF // 2)
        o["ffn_s"].append(full[:, -(FFN_K - 1):])

    return (xp.reshape(1, sp, D_MODEL), xs.reshape(nseq, ts, D_MODEL),
            jnp.stack(o["nsa_rows_p"]), jnp.stack(o["nsa_rows_s"]),
            jnp.stack(o["nsa_win_p"]), jnp.stack(o["nsa_win_s"]),
            jnp.stack(o["gm_v_p"]), jnp.stack(o["gm_v_s"]),
            jnp.stack(o["sb_rows_p"]), jnp.stack(o["sb_rows_s"]),
            jnp.stack(o["conv_p"]), jnp.stack(o["conv_s"]),
            jnp.stack(o["ffn_p"]), jnp.stack(o["ffn_s"]),
            jnp.stack(o["memkv_p"]))
```

```python
import functools

import numpy as np
import jax
import jax.numpy as jnp
from jax import lax
from jax.experimental import pallas as pl
from jax.experimental.pallas import tpu as pltpu

F32 = jnp.float32
BF16 = jnp.bfloat16

D_MODEL = 1024
HEAD_DIM = 64
NSA_HEADS = 8
NSA_KV = 2
CMP_LEN = 32
CMP_STRIDE = 16
SEL_LEN = 64
N_SEL = 16
WINDOW = 512
FORCE_SCORE = 1.0e4
GM_WIDTH = 512
GM_GROUPS = 8
CHUNK = 128
CV_WIDTH = 512
CV_K = 31
SB_HEADS = 8
SB_WIDTH = 512
MEM_HEADS = 4
MEM_HD = 256
D_FF = 2816
FFN_K = 3
PAGE = 128
ROPE_THETA = 10000.0
EPS = 1e-6
TINY = 1e-30
IN_E_PAD = 2432
NEG = -1e30
LOG2E = 1.4426950408889634
SB_EXIT = -120.0

LANES = 128
V7X_VMEM_LIMIT = 56 * 1024 * 1024


def _params(sem):
    return pltpu.CompilerParams(dimension_semantics=sem, vmem_limit_bytes=V7X_VMEM_LIMIT)


def _const_spec(shape):
    nd = len(shape)
    return pl.BlockSpec(shape, lambda *_: (0,) * nd)


def _dot(a, b):
    return jnp.dot(a, b, preferred_element_type=F32)


def _dot_nt(a, b):
    return lax.dot_general(a, b, (((1,), (1,)), ((), ())), preferred_element_type=F32)


def _split_bf16(x):
    hi = x.astype(BF16)
    lo = (x - hi.astype(F32)).astype(BF16)
    return hi, lo


def _dot_hilo(x, w):
    hi, lo = _split_bf16(x)
    return _dot(hi, w) + _dot(lo, w)


def _rms_rows(x, g):
    return x * lax.rsqrt(jnp.mean(x * x, axis=-1, keepdims=True) + EPS) * g


def _seg_rms(x, ones_bd, g, width):
    ss = _dot((x * x).astype(BF16), ones_bd)
    return x * lax.rsqrt(ss * (1.0 / width) + EPS) * g


def _sigmoid(x):
    return 1.0 / (1.0 + jnp.exp(-x))


def _gelu_tanh(x):
    return 0.5 * x * (1.0 + jnp.tanh(0.7978845608028654 * (x + 0.044715 * (x * x * x))))


def _rope_blk(x, cos_t, sin_t, lo32):
    sw = jnp.where(lo32, pltpu.roll(x, 96, 1), pltpu.roll(x, 32, 1))
    return x * cos_t + sw * sin_t


def _inproj_even_kernel(x_ref, g_ref, w_ref, cos_ref, sin_ref, gq_ref, gk_ref, ln_ref, ones_ref,
                        wsg_ref, sbt_ref,
                        qp_ref, qr_ref, rows_ref, win_ref, gates_ref, v_ref, ogm_ref):
    x = x_ref[...]
    tm = x.shape[0]
    h = _rms_rows(x, g_ref[...]).astype(BF16)
    z = _dot(h, w_ref[...])
    lane = lax.broadcasted_iota(jnp.int32, (tm, LANES), 1)
    lo32 = (lane & 32) == 0
    lo64 = lane < 64
    cos_t = cos_ref[...]
    sin_t = sin_ref[...]
    ones128 = ones_ref[0:LANES, 0:LANES]
    gk = gk_ref[...]

    q = _seg_rms(z[:, 0:512], ones_ref[...], gq_ref[...], HEAD_DIM)
    qp_ref[...] = q
    for i in range(4):
        sl = slice(i * LANES, (i + 1) * LANES)
        qr_ref[:, sl] = _rope_blk(q[:, sl], cos_t, sin_t, lo32)

    rows_ref[:, 0:128] = _seg_rms(z[:, 512:640], ones128, gk[0:1], HEAD_DIM)
    rows_ref[:, 128:256] = z[:, 640:768]
    ks_k = _seg_rms(z[:, 768:896], ones128, gk[1:2], HEAD_DIM)
    rows_ref[:, 256:384] = _rope_blk(ks_k, cos_t, sin_t, lo32)
    rows_ref[:, 384:512] = z[:, 896:1024]
    kw_k = _seg_rms(z[:, 1024:1152], ones128, gk[2:3], HEAD_DIM)
    win_ref[:, 0:128] = _rope_blk(kw_k, cos_t, sin_t, lo32)
    win_ref[:, 128:256] = z[:, 1152:1280]

    gates_ref[...] = _sigmoid(z[:, 1280:1408])

    u = _gelu_tanh(z[:, 1408:1920])
    vv = _gelu_tanh(z[:, 1920:2432])
    ln = ln_ref[...]
    vc = vv - jnp.mean(vv, axis=-1, keepdims=True)
    v = vc * lax.rsqrt(jnp.mean(vc * vc, axis=-1, keepdims=True) + EPS) * ln[0:1] + ln[1:2]
    v_ref[...] = v
    vb = v.astype(BF16)
    for i in range(4):
        sl = slice(i * LANES, (i + 1) * LANES)
        m_lo = _dot(wsg_ref[2 * i], vb[:, sl])
        m_hi = _dot(wsg_ref[2 * i + 1], vb[:, sl])
        mixed = jnp.where(lo64, m_lo, m_hi) + sbt_ref[:, sl]
        ogm_ref[:, sl] = u[:, sl] * mixed


def _inproj_even(x, g, w, cos_t, sin_t, gq, gk, ln, ones_bd, wsg, sbt, tm):
    t = x.shape[0]
    row = lambda n: pl.BlockSpec((tm, n), lambda i: (i, 0))
    outs = [(512, F32), (512, F32), (512, F32), (256, F32), (128, F32), (512, F32), (512, F32)]
    return pl.pallas_call(
        _inproj_even_kernel,
        grid=(t // tm,),
        in_specs=[row(D_MODEL), _const_spec((1, D_MODEL)), _const_spec(w.shape), row(128), row(128),
                  _const_spec(gq.shape), _const_spec(gk.shape), _const_spec(ln.shape),
                  _const_spec(ones_bd.shape), _const_spec(wsg.shape), _const_spec(sbt.shape)],
        out_specs=[row(n) for n, _ in outs],
        out_shape=[jax.ShapeDtypeStruct((t, n), d) for n, d in outs],
        compiler_params=_params(("arbitrary",)),
        name="inproj_even",
    )(x, g, w, cos_t, sin_t, gq, gk, ln, ones_bd, wsg, sbt)


def _compress_kernel(pt_ref, *refs, npg, nsteps, nch):
    page_refs = refs[:npg]
    perm_ref, wl_ref, pe_ref, out_ref, xl_ref, a2_ref = refs[npg:]
    k = pl.program_id(1)

    @pl.when(k == 0)
    def _():
        for l in range(16):
            xl_ref[l, nch:nch + 16, :] = pe_ref[l]

    perm = perm_ref[...]
    for pp in range(npg // 2):
        ra, rb = page_refs[2 * pp], page_refs[2 * pp + 1]
        if len(ra.shape) == 2:
            pg = jnp.concatenate([ra[...], rb[...]], axis=0).astype(BF16)
            xp = _dot(perm, pg)
        else:
            pg = jnp.concatenate([ra[...].reshape(256, PAGE), rb[...].reshape(256, PAGE)], axis=1).astype(BF16)
            xp = _dot_nt(perm, pg)
        xp = xp.astype(BF16)
        row0 = pl.multiple_of((k * (npg // 2) + pp) * 16, 16)
        for l in range(16):
            xl_ref[l, pl.ds(row0, 16), :] = xp[l * 16:(l + 1) * 16]

    @pl.when(k == nsteps - 1)
    def _():
        acc = _dot(xl_ref[0], wl_ref[0])
        for l in range(1, 16):
            acc = acc + _dot(xl_ref[l], wl_ref[l])
        bias = (acc[nch:nch + 1, 0:256] + acc[nch + 2:nch + 3, 0:256]
                + acc[nch + 1:nch + 2, 256:512] + acc[nch + 3:nch + 4, 256:512])
        a2_ref[0:nch, :] = acc[0:nch, 256:512]
        a2_ref[nch:nch + 8, :] = jnp.zeros((8, 256), F32)
        out_ref[...] = acc[0:nch, 0:256] + a2_ref[1:nch + 1, :] + bias


def _page_spec(pool, layer, half, phys_fn):
    h = 0 if half is None else half
    if pool.ndim == 3:
        w = pool.shape[2] if half is None else pool.shape[2] // 2
        return pl.BlockSpec((None, PAGE, w), lambda *a: (phys_fn(*a), 0, h))
    ty = pool.shape[2] if half is None else pool.shape[2] // 2
    return pl.BlockSpec((None, None, ty) + tuple(pool.shape[3:]), lambda *a: (layer, phys_fn(*a), h, 0, 0, 0))


def _compress(pool, layer, page_table, perm, wl, pe_rows):
    nseq, npages = page_table.shape
    npg = 16 if npages % 16 == 0 else 8
    nsteps = npages // npg
    nch = npages * (PAGE // CMP_STRIDE)

    def page_spec(p):
        return _page_spec(pool, layer, 0, lambda s, k, pt: pt[s, k * npg + p])

    grid_spec = pltpu.PrefetchScalarGridSpec(
        num_scalar_prefetch=1,
        grid=(nseq, nsteps),
        in_specs=[page_spec(p) for p in range(npg)] + [
            pl.BlockSpec(perm.shape, lambda s, k, pt: (0, 0)),
            pl.BlockSpec(wl.shape, lambda s, k, pt: (0, 0, 0)),
            pl.BlockSpec(pe_rows.shape, lambda s, k, pt: (0, 0, 0))],
        out_specs=pl.BlockSpec((None, nch, 256), lambda s, k, pt: (s, 0, 0)),
        scratch_shapes=[pltpu.VMEM((16, nch + 16, 256), BF16), pltpu.VMEM((nch + 8, 256), F32)],
    )
    return pl.pallas_call(
        functools.partial(_compress_kernel, npg=npg, nsteps=nsteps, nch=nch),
        grid_spec=grid_spec,
        out_shape=jax.ShapeDtypeStruct((nseq, nch, 256), F32),
        compiler_params=_params(("arbitrary", "arbitrary")),
        name="nsa_compress",
    )(page_table, *([pool] * npg), perm, wl, pe_rows)


def _stack_q(q, lo64):
    blks = [q[:, i * LANES:(i + 1) * LANES] for i in range(4)]
    slabs = [jnp.where(lo64, b, 0.0) for b in blks] + [jnp.where(lo64, 0.0, b) for b in blks]
    return jnp.concatenate(slabs, axis=0).astype(BF16)


def _unstack_o(o, tq, lo64):
    return [jnp.where(lo64, o[i * tq:(i + 1) * tq], o[(4 + i) * tq:(5 + i) * tq]) for i in range(4)]


def _gate_expand(gates, gexp):
    return _dot_hilo(gates, gexp)


def _masked_softmax_rows(s, mask):
    s = jnp.where(mask, s, NEG)
    m = jnp.max(s, axis=-1, keepdims=True)
    p = jnp.where(mask, jnp.exp(s - m), 0.0)
    return p / jnp.maximum(jnp.sum(p, axis=-1, keepdims=True), TINY)


def _masked_softmax_stacked(s_all, mask, tq):
    n = s_all.shape[1]
    return _masked_softmax_rows(s_all.reshape(8, tq, n), mask[None]).reshape(8 * tq, n)


def _cmp_attn_kernel(qp_ref, cmp_ref, gates_ref, msel_ref, gexp_ref, oc_ref, sel_ref, *,
                     tq, pos0, nc, ns, nsp, nch, k_eff):
    qb = pl.program_id(1)
    lane = lax.broadcasted_iota(jnp.int32, (tq, LANES), 1)
    lo64 = lane < 64
    qs = _stack_q(qp_ref[...], lo64)
    cm = cmp_ref[...]
    kc = cm[:, 0:128].astype(BF16)
    vc = cm[:, 128:256].astype(BF16)
    s_all = _dot_nt(qs, kc) * (HEAD_DIM ** -0.5)
    qpos = pos0 + qb * tq + lax.broadcasted_iota(jnp.int32, (tq, 1), 0)
    nidx = lax.broadcasted_iota(jnp.int32, (tq, nch), 1)
    cmask = (nidx * CMP_STRIDE + (CMP_LEN - 1) <= qpos) & (nidx < nc)

    blocks_on_rows = tq % LANES == 0
    msel = msel_ref[...]
    ax = 0 if blocks_on_rows else 1
    if blocks_on_rows:
        qpos_b = pos0 + qb * tq + lax.broadcasted_iota(jnp.int32, (1, tq), 1)
    else:
        qpos_b = qpos
    jidx = lax.broadcasted_iota(jnp.int32, (nsp, tq) if blocks_on_rows else (tq, nsp), ax)
    jf = jidx.astype(F32)
    causal_ok = jidx * SEL_LEN <= qpos_b
    forced = (jidx == jnp.right_shift(qpos_b, 6)) | (jidx == 0)
    p_all = _masked_softmax_stacked(s_all, cmask, tq)
    o = _dot(p_all.astype(BF16), vc)
    for kv in range(NSA_KV):
        pg = jnp.sum(p_all[kv * 4 * tq:(kv + 1) * 4 * tq].reshape(4, tq, nch), axis=0)
        if blocks_on_rows:
            hi, lo = _split_bf16(pg)
            p_slc = _dot_nt(msel, hi) + _dot_nt(msel, lo)
        else:
            p_slc = _dot_hilo(pg, msel)
        score = jnp.where(causal_ok, jnp.where(forced, FORCE_SCORE, p_slc), -1.0)
        score = jnp.where(jidx < ns, score, -3.0e38)
        work = score
        for _ in range(k_eff):
            m = jnp.max(work, axis=ax, keepdims=True)
            idx = jnp.min(jnp.where(work == m, jf, 1.0e9), axis=ax, keepdims=True)
            work = jnp.where(jf == idx, -jnp.inf, work)
        picked = jnp.where((work == -jnp.inf) & (score >= 0.0), 1.0, 0.0)
        sel_ref[:, kv * nsp:(kv + 1) * nsp] = picked.T if blocks_on_rows else picked
    gx = _gate_expand(gates_ref[...], gexp_ref[...])
    for i, ob in enumerate(_unstack_o(o, tq, lo64)):
        sl = slice(i * LANES, (i + 1) * LANES)
        oc_ref[:, sl] = gx[:, sl] * ob


def _cmp_attn(qp, cmp, gates, msel, gexp, *, nseq, tq, pos0, nc, ns, k_eff):
    t = qp.shape[0]
    nqb = t // (nseq * tq)
    nch = cmp.shape[1]
    nsp = msel.shape[1]
    if tq % LANES == 0:
        msel = msel.T
    row = lambda n: pl.BlockSpec((tq, n), lambda s, b: (s * nqb + b, 0))
    return pl.pallas_call(
        functools.partial(_cmp_attn_kernel, tq=tq, pos0=pos0, nc=nc, ns=ns, nsp=nsp, nch=nch, k_eff=k_eff),
        grid=(nseq, nqb),
        in_specs=[row(512), pl.BlockSpec((None, nch, 256), lambda s, b: (s, 0, 0)), row(128),
                  _const_spec(msel.shape), _const_spec(gexp.shape)],
        out_specs=[row(512), row(2 * nsp)],
        out_shape=[jax.ShapeDtypeStruct((t, 512), F32), jax.ShapeDtypeStruct((t, 2 * nsp), F32)],
        compiler_params=_params(("arbitrary", "arbitrary")),
        name="nsa_cmp_select",
    )(qp, cmp, gates, msel, gexp)


def _win_attn_kernel(*refs, tq, pos0, kblocks):
    nkb = len(kblocks)
    qr_ref, gates_ref, gexp_ref, oin_ref = refs[:4]
    k_refs = refs[4:4 + nkb]
    out_ref = refs[4 + nkb]
    qb = pl.program_id(1)
    lane = lax.broadcasted_iota(jnp.int32, (tq, LANES), 1)
    lo64 = lane < 64
    qs = _stack_q(qr_ref[...], lo64)
    kv = jnp.concatenate([r[...] for r in k_refs], axis=0)
    kk = kv[:, 0:128].astype(BF16)
    vv = kv[:, 128:256].astype(BF16)
    qpos = pos0 + qb * tq + lax.broadcasted_iota(jnp.int32, (tq, 1), 0)
    kpos = jnp.concatenate(
        [c0 + c1 * qb + lax.broadcasted_iota(jnp.int32, (tq, n), 1) for (c0, c1, n) in kblocks], axis=1)
    mask = (kpos <= qpos) & (kpos > qpos - WINDOW) & (kpos >= 0)
    s_all = _dot_nt(qs, kk) * (HEAD_DIM ** -0.5)
    o = _dot(_masked_softmax_stacked(s_all, mask, tq).astype(BF16), vv)
    gx = _gate_expand(gates_ref[...], gexp_ref[...])
    for i, ob in enumerate(_unstack_o(o, tq, lo64)):
        sl = slice(i * LANES, (i + 1) * LANES)
        out_ref[:, sl] = oin_ref[:, sl] + gx[:, sl] * ob


def _win_attn(qr, gates, gexp, o_in, key_arrays, key_specs, kblocks, *, nseq, tq, pos0):
    t = qr.shape[0]
    nqb = t // (nseq * tq)
    row = lambda n: pl.BlockSpec((tq, n), lambda s, b: (s * nqb + b, 0))
    return pl.pallas_call(
        functools.partial(_win_attn_kernel, tq=tq, pos0=pos0, kblocks=tuple(kblocks)),
        grid=(nseq, nqb),
        in_specs=[row(512), row(128), _const_spec(gexp.shape), row(512)] + list(key_specs),
        out_specs=row(512),
        out_shape=jax.ShapeDtypeStruct((t, 512), F32),
        compiler_params=_params(("arbitrary", "arbitrary")),
        name="nsa_window",
    )(qr, gates, gexp, o_in, *key_arrays)


def _sel_schedule(nqb, npg, c0, c1):
    page_steps = lambda b: (c0 + c1 * b + npg - 1) // npg
    if c1 == 0 or nqb % 2:
        return nqb, page_steps(0 if c1 == 0 else nqb - 1) + 1, lambda p, k: (p, k, page_steps(p))
    total = max(page_steps(b) + page_steps(nqb - 1 - b) + 2 for b in range(nqb // 2))

    def sched(p, k):
        n1 = page_steps(p) + 1
        first = k < n1
        b = jnp.where(first, p, nqb - 1 - p)
        return b, jnp.where(first, k, k - n1), page_steps(b)

    return nqb // 2, total, sched


def _sel_attn_kernel(pt_ref, *refs, tq, pos0, npg, nsp, c0, c1, sched, stream_e):
    qr_ref, sel_ref, gates_ref, gexp_ref, oin_ref, tail_ref, e_ref, etail_ref = refs[:8]
    page_refs = refs[8:8 + npg]
    out_ref, qs_ref, m_ref, acc_ref = refs[8 + npg:]
    qb, k, na = sched(pl.program_id(1), pl.program_id(2))
    npast = c0 + c1 * qb
    lane = lax.broadcasted_iota(jnp.int32, (tq, LANES), 1)
    lo64 = lane < 64

    @pl.when(k == 0)
    def _():
        qs_ref[...] = _stack_q(qr_ref[...] * (LOG2E * HEAD_DIM ** -0.5), lo64)
        m_ref[...] = jnp.full(m_ref.shape, NEG, F32)
        acc_ref[...] = jnp.zeros(acc_ref.shape, F32)

    def update(s_all, pv, biases):
        nk = s_all.shape[1]
        for kvi in range(NSA_KV):
            rows = slice(kvi * 4 * tq, (kvi + 1) * 4 * tq)
            s = (s_all[rows].reshape(4, tq, nk) + biases[kvi][None]).reshape(4 * tq, nk)
            m_old = m_ref[rows]
            m_new = jnp.maximum(m_old, jnp.max(s, axis=-1, keepdims=True))
            p = jnp.exp2(s - m_new)
            acc_ref[rows] = jnp.exp2(m_old - m_new) * acc_ref[rows] + pv(p.astype(BF16), kvi)
            m_ref[rows] = m_new

    def with_ones(v, axis):
        first = lax.broadcasted_iota(jnp.int32, v.shape, axis) < HEAD_DIM
        return jnp.where(first, v, 1.0).astype(BF16), jnp.where(first, 1.0, v).astype(BF16)

    def block_biases(j0, nk, extra, tail):
        shift = lax.rem(nsp - j0, nsp)
        out = []
        for kvi in range(NSA_KV):
            sel = sel_ref[:, kvi * nsp:(kvi + 1) * nsp]
            if stream_e:
                b = _dot(sel.astype(BF16), (etail_ref if tail else e_ref)[...])
            else:
                b = _dot(pltpu.roll(sel, shift, 1)[:, 0:LANES].astype(BF16), e_ref[:, 0:nk])
            b = (b - 1.0) * (-NEG)
            out.append(b if extra is None else jnp.where(extra, b, NEG))
        return out

    def rows_update(kv, biases):
        vs = with_ones(kv[:, 128:256], 1)
        update(_dot_nt(qs_ref[...], kv[:, 0:128].astype(BF16)), lambda p, kvi: _dot(p, vs[kvi]), biases)

    @pl.when(k < na)
    def _():
        nk = npg * PAGE
        if c1 == 0 and c0 % npg == 0:
            extra = None
        else:
            extra = k * npg + jnp.right_shift(lax.broadcasted_iota(jnp.int32, (tq, nk), 1), 7) < npast
        biases = block_biases(k * npg * (PAGE // SEL_LEN), nk, extra, False)
        if len(page_refs[0].shape) == 2:
            rows_update(jnp.concatenate([r[...] for r in page_refs], axis=0), biases)
        else:
            kt = jnp.concatenate([r[0].reshape(LANES, PAGE) for r in page_refs], axis=1).astype(BF16)
            vts = with_ones(jnp.concatenate([r[1].reshape(LANES, PAGE) for r in page_refs], axis=1), 0)
            update(_dot(qs_ref[...], kt), lambda p, kvi: _dot_nt(p, vts[kvi]), biases)

    @pl.when(k == na)
    def _():
        qpos = pos0 + qb * tq + lax.broadcasted_iota(jnp.int32, (tq, 1), 0)
        kpos = npast * PAGE + lax.broadcasted_iota(jnp.int32, (tq, PAGE), 1)
        rows_update(tail_ref[...], block_biases(npast * (PAGE // SEL_LEN), PAGE, kpos <= qpos, True))
        acc = acc_ref[...]
        o = jnp.where(m_ref[...] > 0.5 * NEG, acc / jnp.maximum(pltpu.roll(acc, HEAD_DIM, 1), TINY), 0.0)
        gx = _gate_expand(gates_ref[...], gexp_ref[...])
        for i, ob in enumerate(_unstack_o(o, tq, lo64)):
            sl = slice(i * LANES, (i + 1) * LANES)
            out_ref[:, sl] = oin_ref[:, sl] + gx[:, sl] * ob


def _sel_attn(qr, sel, gates, gexp, o_in, tail3, pool, layer, page_table, *, nseq, tq, pos0, c0, c1, npg):
    t = qr.shape[0]
    nqb = t // (nseq * tq)
    npages = page_table.shape[1]
    assert npages % npg == 0
    nsp = sel.shape[1] // 2
    nrows, nsteps, sched = _sel_schedule(nqb, npg, c0, c1)
    blk = lambda p, k: sched(p, k)[0]
    row = lambda n: pl.BlockSpec((tq, n), lambda s, p, k, pt: (s * nqb + blk(p, k), 0))
    stream_e = c1 != 0
    if stream_e:
        e_map = (jnp.arange((npages + 1) * PAGE)[None, :] // SEL_LEN == jnp.arange(nsp)[:, None]).astype(BF16)

        def e_index(s, p, k, pt):
            _, kl, na = sched(p, k)
            return (0, jnp.clip(kl, 0, jnp.maximum(na - 1, 0)))

        e_specs = [pl.BlockSpec((nsp, npg * PAGE), e_index),
                   pl.BlockSpec((nsp, PAGE), lambda s, p, k, pt: (0, c0 + c1 * blk(p, k)))]
    else:
        assert npg * (PAGE // SEL_LEN) <= LANES
        e_map = (jnp.arange(npg * PAGE)[None, :] // SEL_LEN == jnp.arange(LANES)[:, None]).astype(BF16)
        e_specs = [pl.BlockSpec(e_map.shape, lambda s, p, k, pt: (0, 0))] * 2

    def page_spec(slot):
        def phys(s, p, k, pt):
            b, kl, _ = sched(p, k)
            last = jnp.maximum(c0 + c1 * b - 1, 0)
            return pt[s, jnp.minimum(jnp.minimum(kl * npg + slot, last), npages - 1)]
        return _page_spec(pool, layer, 1, phys)

    grid_spec = pltpu.PrefetchScalarGridSpec(
        num_scalar_prefetch=1,
        grid=(nseq, nrows, nsteps),
        in_specs=[row(512), row(2 * nsp), row(128),
                  pl.BlockSpec(gexp.shape, lambda s, p, k, pt: (0, 0)), row(512),
                  pl.BlockSpec((None, PAGE, 256), lambda s, p, k, pt: (s * nqb + blk(p, k), 0, 1))]
                 + e_specs + [page_spec(slot) for slot in range(npg)],
        out_specs=row(512),
        scratch_shapes=[pltpu.VMEM((8 * tq, LANES), BF16), pltpu.VMEM((8 * tq, 1), F32),
                        pltpu.VMEM((8 * tq, LANES), F32)],
    )
    return pl.pallas_call(
        functools.partial(_sel_attn_kernel, tq=tq, pos0=pos0, npg=npg, nsp=nsp, c0=c0, c1=c1, sched=sched,
                          stream_e=stream_e),
        grid_spec=grid_spec,
        out_shape=jax.ShapeDtypeStruct((t, 512), F32),
        compiler_params=_params(("arbitrary", "arbitrary", "arbitrary")),
        name="nsa_selected",
    )(page_table, qr, sel, gates, gexp, o_in, tail3, e_map, e_map, *([pool] * npg))


def _rms_matmul_kernel(x_ref, g_ref, w_ref, *rest, n_norm):
    if n_norm:
        gseg_ref, out_ref = rest
    else:
        (out_ref,) = rest
    h = _rms_rows(x_ref[...], g_ref[...]).astype(BF16)
    z = _dot(h, w_ref[...])
    if n_norm:
        for c0 in range(0, n_norm, MEM_HD):
            out_ref[:, c0:c0 + MEM_HD] = _rms_rows(z[:, c0:c0 + MEM_HD], gseg_ref[...])
        if n_norm < z.shape[1]:
            out_ref[:, n_norm:] = z[:, n_norm:]
    else:
        out_ref[...] = z


def _rms_matmul(x, g, w, tm, tn, n_norm=0, gseg=None):
    t, n = x.shape[0], w.shape[1]
    extra, extra_specs = [], []
    if n_norm:
        assert tn == n
        extra = [gseg]
        extra_specs = [_const_spec(gseg.shape)]
    return pl.pallas_call(
        functools.partial(_rms_matmul_kernel, n_norm=n_norm),
        grid=(t // tm, n // tn),
        in_specs=[pl.BlockSpec((tm, D_MODEL), lambda i, j: (i, 0)), _const_spec((1, D_MODEL)),
                  pl.BlockSpec((D_MODEL, tn), lambda i, j: (0, j))] + extra_specs,
        out_specs=pl.BlockSpec((tm, tn), lambda i, j: (i, j)),
        out_shape=jax.ShapeDtypeStruct((t, n), F32),
        compiler_params=_params(("arbitrary", "arbitrary")),
        name="rms_matmul",
    )(x, g, w, *extra)


def _matmul_res_kernel(*refs, na):
    a_refs = refs[:na]
    w_ref, x_ref, out_ref = refs[na:]
    acc = x_ref[...]
    k0 = 0
    for a_ref in a_refs:
        kw = a_ref.shape[1]
        acc = acc + _dot(a_ref[...].astype(BF16), w_ref[k0:k0 + kw, :])
        k0 += kw
    out_ref[...] = acc


def _matmul_res(acts, w, x, tm):
    t = x.shape[0]
    return pl.pallas_call(
        functools.partial(_matmul_res_kernel, na=len(acts)),
        grid=(t // tm,),
        in_specs=[pl.BlockSpec((tm, a.shape[1]), lambda i: (i, 0)) for a in acts]
                 + [_const_spec(w.shape), pl.BlockSpec((tm, D_MODEL), lambda i: (i, 0))],
        out_specs=pl.BlockSpec((tm, D_MODEL), lambda i: (i, 0)),
        out_shape=jax.ShapeDtypeStruct((t, D_MODEL), F32),
        compiler_params=_params(("arbitrary",)),
        name="matmul_residual",
    )(*acts, w, x)


def _cross_attn_kernel(q_ref, mkv_ref, out_ref):
    for h in range(MEM_HEADS):
        sl = slice(h * MEM_HD, (h + 1) * MEM_HD)
        q = q_ref[:, sl].astype(BF16)
        if len(mkv_ref.shape) == 2:
            kk = mkv_ref[:, sl].astype(BF16)
            vv = mkv_ref[:, D_MODEL + h * MEM_HD:D_MODEL + (h + 1) * MEM_HD].astype(BF16)
        else:
            kk = mkv_ref[:, 0, h, :].astype(BF16)
            vv = mkv_ref[:, 1, h, :].astype(BF16)
        s = _dot_nt(q, kk) * (MEM_HD ** -0.5)
        m = jnp.max(s, axis=-1, keepdims=True)
        p = jnp.exp(s - m)
        p = p / jnp.sum(p, axis=-1, keepdims=True)
        out_ref[:, sl] = _dot(p.astype(BF16), vv)


def _cross_attn(q, mkv, layer, *, nseq, tq):
    t = q.shape[0]
    nqb = t // (nseq * tq)
    if mkv.ndim == 3:
        mkv_spec = pl.BlockSpec((None,) + tuple(mkv.shape[1:]), lambda s, b: (s, 0, 0))
    else:
        mkv_spec = pl.BlockSpec((None, None) + tuple(mkv.shape[2:]), lambda s, b: (layer, s, 0, 0, 0, 0))
    return pl.pallas_call(
        _cross_attn_kernel,
        grid=(nseq, nqb),
        in_specs=[pl.BlockSpec((tq, D_MODEL), lambda s, b: (s * nqb + b, 0)), mkv_spec],
        out_specs=pl.BlockSpec((tq, D_MODEL), lambda s, b: (s * nqb + b, 0)),
        out_shape=jax.ShapeDtypeStruct((t, D_MODEL), F32),
        compiler_params=_params(("arbitrary", "arbitrary")),
        name="cross_attn",
    )(q, mkv)


def _cross_block_kernel(x_ref, g_ref, wq_ref, gq_ref, mkv_ref, wo_ref, out_ref):
    x = x_ref[...]
    z = _dot(_rms_rows(x, g_ref[...]).astype(BF16), wq_ref[...])
    acc = x
    for h in range(MEM_HEADS):
        sl = slice(h * MEM_HD, (h + 1) * MEM_HD)
        q = _rms_rows(z[:, sl], gq_ref[...]).astype(BF16)
        kk = mkv_ref[:, sl].astype(BF16)
        vv = mkv_ref[:, D_MODEL + h * MEM_HD:D_MODEL + (h + 1) * MEM_HD].astype(BF16)
        s = _dot_nt(q, kk) * (MEM_HD ** -0.5)
        p = jnp.exp(s - jnp.max(s, axis=-1, keepdims=True))
        p = p / jnp.sum(p, axis=-1, keepdims=True)
        acc = acc + _dot(_dot(p.astype(BF16), vv).astype(BF16), wo_ref[sl, :])
    out_ref[...] = acc


def _cross_block(x, g, wq, gq, mkv, wo, tm):
    t = x.shape[0]
    return pl.pallas_call(
        _cross_block_kernel,
        grid=(t // tm,),
        in_specs=[pl.BlockSpec((tm, D_MODEL), lambda i: (i, 0)), _const_spec((1, D_MODEL)), _const_spec(wq.shape),
                  _const_spec(gq.shape), _const_spec(mkv.shape), _const_spec(wo.shape)],
        out_specs=pl.BlockSpec((tm, D_MODEL), lambda i: (i, 0)),
        out_shape=jax.ShapeDtypeStruct((t, D_MODEL), F32),
        compiler_params=_params(("arbitrary",)),
        name="cross_block",
    )(x, g, wq, gq, mkv, wo)


def _convffn_kernel(x_ref, g_ref, wa_ref, wg_ref, dwa_ref, dwg_ref, wd_ref,
                    out_ref, sta_ref, stg_ref,
                    hn_ref, acc_ref, sa_ref, sg_ref, ca_ref, cg_ref, *, tm, nff):
    i = pl.program_id(0)
    j = pl.program_id(1)

    @pl.when(j == 0)
    def _():
        hn_ref[...] = _rms_rows(x_ref[...], g_ref[...]).astype(BF16)
        acc_ref[...] = jnp.zeros(acc_ref.shape, F32)

    @pl.when(i == 0)
    def _():
        ca_ref[j] = jnp.zeros(ca_ref.shape[1:], F32)
        cg_ref[j] = jnp.zeros(cg_ref.shape[1:], F32)

    hn = hn_ref[...]

    def conv(w_ref, s_ref, c_ref, dw_ref, st_ref):
        u = _dot(hn, w_ref[...])
        s_ref[0:8, :] = c_ref[j]
        s_ref[8:8 + tm, :] = u
        c_ref[j] = u[tm - 8:tm]
        st_ref[...] = u[tm - 8:tm]
        dw = dw_ref[...]
        return dw[0:1] * s_ref[6:6 + tm, :] + dw[1:2] * s_ref[7:7 + tm, :] + dw[2:3] * u + dw[3:4]

    a = conv(wa_ref, sa_ref, ca_ref, dwa_ref, sta_ref)
    g = conv(wg_ref, sg_ref, cg_ref, dwg_ref, stg_ref)
    y = (g * _sigmoid(g)) * a
    acc_ref[...] += _dot(y.astype(BF16), wd_ref[...])

    @pl.when(j == nff - 1)
    def _():
        out_ref[...] = x_ref[...] + acc_ref[...]


def _convffn(x, g, w_up, dwb, w_down, tm, tf):
    t = x.shape[0]
    nff = D_FF // tf
    nt = t // tm
    return pl.pallas_call(
        functools.partial(_convffn_kernel, tm=tm, nff=nff),
        grid=(nt, nff),
        in_specs=[pl.BlockSpec((tm, D_MODEL), lambda i, j: (i, 0)), _const_spec((1, D_MODEL)),
                  pl.BlockSpec((D_MODEL, tf), lambda i, j: (0, j)),
                  pl.BlockSpec((D_MODEL, tf), lambda i, j: (0, nff + j)),
                  pl.BlockSpec((8, tf), lambda i, j: (0, j)),
                  pl.BlockSpec((8, tf), lambda i, j: (0, nff + j)),
                  pl.BlockSpec((tf, D_MODEL), lambda i, j: (j, 0))],
        out_specs=[pl.BlockSpec((tm, D_MODEL), lambda i, j: (i, 0)),
                   pl.BlockSpec((None, 8, tf), lambda i, j: (i, 0, j)),
                   pl.BlockSpec((None, 8, tf), lambda i, j: (i, 0, j))],
        out_shape=[jax.ShapeDtypeStruct((t, D_MODEL), F32),
                   jax.ShapeDtypeStruct((nt, 8, D_FF), F32),
                   jax.ShapeDtypeStruct((nt, 8, D_FF), F32)],
        scratch_shapes=[pltpu.VMEM((tm, D_MODEL), BF16), pltpu.VMEM((tm, D_MODEL), F32),
                        pltpu.VMEM((tm + 8, tf), F32), pltpu.VMEM((tm + 8, tf), F32),
                        pltpu.VMEM((nff, 8, tf), F32), pltpu.VMEM((nff, 8, tf), F32)],
        compiler_params=_params(("arbitrary", "arbitrary")),
        name="convffn",
    )(x, g, w_up, w_up, dwb, dwb, w_down)


def _convgate_down_kernel(a0_ref, a1_ref, a2_ref, g0_ref, g1_ref, g2_ref, dwa_ref, dwg_ref, wd_ref, x_ref,
                          out_ref, acc_ref, *, nff):
    j = pl.program_id(0)

    @pl.when(j == 0)
    def _():
        acc_ref[...] = jnp.zeros(acc_ref.shape, F32)

    dwa = dwa_ref[...]
    dwg = dwg_ref[...]
    a = dwa[0:1] * a0_ref[...] + dwa[1:2] * a1_ref[...] + dwa[2:3] * a2_ref[...] + dwa[3:4]
    g = dwg[0:1] * g0_ref[...] + dwg[1:2] * g1_ref[...] + dwg[2:3] * g2_ref[...] + dwg[3:4]
    y = (g * _sigmoid(g)) * a
    acc_ref[...] += _dot(y.astype(BF16), wd_ref[...])

    @pl.when(j == nff - 1)
    def _():
        out_ref[...] = x_ref[...] + acc_ref[...]


def _convgate_down(f0, f1, f2, dwb, w_down, x, tf):
    t = x.shape[0]
    nff = D_FF // tf
    fa = pl.BlockSpec((t, tf), lambda j: (0, j))
    fg = pl.BlockSpec((t, tf), lambda j: (0, nff + j))
    return pl.pallas_call(
        functools.partial(_convgate_down_kernel, nff=nff),
        grid=(nff,),
        in_specs=[fa, fa, fa, fg, fg, fg,
                  pl.BlockSpec((8, tf), lambda j: (0, j)), pl.BlockSpec((8, tf), lambda j: (0, nff + j)),
                  pl.BlockSpec((tf, D_MODEL), lambda j: (j, 0)), _const_spec((t, D_MODEL))],
        out_specs=_const_spec((t, D_MODEL)),
        out_shape=jax.ShapeDtypeStruct((t, D_MODEL), F32),
        scratch_shapes=[pltpu.VMEM((t, D_MODEL), F32)],
        compiler_params=_params(("arbitrary",)),
        name="convgate_down",
    )(f0, f1, f2, f0, f1, f2, dwb, dwb, w_down, x)


def _inproj_odd_kernel(x_ref, g_ref, w_ref, glu_ref, q_ref, rows_ref):
    h = _rms_rows(x_ref[...], g_ref[...]).astype(BF16)
    z = _dot(h, w_ref[...])
    glu_ref[...] = z[:, 0:512] * _sigmoid(z[:, 512:1024])
    q_ref[...] = z[:, 1024:1536]
    rows_ref[...] = z[:, 1536:2560]


def _inproj_odd(x, g, w, tm):
    t = x.shape[0]
    row = lambda n: pl.BlockSpec((tm, n), lambda i: (i, 0))
    return pl.pallas_call(
        _inproj_odd_kernel,
        grid=(t // tm,),
        in_specs=[row(D_MODEL), _const_spec((1, D_MODEL)), _const_spec(w.shape)],
        out_specs=[row(512), row(512), row(1024)],
        out_shape=[jax.ShapeDtypeStruct((t, 512), F32), jax.ShapeDtypeStruct((t, 512), F32),
                   jax.ShapeDtypeStruct((t, 1024), F32)],
        compiler_params=_params(("arbitrary",)),
        name="inproj_odd",
    )(x, g, w)


def _conv_module_kernel(prev_ref, cur_ref, dw_ref, aux_ref, out_ref, s_ref, *, tm, zero_first, rb):
    i = pl.program_id(1)
    prev = prev_ref[...]
    if zero_first:
        prev = jnp.where(i == 0, 0.0, prev)
    s_ref[0:32, :] = prev
    s_ref[32:32 + tm, :] = cur_ref[...]
    aux = aux_ref[...]
    for r0 in range(0, tm, rb):
        acc = jnp.zeros((rb, CV_WIDTH), F32) + aux[0:1]
        for d in range(CV_K):
            acc = acc + dw_ref[CV_K - 1 - d:CV_K - d, :] * s_ref[32 - d + r0:32 - d + r0 + rb, :]
        c = acc - jnp.mean(acc, axis=-1, keepdims=True)
        y = c * lax.rsqrt(jnp.mean(c * c, axis=-1, keepdims=True) + EPS) * aux[1:2] + aux[2:3]
        out_ref[r0:r0 + rb, :] = y * _sigmoid(y)


def _conv_module(prev3, cur3, dw, aux, *, tm, zero_first):
    nseq, tseg, _ = cur3.shape
    nb = tseg // tm
    if zero_first:
        prev_spec = pl.BlockSpec((None, 32, CV_WIDTH),
                                 lambda s, i: (s, jnp.maximum(i * (tm // 32) - 1, 0), 0))
    else:
        prev_spec = pl.BlockSpec((None, 32, CV_WIDTH), lambda s, i: (s, 0, 0))
    return pl.pallas_call(
        functools.partial(_conv_module_kernel, tm=tm, zero_first=zero_first, rb=min(tm, 32)),
        grid=(nseq, nb),
        in_specs=[prev_spec, pl.BlockSpec((None, tm, CV_WIDTH), lambda s, i: (s, i, 0)),
                  _const_spec(dw.shape), _const_spec(aux.shape)],
        out_specs=pl.BlockSpec((None, tm, CV_WIDTH), lambda s, i: (s, i, 0)),
        out_shape=jax.ShapeDtypeStruct((nseq, tseg, CV_WIDTH), F32),
        scratch_shapes=[pltpu.VMEM((tm + 32, CV_WIDTH), F32)],
        compiler_params=_params(("arbitrary", "arbitrary")),
        name="conv_module",
    )(prev3, cur3, dw, aux)


def _sb_kernel(pt_ref, q_ref, tail_ref, p1_ref, p2_ref, ucat_ref, pool_ref, out_ref,
               carry_ref, acc_ref, buf_ref, sem, *, tq, pos0, c0, c1, layer):
    s = pl.program_id(0)
    qb = pl.program_id(1)
    npast = c0 + c1 * qb
    lane = lax.broadcasted_iota(jnp.int32, (tq, LANES), 1)
    lo64 = lane < 64
    q = q_ref[...]
    qs = []
    for i in range(4):
        blk = q[:, i * LANES:(i + 1) * LANES]
        qs.append(jnp.concatenate([jnp.where(lo64, blk, 0.0), jnp.where(lo64, 0.0, blk)], axis=0).astype(BF16))
    carry_ref[...] = jnp.zeros(carry_ref.shape, F32)
    acc_ref[...] = jnp.zeros(acc_ref.shape, F32)
    ucat = ucat_ref[...]

    def process(kv_ref, mask):
        zs, pvs = [], []
        for i in range(4):
            if len(kv_ref.shape) == 2:
                kb = kv_ref[:, i * LANES:(i + 1) * LANES].astype(BF16)
                vb = kv_ref[:, SB_WIDTH + i * LANES:SB_WIDTH + (i + 1) * LANES].astype(BF16)
                zs.append(_dot_nt(qs[i], kb))
                pvs.append(functools.partial(_dot, b=vb))
            else:
                kt = jnp.concatenate([kv_ref[0, 2 * i], kv_ref[0, 2 * i + 1]], axis=0).astype(BF16)
                vt = jnp.concatenate([kv_ref[1, 2 * i], kv_ref[1, 2 * i + 1]], axis=0).astype(BF16)
                zs.append(_dot(qs[i], kt))
                pvs.append(functools.partial(_dot_nt, b=vt))
        z = jnp.concatenate(zs, axis=0) * (HEAD_DIM ** -0.5)
        log_b = -(jnp.maximum(-z, 0.0) + jnp.log1p(jnp.exp(-jnp.abs(z))))
        l1m = log_b - z
        if mask is not None:
            l1m = jnp.where(mask[None], l1m.reshape(4, 2 * tq, PAGE), 0.0).reshape(8 * tq, PAGE)
        ac = _dot_hilo(l1m, ucat)
        a = jnp.exp(log_b + ac[:, 0:PAGE] + carry_ref[...])
        if mask is not None:
            a = jnp.where(mask[None], a.reshape(4, 2 * tq, PAGE), 0.0).reshape(8 * tq, PAGE)
        ab = a.astype(BF16)
        acc_ref[...] += jnp.concatenate(
            [pvs[i](ab[i * 2 * tq:(i + 1) * 2 * tq]) for i in range(4)], axis=0)
        carry_ref[...] += ac[:, PAGE:2 * PAGE]

    def live():
        return jnp.max(carry_ref[...]) >= SB_EXIT

    qpos = pos0 + qb * tq + lax.broadcasted_iota(jnp.int32, (tq, 1), 0)
    qpos2 = jnp.concatenate([qpos, qpos], axis=0)
    kpos = npast * PAGE + lax.broadcasted_iota(jnp.int32, (2 * tq, PAGE), 1)
    process(tail_ref, kpos < qpos2)

    @pl.when(npast >= 1)
    def _():
        process(p1_ref, None)

    @pl.when((npast >= 2) & live())
    def _():
        process(p2_ref, None)

    def body(state):
        p, _ = state
        src = pool_ref.at[pt_ref[s, p]] if layer is None else pool_ref.at[layer, pt_ref[s, p]]
        cp = pltpu.make_async_copy(src, buf_ref, sem)
        cp.start()
        cp.wait()
        process(buf_ref, None)
        return p - 1, live().astype(jnp.int32)

    lax.while_loop(lambda st: (st[0] >= 0) & (st[1] > 0), body, (npast - 3, live().astype(jnp.int32)))

    for i in range(4):
        r0 = 2 * i * tq
        out_ref[:, i * LANES:(i + 1) * LANES] = jnp.where(lo64, acc_ref[r0:r0 + tq], acc_ref[r0 + tq:r0 + 2 * tq])


def _sb_attn(q, tail3, pool, layer, page_table, ucat, *, nseq, tq, pos0, c0, c1):
    t = q.shape[0]
    nqb = t // (nseq * tq)
    npages = page_table.shape[1]
    page_shape = tuple(pool.shape[1:]) if pool.ndim == 3 else tuple(pool.shape[2:])

    def page_spec(back):
        return _page_spec(pool, layer, None,
                          lambda s, b, pt: pt[s, jnp.clip(c0 + c1 * b - back, 0, npages - 1)])

    grid_spec = pltpu.PrefetchScalarGridSpec(
        num_scalar_prefetch=1,
        grid=(nseq, nqb),
        in_specs=[pl.BlockSpec((tq, SB_WIDTH), lambda s, b, pt: (s * nqb + b, 0)),
                  pl.BlockSpec((None, PAGE, 2 * SB_WIDTH), lambda s, b, pt: (s * nqb + b, 0, 0)),
                  page_spec(1), page_spec(2),
                  pl.BlockSpec(ucat.shape, lambda s, b, pt: (0, 0)),
                  pl.BlockSpec(memory_space=pl.ANY)],
        out_specs=pl.BlockSpec((tq, SB_WIDTH), lambda s, b, pt: (s * nqb + b, 0)),
        scratch_shapes=[pltpu.VMEM((8 * tq, PAGE), F32), pltpu.VMEM((8 * tq, LANES), F32),
                        pltpu.VMEM(page_shape, F32), pltpu.SemaphoreType.DMA(())],
    )
    return pl.pallas_call(
        functools.partial(_sb_kernel, tq=tq, pos0=pos0, c0=c0, c1=c1, layer=None if pool.ndim == 3 else layer),
        grid_spec=grid_spec,
        out_shape=jax.ShapeDtypeStruct((t, SB_WIDTH), F32),
        compiler_params=_params(("arbitrary", "arbitrary")),
        name="stick_breaking",
    )(page_table, q, tail3, pool, pool, ucat, pool)


def _q_perm():
    idx = np.zeros((512,), np.int32)
    for i in range(4):
        for half in range(2):
            for d in range(HEAD_DIM):
                idx[i * 128 + half * 64 + d] = (half * 4 + i) * HEAD_DIM + d
    return idx


def _gate_perm():
    idx = np.zeros((24,), np.int32)
    for c in range(3):
        for i in range(4):
            for half in range(2):
                idx[c * 8 + 2 * i + half] = (half * 4 + i) * 3 + c
    return idx


def _gate_expand_mats():
    g = np.zeros((3, 128, 512), np.float32)
    for c in range(3):
        for i in range(4):
            for half in range(2):
                g[c, c * 8 + 2 * i + half, i * 128 + half * 64:i * 128 + (half + 1) * 64] = 1.0
    return jnp.asarray(g, BF16)


def _block_ones(n, w):
    r = np.arange(n) // w
    return jnp.asarray((r[:, None] == r[None, :]).astype(np.float32), BF16)


def _rope_tables(pos):
    half = HEAD_DIM // 2
    inv = jnp.power(ROPE_THETA, -jnp.arange(half, dtype=F32) / half)
    ang = pos.astype(F32)[:, None] * inv[None, :]
    c, s = jnp.cos(ang), jnp.sin(ang)
    return jnp.concatenate([c, c, c, c], -1), jnp.concatenate([-s, s, -s, s], -1)


def _sel_sum_matrix(nch, nsp, nc):
    n = np.arange(nch)[:, None]
    j = np.arange(nsp)[None, :]
    step = SEL_LEN // CMP_STRIDE
    lo = 1 - CMP_LEN // CMP_STRIDE
    m = (n >= step * j + lo) & (n <= step * j + step - 1) & (n < nc)
    return jnp.asarray(m.astype(np.float32), BF16)


def _chunk_perm():
    p = np.zeros((256, 256), np.float32)
    for l in range(16):
        for c in range(16):
            p[l * 16 + c, c * 16 + l] = 1.0
    return jnp.asarray(p, BF16)


def _compress_weights(cmp_pe, w_cmp):
    wl = jnp.zeros((16, 4, HEAD_DIM, 2, 4, HEAD_DIM), F32)
    for part in range(4):
        for half in range(2):
            wl = wl.at[:, part, :, half, part, :].set(w_cmp[part // 2, half * 16:(half + 1) * 16])
    wl = wl.reshape(16, 256, 512).astype(BF16)
    pe1 = jnp.concatenate([cmp_pe[0, 0:16]] * 2 + [cmp_pe[1, 0:16]] * 2, -1)
    pe2 = jnp.concatenate([cmp_pe[0, 16:32]] * 2 + [cmp_pe[1, 16:32]] * 2, -1)
    h1, l1 = _split_bf16(pe1)
    h2, l2 = _split_bf16(pe2)
    pe_rows = jnp.zeros((16, 16, 256), BF16)
    pe_rows = pe_rows.at[:, 0].set(h1).at[:, 1].set(h2).at[:, 2].set(l1).at[:, 3].set(l2)
    return wl, pe_rows


def _sb_ucat():
    j = np.arange(PAGE)[:, None]
    s = np.arange(PAGE)[None, :]
    u = (j > s).astype(np.float32)
    return jnp.asarray(np.concatenate([u, np.ones((PAGE, PAGE), np.float32)], 1), BF16)


def _pad_rows(a, n):
    return jnp.pad(a, ((0, 0), (0, n - a.shape[1]), (0, 0)))


def _even_layer(xp, xs, i, g0, page_table, cache_nsa, cache_nsa_win, w_in_e, g_qk_nsa, cmp_pe, w_cmp,
                gm_ws, gm_b, gm_ln, w_out_e, past):
    sp = xp.shape[0]
    nseq = page_table.shape[0]
    ts = xs.shape[0] // nseq
    qperm, gperm = _q_perm(), _gate_perm()
    w = w_in_e[i]
    wq = w[:, 0:512][:, qperm]
    wg = jnp.pad(w[:, 1280:1304][:, gperm], ((0, 0), (0, 104)))
    w_all = jnp.concatenate([wq, w[:, 512:1280], wg, w[:, 1304:2328]], -1).astype(BF16)
    gq = jnp.tile(g_qk_nsa[i, 0], 8)[None]
    gk = jnp.stack([jnp.tile(g_qk_nsa[i, r], 2) for r in (1, 2, 3)])
    ones_bd = _block_ones(512, HEAD_DIM)
    gexp = _gate_expand_mats()
    w_out = jnp.concatenate([w_out_e[i][0:512][qperm], w_out_e[i][512:1024]], 0).astype(BF16)
    wl, pe_rows = _compress_weights(cmp_pe[i], w_cmp[i])
    perm = _chunk_perm()

    def inproj(x, pos, tm, rchunk):
        cos_t, sin_t = _rope_tables(pos)
        ws_t = jnp.tril(gm_ws[i])[:, :rchunk, :rchunk]
        eye = jnp.eye(tm // rchunk, dtype=F32)
        wsg = jnp.einsum("ab,gts->gatbs", eye, ws_t).reshape(GM_GROUPS, tm, tm).astype(BF16)
        sbt = jnp.tile(jnp.repeat(gm_b[i][:, :rchunk].T, GM_WIDTH // GM_GROUPS, axis=1), (tm // rchunk, 1))
        return _inproj_even(x, g0, w_all, cos_t, sin_t, gq, gk, gm_ln[i], ones_bd, wsg, sbt, tm)

    tqp = 128
    qp, qr, rows, win, gates, v, ogm = inproj(xp, jnp.arange(sp), 256, CHUNK)
    npp = sp // PAGE
    pt_p = jnp.arange(npp, dtype=jnp.int32)[None]
    pool_p = rows.reshape(npp, PAGE, 512)
    cmp_p = _compress(pool_p, None, pt_p, perm, wl, pe_rows)
    nc = (sp - CMP_LEN) // CMP_STRIDE + 1
    ns = -(-sp // SEL_LEN)
    nsp = -(-ns // LANES) * LANES
    assert nc + 1 <= cmp_p.shape[1]
    msel = _sel_sum_matrix(cmp_p.shape[1], nsp, nc)
    o, sel = _cmp_attn(qp, cmp_p, gates, msel, gexp[0], nseq=1, tq=tqp, pos0=0, nc=nc, ns=ns,
                       k_eff=min(N_SEL, ns))
    nwb = WINDOW // tqp
    kspecs = [pl.BlockSpec((tqp, 256), (lambda s, b, j=j: (jnp.maximum(b - nwb + j, 0), 0)))
              for j in range(nwb + 1)]
    kblocks = [((j - nwb) * tqp, tqp, tqp) for j in range(nwb + 1)]
    o = _win_attn(qr, gates, gexp[2], o, [win] * (nwb + 1), kspecs, kblocks, nseq=1, tq=tqp, pos0=0)
    o = _sel_attn(qr, sel, gates, gexp[1], o, pool_p, pool_p, None, pt_p, nseq=1, tq=tqp, pos0=0, c0=0, c1=1,
                  npg=8)
    xp = _matmul_res([o, ogm], w_out, xp, 512)
    outs_p = (rows, win[sp - min(WINDOW, sp):], v[((sp - 1) // CHUNK) * CHUNK:])

    pos_s = jnp.tile(past + jnp.arange(ts), nseq)
    qp, qr, rows, win, gates, v, ogm = inproj(xs, pos_s, nseq * ts, ts)
    pool_s = jnp.transpose(cache_nsa, (0, 1, 3, 4, 5, 2))
    cmp_s = _compress(pool_s, i, page_table, perm, wl, pe_rows)
    ltot = past + ts
    nc = (ltot - CMP_LEN) // CMP_STRIDE + 1
    ns = -(-ltot // SEL_LEN)
    nsp = -(-ns // LANES) * LANES
    assert nc + 1 <= cmp_s.shape[1] and ts <= PAGE and past % PAGE == 0
    msel = _sel_sum_matrix(cmp_s.shape[1], nsp, nc)
    o, sel = _cmp_attn(qp, cmp_s, gates, msel, gexp[0], nseq=nseq, tq=ts, pos0=past, nc=nc, ns=ns,
                       k_eff=min(N_SEL, ns))
    wb = cache_nsa_win.shape[2]
    win_old = cache_nsa_win[i].reshape(nseq, wb, 256)
    win_new = _pad_rows(win.reshape(nseq, ts, 256), PAGE)
    kspecs = [pl.BlockSpec((None, wb, 256), lambda s, b: (s, 0, 0)),
              pl.BlockSpec((None, PAGE, 256), lambda s, b: (s, 0, 0))]
    kblocks = [(past - wb, 0, wb), (past, 0, PAGE)]
    o = _win_attn(qr, gates, gexp[2], o, [win_old, win_new], kspecs, kblocks, nseq=nseq, tq=ts, pos0=past)
    tail_s = _pad_rows(rows.reshape(nseq, ts, 512), PAGE)
    o = _sel_attn(qr, sel, gates, gexp[1], o, tail_s, pool_s, i, page_table, nseq=nseq, tq=ts, pos0=past,
                  c0=past // PAGE, c1=0, npg=16 if (past // PAGE) % 16 == 0 else 8)
    xs = _matmul_res([o, ogm], w_out, xs, nseq * ts)
    win_s = jnp.concatenate([win_old, win.reshape(nseq, ts, 256)], 1)[:, -wb:]
    outs_s = (rows, win_s, v)
    return xp, xs, outs_p, outs_s


def _odd_layer(xp, xs, i, g0, page_table, cache_sb, state_conv, w_in_o, cv_dw, cv_b, cv_ln, w_out_o, past):
    sp = xp.shape[0]
    nseq = page_table.shape[0]
    ts = xs.shape[0] // nseq
    w_in = w_in_o[i].astype(BF16)
    w_out = w_out_o[i].astype(BF16)
    dw = jnp.pad(cv_dw[i], ((0, 1), (0, 0)))
    aux = jnp.concatenate([cv_b[i][None], cv_ln[i], jnp.zeros((5, CV_WIDTH), F32)], 0)
    ucat = _sb_ucat()

    glu, q, rows = _inproj_odd(xp, g0, w_in, 256)
    o_cv = _conv_module(glu[None], glu[None], dw, aux, tm=256, zero_first=True)[0]
    npp = sp // PAGE
    pt_p = jnp.arange(npp, dtype=jnp.int32)[None]
    pool_p = rows.reshape(npp, PAGE, 2 * SB_WIDTH)
    o_sb = _sb_attn(q, pool_p, pool_p, None, pt_p, ucat, nseq=1, tq=PAGE, pos0=0, c0=0, c1=1)
    xp = _matmul_res([o_cv, o_sb], w_out, xp, 512)
    outs_p = (rows, glu[sp - (CV_K - 1):])

    glu, q, rows = _inproj_odd(xs, g0, w_in, nseq * ts)
    glu3 = glu.reshape(nseq, ts, CV_WIDTH)
    prev = jnp.pad(state_conv[i], ((0, 0), (32 - (CV_K - 1), 0), (0, 0)))
    o_cv = _conv_module(prev, glu3, dw, aux, tm=ts, zero_first=False).reshape(nseq * ts, CV_WIDTH)
    pool_s = jnp.transpose(cache_sb, (0, 1, 3, 4, 5, 2))
    tail_s = _pad_rows(rows.reshape(nseq, ts, 2 * SB_WIDTH), PAGE)
    o_sb = _sb_attn(q, tail_s, pool_s, i, page_table, ucat, nseq=nseq, tq=ts, pos0=past, c0=past // PAGE, c1=0)
    xs = _matmul_res([o_cv, o_sb], w_out, xs, nseq * ts)
    conv_s = jnp.concatenate([state_conv[i], glu3], 1)[:, -(CV_K - 1):]
    outs_s = (rows, conv_s)
    return xp, xs, outs_p, outs_s


def kernel(x_prompt, x_sample, mem_prompt, page_table, cache_nsa, cache_nsa_win, cache_sb, state_conv, state_ffn, cache_mem, g_norm, w_in_e, g_qk_nsa, cmp_pe, w_cmp, gm_ws, gm_b, gm_ln, w_out_e, w_in_o, cv_dw, cv_b, cv_ln, w_out_o, w_xq, w_xkv, g_xqk, w_xo, w_up, ffn_dw, ffn_db, w_down):
    bp, sp, _ = x_prompt.shape
    nseq, ts, _ = x_sample.shape
    assert bp == 1
    depth = g_norm.shape[0]
    past = page_table.shape[1] * PAGE
    mlen = mem_prompt.shape[1]
    xp = x_prompt.reshape(sp, D_MODEL)
    xs = x_sample.reshape(nseq * ts, D_MODEL)
    o = {k: [] for k in ("nsa_rows_p", "nsa_rows_s", "nsa_win_p", "nsa_win_s", "gm_v_p", "gm_v_s",
                         "sb_rows_p", "sb_rows_s", "conv_p", "conv_s", "ffn_p", "ffn_s", "memkv_p")}
    for l in range(depth):
        i = l // 2
        g0 = g_norm[l, 0][None]
        if l % 2 == 0:
            xp, xs, (rows_p, win_p, v_p), (rows_s, win_s, v_s) = _even_layer(
                xp, xs, i, g0, page_table, cache_nsa, cache_nsa_win, w_in_e, g_qk_nsa, cmp_pe, w_cmp,
                gm_ws, gm_b, gm_ln, w_out_e, past)
            o["nsa_rows_p"].append(rows_p.reshape(1, sp, 4, NSA_KV, HEAD_DIM))
            o["nsa_rows_s"].append(rows_s.reshape(nseq, ts, 4, NSA_KV, HEAD_DIM))
            o["nsa_win_p"].append(win_p.reshape(1, -1, 2, NSA_KV, HEAD_DIM))
            o["nsa_win_s"].append(win_s.reshape(nseq, -1, 2, NSA_KV, HEAD_DIM))
            o["gm_v_p"].append(v_p[None])
            o["gm_v_s"].append(v_s.reshape(nseq, ts, GM_WIDTH))
        else:
            xp, xs, (rows_p, conv_p), (rows_s, conv_s) = _odd_layer(
                xp, xs, i, g0, page_table, cache_sb, state_conv, w_in_o, cv_dw, cv_b, cv_ln, w_out_o, past)
            o["sb_rows_p"].append(rows_p.reshape(1, sp, 2, SB_HEADS, HEAD_DIM))
            o["sb_rows_s"].append(rows_s.reshape(nseq, ts, 2, SB_HEADS, HEAD_DIM))
            o["conv_p"].append(conv_p[None])
            o["conv_s"].append(conv_s)

        w_xq_b = w_xq[l].astype(BF16)
        w_xo_b = w_xo[l].astype(BF16)
        g_xk = g_xqk[l, 1][None]
        g_xq = g_xqk[l, 0][None]
        mkv = _rms_matmul(mem_prompt.reshape(mlen, D_MODEL), g_norm[l, 3][None], w_xkv[l].astype(BF16),
                          mlen, 2 * D_MODEL, D_MODEL, g_xk)
        o["memkv_p"].append(mkv.reshape(1, mlen, 2, MEM_HEADS, MEM_HD))
        g1 = g_norm[l, 1][None]
        xp = _cross_block(xp, g1, w_xq_b, g_xq, mkv, w_xo_b, 512)
        qx = _rms_matmul(xs, g1, w_xq_b, nseq * ts, D_MODEL, D_MODEL, g_xq)
        xs = _matmul_res([_cross_attn(qx, cache_mem, l, nseq=nseq, tq=ts)], w_xo_b, xs, nseq * ts)

        g2 = g_norm[l, 2][None]
        w_up_b = w_up[l].astype(BF16)
        w_down_b = w_down[l].astype(BF16)
        dwb = jnp.concatenate([ffn_dw[l], ffn_db[l][None], jnp.zeros((4, 2 * D_FF), F32)], 0)
        xp, sta, stg = _convffn(xp, g2, w_up_b, dwb, w_down_b, 512, D_FF // 2)
        o["ffn_p"].append(jnp.concatenate([sta[-1, 8 - (FFN_K - 1):], stg[-1, 8 - (FFN_K - 1):]], -1)[None])
        up = _rms_matmul(xs, g2, w_up_b, nseq * ts, D_FF // 2)
        full = jnp.concatenate([state_ffn[l], up.reshape(nseq, ts, 2 * D_FF)], 1)
        taps = [full[:, k:k + ts].reshape(nseq * ts, 2 * D_FF) for k in range(FFN_K)]
        xs = _convgate_down(taps[0], taps[1], taps[2], dwb, w_down_b, xs, D_FF // 2)
        o["ffn_s"].append(full[:, -(FFN_K - 1):])

    return (xp.reshape(1, sp, D_MODEL), xs.reshape(nseq, ts, D_MODEL),
            jnp.stack(o["nsa_rows_p"]), jnp.stack(o["nsa_rows_s"]),
            jnp.stack(o["nsa_win_p"]), jnp.stack(o["nsa_win_s"]),
            jnp.stack(o["gm_v_p"]), jnp.stack(o["gm_v_s"]),
            jnp.stack(o["sb_rows_p"]), jnp.stack(o["sb_rows_s"]),
            jnp.stack(o["conv_p"]), jnp.stack(o["conv_s"]),
            jnp.stack(o["ffn_p"]), jnp.stack(o["ffn_s"]),
            jnp.stack(o["memkv_p"]))
```

```python
import functools

import numpy as np
import jax
import jax.numpy as jnp
from jax import lax
from jax.experimental import pallas as pl
from jax.experimental.pallas import tpu as pltpu

F32 = jnp.float32
BF16 = jnp.bfloat16

D_MODEL = 1024
HEAD_DIM = 64
NSA_HEADS = 8
NSA_KV = 2
CMP_LEN = 32
CMP_STRIDE = 16
SEL_LEN = 64
N_SEL = 16
WINDOW = 512
FORCE_SCORE = 1.0e4
GM_WIDTH = 512
GM_GROUPS = 8
CHUNK = 128
CV_WIDTH = 512
CV_K = 31
SB_HEADS = 8
SB_WIDTH = 512
MEM_HEADS = 4
MEM_HD = 256
D_FF = 2816
FFN_K = 3
PAGE = 128
ROPE_THETA = 10000.0
EPS = 1e-6
TINY = 1e-30
IN_E_PAD = 2432
NEG = -1e30
LOG2E = 1.4426950408889634
SB_EXIT = -120.0

LANES = 128
V7X_VMEM_LIMIT = 56 * 1024 * 1024


def _params(sem):
    return pltpu.CompilerParams(dimension_semantics=sem, vmem_limit_bytes=V7X_VMEM_LIMIT)


def _const_spec(shape):
    nd = len(shape)
    return pl.BlockSpec(shape, lambda *_: (0,) * nd)


def _dot(a, b):
    return jnp.dot(a, b, preferred_element_type=F32)


def _dot_nt(a, b):
    return lax.dot_general(a, b, (((1,), (1,)), ((), ())), preferred_element_type=F32)


def _split_bf16(x):
    hi = x.astype(BF16)
    lo = (x - hi.astype(F32)).astype(BF16)
    return hi, lo


def _dot_hilo(x, w):
    hi, lo = _split_bf16(x)
    return _dot(hi, w) + _dot(lo, w)


def _rms_rows(x, g):
    return x * lax.rsqrt(jnp.mean(x * x, axis=-1, keepdims=True) + EPS) * g


def _seg_rms(x, ones_bd, g, width):
    ss = _dot((x * x).astype(BF16), ones_bd)
    return x * lax.rsqrt(ss * (1.0 / width) + EPS) * g


def _sigmoid(x):
    return 1.0 / (1.0 + jnp.exp(-x))


def _gelu_tanh(x):
    return 0.5 * x * (1.0 + jnp.tanh(0.7978845608028654 * (x + 0.044715 * (x * x * x))))


def _rope_blk(x, cos_t, sin_t, lo32):
    sw = jnp.where(lo32, pltpu.roll(x, 96, 1), pltpu.roll(x, 32, 1))
    return x * cos_t + sw * sin_t


def _inproj_even_kernel(x_ref, g_ref, w_ref, cos_ref, sin_ref, gq_ref, gk_ref, ln_ref, ones_ref,
                        wsg_ref, sbt_ref,
                        qp_ref, qr_ref, rows_ref, win_ref, gates_ref, v_ref, ogm_ref):
    x = x_ref[...]
    tm = x.shape[0]
    h = _rms_rows(x, g_ref[...]).astype(BF16)
    z = _dot(h, w_ref[...])
    lane = lax.broadcasted_iota(jnp.int32, (tm, LANES), 1)
    lo32 = (lane & 32) == 0
    lo64 = lane < 64
    cos_t = cos_ref[...]
    sin_t = sin_ref[...]
    ones128 = ones_ref[0:LANES, 0:LANES]
    gk = gk_ref[...]

    q = _seg_rms(z[:, 0:512], ones_ref[...], gq_ref[...], HEAD_DIM)
    qp_ref[...] = q
    for i in range(4):
        sl = slice(i * LANES, (i + 1) * LANES)
        qr_ref[:, sl] = _rope_blk(q[:, sl], cos_t, sin_t, lo32)

    rows_ref[:, 0:128] = _seg_rms(z[:, 512:640], ones128, gk[0:1], HEAD_DIM)
    rows_ref[:, 128:256] = z[:, 640:768]
    ks_k = _seg_rms(z[:, 768:896], ones128, gk[1:2], HEAD_DIM)
    rows_ref[:, 256:384] = _rope_blk(ks_k, cos_t, sin_t, lo32)
    rows_ref[:, 384:512] = z[:, 896:1024]
    kw_k = _seg_rms(z[:, 1024:1152], ones128, gk[2:3], HEAD_DIM)
    win_ref[:, 0:128] = _rope_blk(kw_k, cos_t, sin_t, lo32)
    win_ref[:, 128:256] = z[:, 1152:1280]

    gates_ref[...] = _sigmoid(z[:, 1280:1408])

    u = _gelu_tanh(z[:, 1408:1920])
    vv = _gelu_tanh(z[:, 1920:2432])
    ln = ln_ref[...]
    vc = vv - jnp.mean(vv, axis=-1, keepdims=True)
    v = vc * lax.rsqrt(jnp.mean(vc * vc, axis=-1, keepdims=True) + EPS) * ln[0:1] + ln[1:2]
    v_ref[...] = v
    vb = v.astype(BF16)
    for i in range(4):
        sl = slice(i * LANES, (i + 1) * LANES)
        m_lo = _dot(wsg_ref[2 * i], vb[:, sl])
        m_hi = _dot(wsg_ref[2 * i + 1], vb[:, sl])
        mixed = jnp.where(lo64, m_lo, m_hi) + sbt_ref[:, sl]
        ogm_ref[:, sl] = u[:, sl] * mixed


def _inproj_even(x, g, w, cos_t, sin_t, gq, gk, ln, ones_bd, wsg, sbt, tm):
    t = x.shape[0]
    row = lambda n: pl.BlockSpec((tm, n), lambda i: (i, 0))
    outs = [(512, F32), (512, F32), (512, F32), (256, F32), (128, F32), (512, F32), (512, F32)]
    return pl.pallas_call(
        _inproj_even_kernel,
        grid=(t // tm,),
        in_specs=[row(D_MODEL), _const_spec((1, D_MODEL)), _const_spec(w.shape), row(128), row(128),
                  _const_spec(gq.shape), _const_spec(gk.shape), _const_spec(ln.shape),
                  _const_spec(ones_bd.shape), _const_spec(wsg.shape), _const_spec(sbt.shape)],
        out_specs=[row(n) for n, _ in outs],
        out_shape=[jax.ShapeDtypeStruct((t, n), d) for n, d in outs],
        compiler_params=_params(("arbitrary",)),
        name="inproj_even",
    )(x, g, w, cos_t, sin_t, gq, gk, ln, ones_bd, wsg, sbt)


def _compress_kernel(pt_ref, *refs, npg, nsteps, nch):
    page_refs = refs[:npg]
    perm_ref, wl_ref, pe_ref, out_ref, xl_ref, a2_ref = refs[npg:]
    k = pl.program_id(1)

    @pl.when(k == 0)
    def _():
        for l in range(16):
            xl_ref[l, nch:nch + 16, :] = pe_ref[l]

    perm = perm_ref[...]
    for pp in range(npg // 2):
        ra, rb = page_refs[2 * pp], page_refs[2 * pp + 1]
        if len(ra.shape) == 2:
            pg = jnp.concatenate([ra[...], rb[...]], axis=0).astype(BF16)
            xp = _dot(perm, pg)
        else:
            pg = jnp.concatenate([ra[...].reshape(256, PAGE), rb[...].reshape(256, PAGE)], axis=1).astype(BF16)
            xp = _dot_nt(perm, pg)
        xp = xp.astype(BF16)
        row0 = pl.multiple_of((k * (npg // 2) + pp) * 16, 16)
        for l in range(16):
            xl_ref[l, pl.ds(row0, 16), :] = xp[l * 16:(l + 1) * 16]

    @pl.when(k == nsteps - 1)
    def _():
        acc = _dot(xl_ref[0], wl_ref[0])
        for l in range(1, 16):
            acc = acc + _dot(xl_ref[l], wl_ref[l])
        bias = (acc[nch:nch + 1, 0:256] + acc[nch + 2:nch + 3, 0:256]
                + acc[nch + 1:nch + 2, 256:512] + acc[nch + 3:nch + 4, 256:512])
        a2_ref[0:nch, :] = acc[0:nch, 256:512]
        a2_ref[nch:nch + 8, :] = jnp.zeros((8, 256), F32)
        out_ref[...] = acc[0:nch, 0:256] + a2_ref[1:nch + 1, :] + bias


def _page_spec(pool, layer, half, phys_fn):
    h = 0 if half is None else half
    if pool.ndim == 3:
        w = pool.shape[2] if half is None else pool.shape[2] // 2
        return pl.BlockSpec((None, PAGE, w), lambda *a: (phys_fn(*a), 0, h))
    ty = pool.shape[2] if half is None else pool.shape[2] // 2
    return pl.BlockSpec((None, None, ty) + tuple(pool.shape[3:]), lambda *a: (layer, phys_fn(*a), h, 0, 0, 0))


def _compress(pool, layer, page_table, perm, wl, pe_rows):
    nseq, npages = page_table.shape
    npg = next(n for n in (32, 16, 8) if npages % n == 0)
    nsteps = npages // npg
    nch = npages * (PAGE // CMP_STRIDE)

    def page_spec(p):
        return _page_spec(pool, layer, 0, lambda s, k, pt: pt[s, k * npg + p])

    grid_spec = pltpu.PrefetchScalarGridSpec(
        num_scalar_prefetch=1,
        grid=(nseq, nsteps),
        in_specs=[page_spec(p) for p in range(npg)] + [
            pl.BlockSpec(perm.shape, lambda s, k, pt: (0, 0)),
            pl.BlockSpec(wl.shape, lambda s, k, pt: (0, 0, 0)),
            pl.BlockSpec(pe_rows.shape, lambda s, k, pt: (0, 0, 0))],
        out_specs=pl.BlockSpec((None, nch, 256), lambda s, k, pt: (s, 0, 0)),
        scratch_shapes=[pltpu.VMEM((16, nch + 16, 256), BF16), pltpu.VMEM((nch + 8, 256), F32)],
    )
    return pl.pallas_call(
        functools.partial(_compress_kernel, npg=npg, nsteps=nsteps, nch=nch),
        grid_spec=grid_spec,
        out_shape=jax.ShapeDtypeStruct((nseq, nch, 256), F32),
        compiler_params=_params(("arbitrary", "arbitrary")),
        name="nsa_compress",
    )(page_table, *([pool] * npg), perm, wl, pe_rows)


def _stack_q(q, lo64):
    blks = [q[:, i * LANES:(i + 1) * LANES] for i in range(4)]
    slabs = [jnp.where(lo64, b, 0.0) for b in blks] + [jnp.where(lo64, 0.0, b) for b in blks]
    return jnp.concatenate(slabs, axis=0).astype(BF16)


def _unstack_o(o, tq, lo64):
    return [jnp.where(lo64, o[i * tq:(i + 1) * tq], o[(4 + i) * tq:(5 + i) * tq]) for i in range(4)]


def _with_ones(v, axis):
    first = lax.broadcasted_iota(jnp.int32, v.shape, axis) < HEAD_DIM
    return jnp.where(first, v, 1.0).astype(BF16), jnp.where(first, 1.0, v).astype(BF16)


def _gate_expand(gates, gexp):
    return _dot_hilo(gates, gexp)


def _masked_softmax_rows(s, mask):
    s = jnp.where(mask, s, NEG)
    m = jnp.max(s, axis=-1, keepdims=True)
    p = jnp.where(mask, jnp.exp(s - m), 0.0)
    return p / jnp.maximum(jnp.sum(p, axis=-1, keepdims=True), TINY)


def _masked_softmax_stacked(s_all, mask, tq):
    n = s_all.shape[1]
    return _masked_softmax_rows(s_all.reshape(8, tq, n), mask[None]).reshape(8 * tq, n)


def _cmp_attn_kernel(qp_ref, cmp_ref, gates_ref, msel_ref, gexp_ref, oc_ref, sel_ref, *,
                     tq, pos0, nc, ns, nsp, nch, k_eff):
    qb = pl.program_id(1)
    lane = lax.broadcasted_iota(jnp.int32, (tq, LANES), 1)
    lo64 = lane < 64
    qs = _stack_q(qp_ref[...], lo64)
    cm = cmp_ref[...]
    kc = cm[:, 0:128].astype(BF16)
    vc = cm[:, 128:256].astype(BF16)
    s_all = _dot_nt(qs, kc) * (HEAD_DIM ** -0.5)
    qpos = pos0 + qb * tq + lax.broadcasted_iota(jnp.int32, (tq, 1), 0)
    nidx = lax.broadcasted_iota(jnp.int32, (tq, nch), 1)
    cmask = (nidx * CMP_STRIDE + (CMP_LEN - 1) <= qpos) & (nidx < nc)

    blocks_on_rows = tq % LANES == 0
    msel = msel_ref[...]
    ax = 0 if blocks_on_rows else 1
    if blocks_on_rows:
        qpos_b = pos0 + qb * tq + lax.broadcasted_iota(jnp.int32, (1, tq), 1)
    else:
        qpos_b = qpos
    jidx = lax.broadcasted_iota(jnp.int32, (nsp, tq) if blocks_on_rows else (tq, nsp), ax)
    jf = jidx.astype(F32)
    causal_ok = jidx * SEL_LEN <= qpos_b
    forced = (jidx == jnp.right_shift(qpos_b, 6)) | (jidx == 0)
    p_all = _masked_softmax_stacked(s_all, cmask, tq)
    o = _dot(p_all.astype(BF16), vc)
    for kv in range(NSA_KV):
        pg = jnp.sum(p_all[kv * 4 * tq:(kv + 1) * 4 * tq].reshape(4, tq, nch), axis=0)
        if blocks_on_rows:
            hi, lo = _split_bf16(pg)
            p_slc = _dot_nt(msel, hi) + _dot_nt(msel, lo)
        else:
            p_slc = _dot_hilo(pg, msel)
        score = jnp.where(causal_ok, jnp.where(forced, FORCE_SCORE, p_slc), -1.0)
        score = jnp.where(jidx < ns, score, -3.0e38)
        work = score
        for _ in range(k_eff):
            m = jnp.max(work, axis=ax, keepdims=True)
            idx = jnp.min(jnp.where(work == m, jf, 1.0e9), axis=ax, keepdims=True)
            work = jnp.where(jf == idx, -jnp.inf, work)
        picked = jnp.where((work == -jnp.inf) & (score >= 0.0), 1.0, 0.0)
        sel_ref[:, kv * nsp:(kv + 1) * nsp] = picked.T if blocks_on_rows else picked
    gx = _gate_expand(gates_ref[...], gexp_ref[...])
    for i, ob in enumerate(_unstack_o(o, tq, lo64)):
        sl = slice(i * LANES, (i + 1) * LANES)
        oc_ref[:, sl] = gx[:, sl] * ob


def _cmp_attn(qp, cmp, gates, msel, gexp, *, nseq, tq, pos0, nc, ns, k_eff):
    t = qp.shape[0]
    nqb = t // (nseq * tq)
    nch = cmp.shape[1]
    nsp = msel.shape[1]
    if tq % LANES == 0:
        msel = msel.T
    row = lambda n: pl.BlockSpec((tq, n), lambda s, b: (s * nqb + b, 0))
    return pl.pallas_call(
        functools.partial(_cmp_attn_kernel, tq=tq, pos0=pos0, nc=nc, ns=ns, nsp=nsp, nch=nch, k_eff=k_eff),
        grid=(nseq, nqb),
        in_specs=[row(512), pl.BlockSpec((None, nch, 256), lambda s, b: (s, 0, 0)), row(128),
                  _const_spec(msel.shape), _const_spec(gexp.shape)],
        out_specs=[row(512), row(2 * nsp)],
        out_shape=[jax.ShapeDtypeStruct((t, 512), F32), jax.ShapeDtypeStruct((t, 2 * nsp), F32)],
        compiler_params=_params(("arbitrary", "arbitrary")),
        name="nsa_cmp_select",
    )(qp, cmp, gates, msel, gexp)


def _win_attn_kernel(*refs, tq, pos0, kblocks):
    nkb = len(kblocks)
    qr_ref, gates_ref, gexp_ref, oin_ref = refs[:4]
    k_refs = refs[4:4 + nkb]
    out_ref = refs[4 + nkb]
    qb = pl.program_id(1)
    lane = lax.broadcasted_iota(jnp.int32, (tq, LANES), 1)
    lo64 = lane < 64
    qs = _stack_q(qr_ref[...] * (LOG2E * HEAD_DIM ** -0.5), lo64)
    kv = jnp.concatenate([r[...] for r in k_refs], axis=0)
    nk = kv.shape[0]
    kk = kv[:, 0:128].astype(BF16)
    v0, v1 = _with_ones(kv[:, 128:256], 1)
    qpos = pos0 + qb * tq + lax.broadcasted_iota(jnp.int32, (tq, 1), 0)
    kpos = jnp.concatenate(
        [c0 + c1 * qb + lax.broadcasted_iota(jnp.int32, (tq, n), 1) for (c0, c1, n) in kblocks], axis=1)
    mask = (kpos <= qpos) & (kpos > qpos - WINDOW) & (kpos >= 0)
    s = _dot_nt(qs, kk).reshape(8, tq, nk) + jnp.where(mask, 0.0, NEG)[None]
    m = jnp.max(s, axis=-1, keepdims=True)
    p = jnp.exp2(s - m).reshape(8 * tq, nk).astype(BF16)
    acc = jnp.concatenate([_dot(p[0:4 * tq], v0), _dot(p[4 * tq:8 * tq], v1)], axis=0)
    o = jnp.where(m.reshape(8 * tq, 1) > 0.5 * NEG,
                  acc / jnp.maximum(pltpu.roll(acc, HEAD_DIM, 1), TINY), 0.0)
    gx = _gate_expand(gates_ref[...], gexp_ref[...])
    for i, ob in enumerate(_unstack_o(o, tq, lo64)):
        sl = slice(i * LANES, (i + 1) * LANES)
        out_ref[:, sl] = oin_ref[:, sl] + gx[:, sl] * ob


def _win_attn(qr, gates, gexp, o_in, key_arrays, key_specs, kblocks, *, nseq, tq, pos0):
    t = qr.shape[0]
    nqb = t // (nseq * tq)
    row = lambda n: pl.BlockSpec((tq, n), lambda s, b: (s * nqb + b, 0))
    return pl.pallas_call(
        functools.partial(_win_attn_kernel, tq=tq, pos0=pos0, kblocks=tuple(kblocks)),
        grid=(nseq, nqb),
        in_specs=[row(512), row(128), _const_spec(gexp.shape), row(512)] + list(key_specs),
        out_specs=row(512),
        out_shape=jax.ShapeDtypeStruct((t, 512), F32),
        compiler_params=_params(("arbitrary", "arbitrary")),
        name="nsa_window",
    )(qr, gates, gexp, o_in, *key_arrays)


def _sel_schedule(nqb, npg, c0, c1):
    page_steps = lambda b: (c0 + c1 * b + npg - 1) // npg
    if c1 == 0 or nqb % 2:
        return nqb, page_steps(0 if c1 == 0 else nqb - 1) + 1, lambda p, k: (p, k, page_steps(p))
    total = max(page_steps(b) + page_steps(nqb - 1 - b) + 2 for b in range(nqb // 2))

    def sched(p, k):
        n1 = page_steps(p) + 1
        first = k < n1
        b = jnp.where(first, p, nqb - 1 - p)
        return b, jnp.where(first, k, k - n1), page_steps(b)

    return nqb // 2, total, sched


def _sel_attn_kernel(pt_ref, *refs, tq, pos0, npg, nsp, c0, c1, sched, stream_e):
    qr_ref, sel_ref, gates_ref, gexp_ref, oin_ref, tail_ref, e_ref, etail_ref = refs[:8]
    page_refs = refs[8:8 + npg]
    out_ref, qs_ref, m_ref, acc_ref = refs[8 + npg:]
    qb, k, na = sched(pl.program_id(1), pl.program_id(2))
    npast = c0 + c1 * qb
    lane = lax.broadcasted_iota(jnp.int32, (tq, LANES), 1)
    lo64 = lane < 64

    @pl.when(k == 0)
    def _():
        qs_ref[...] = _stack_q(qr_ref[...] * (LOG2E * HEAD_DIM ** -0.5), lo64)
        m_ref[...] = jnp.full(m_ref.shape, NEG, F32)
        acc_ref[...] = jnp.zeros(acc_ref.shape, F32)

    def update(s_all, pv, biases):
        nk = s_all.shape[1]
        for kvi in range(NSA_KV):
            rows = slice(kvi * 4 * tq, (kvi + 1) * 4 * tq)
            s = (s_all[rows].reshape(4, tq, nk) + biases[kvi][None]).reshape(4 * tq, nk)
            m_old = m_ref[rows]
            m_new = jnp.maximum(m_old, jnp.max(s, axis=-1, keepdims=True))
            p = jnp.exp2(s - m_new)
            acc_ref[rows] = jnp.exp2(m_old - m_new) * acc_ref[rows] + pv(p.astype(BF16), kvi)
            m_ref[rows] = m_new

    def block_biases(j0, nk, extra, tail):
        shift = lax.rem(nsp - j0, nsp)
        out = []
        for kvi in range(NSA_KV):
            sel = sel_ref[:, kvi * nsp:(kvi + 1) * nsp]
            if stream_e:
                b = _dot(sel.astype(BF16), (etail_ref if tail else e_ref)[...])
            else:
                b = _dot(pltpu.roll(sel, shift, 1)[:, 0:LANES].astype(BF16), e_ref[:, 0:nk])
            b = (b - 1.0) * (-NEG)
            out.append(b if extra is None else jnp.where(extra, b, NEG))
        return out

    def rows_update(kv, biases):
        vs = _with_ones(kv[:, 128:256], 1)
        update(_dot_nt(qs_ref[...], kv[:, 0:128].astype(BF16)), lambda p, kvi: _dot(p, vs[kvi]), biases)

    @pl.when(k < na)
    def _():
        nk = npg * PAGE
        if c1 == 0 and c0 % npg == 0:
            extra = None
        else:
            extra = k * npg + jnp.right_shift(lax.broadcasted_iota(jnp.int32, (tq, nk), 1), 7) < npast
        biases = block_biases(k * npg * (PAGE // SEL_LEN), nk, extra, False)
        if len(page_refs[0].shape) == 2:
            rows_update(jnp.concatenate([r[...] for r in page_refs], axis=0), biases)
        else:
            kt = jnp.concatenate([r[0].reshape(LANES, PAGE) for r in page_refs], axis=1).astype(BF16)
            vts = _with_ones(jnp.concatenate([r[1].reshape(LANES, PAGE) for r in page_refs], axis=1), 0)
            update(_dot(qs_ref[...], kt), lambda p, kvi: _dot_nt(p, vts[kvi]), biases)

    @pl.when(k == na)
    def _():
        qpos = pos0 + qb * tq + lax.broadcasted_iota(jnp.int32, (tq, 1), 0)
        kpos = npast * PAGE + lax.broadcasted_iota(jnp.int32, (tq, PAGE), 1)
        rows_update(tail_ref[...], block_biases(npast * (PAGE // SEL_LEN), PAGE, kpos <= qpos, True))
        acc = acc_ref[...]
        o = jnp.where(m_ref[...] > 0.5 * NEG, acc / jnp.maximum(pltpu.roll(acc, HEAD_DIM, 1), TINY), 0.0)
        gx = _gate_expand(gates_ref[...], gexp_ref[...])
        for i, ob in enumerate(_unstack_o(o, tq, lo64)):
            sl = slice(i * LANES, (i + 1) * LANES)
            out_ref[:, sl] = oin_ref[:, sl] + gx[:, sl] * ob


def _sel_attn(qr, sel, gates, gexp, o_in, tail3, pool, layer, page_table, *, nseq, tq, pos0, c0, c1, npg):
    t = qr.shape[0]
    nqb = t // (nseq * tq)
    npages = page_table.shape[1]
    assert npages % npg == 0
    nsp = sel.shape[1] // 2
    nrows, nsteps, sched = _sel_schedule(nqb, npg, c0, c1)
    blk = lambda p, k: sched(p, k)[0]
    row = lambda n: pl.BlockSpec((tq, n), lambda s, p, k, pt: (s * nqb + blk(p, k), 0))
    stream_e = c1 != 0
    if stream_e:
        e_map = (jnp.arange((npages + 1) * PAGE)[None, :] // SEL_LEN == jnp.arange(nsp)[:, None]).astype(BF16)

        def e_index(s, p, k, pt):
            _, kl, na = sched(p, k)
            return (0, jnp.clip(kl, 0, jnp.maximum(na - 1, 0)))

        e_specs = [pl.BlockSpec((nsp, npg * PAGE), e_index),
                   pl.BlockSpec((nsp, PAGE), lambda s, p, k, pt: (0, c0 + c1 * blk(p, k)))]
    else:
        assert npg * (PAGE // SEL_LEN) <= LANES
        e_map = (jnp.arange(npg * PAGE)[None, :] // SEL_LEN == jnp.arange(LANES)[:, None]).astype(BF16)
        e_specs = [pl.BlockSpec(e_map.shape, lambda s, p, k, pt: (0, 0))] * 2

    def page_spec(slot):
        def phys(s, p, k, pt):
            b, kl, _ = sched(p, k)
            last = jnp.maximum(c0 + c1 * b - 1, 0)
            return pt[s, jnp.minimum(jnp.minimum(kl * npg + slot, last), npages - 1)]
        return _page_spec(pool, layer, 1, phys)

    grid_spec = pltpu.PrefetchScalarGridSpec(
        num_scalar_prefetch=1,
        grid=(nseq, nrows, nsteps),
        in_specs=[row(512), row(2 * nsp), row(128),
                  pl.BlockSpec(gexp.shape, lambda s, p, k, pt: (0, 0)), row(512),
                  pl.BlockSpec((None, PAGE, 256), lambda s, p, k, pt: (s * nqb + blk(p, k), 0, 1))]
                 + e_specs + [page_spec(slot) for slot in range(npg)],
        out_specs=row(512),
        scratch_shapes=[pltpu.VMEM((8 * tq, LANES), BF16), pltpu.VMEM((8 * tq, 1), F32),
                        pltpu.VMEM((8 * tq, LANES), F32)],
    )
    return pl.pallas_call(
        functools.partial(_sel_attn_kernel, tq=tq, pos0=pos0, npg=npg, nsp=nsp, c0=c0, c1=c1, sched=sched,
                          stream_e=stream_e),
        grid_spec=grid_spec,
        out_shape=jax.ShapeDtypeStruct((t, 512), F32),
        compiler_params=_params(("arbitrary", "arbitrary", "arbitrary")),
        name="nsa_selected",
    )(page_table, qr, sel, gates, gexp, o_in, tail3, e_map, e_map, *([pool] * npg))


def _rms_matmul_kernel(x_ref, g_ref, w_ref, *rest, n_norm):
    if n_norm:
        gseg_ref, out_ref = rest
    else:
        (out_ref,) = rest
    h = _rms_rows(x_ref[...], g_ref[...]).astype(BF16)
    z = _dot(h, w_ref[...])
    if n_norm:
        for c0 in range(0, n_norm, MEM_HD):
            out_ref[:, c0:c0 + MEM_HD] = _rms_rows(z[:, c0:c0 + MEM_HD], gseg_ref[...])
        if n_norm < z.shape[1]:
            out_ref[:, n_norm:] = z[:, n_norm:]
    else:
        out_ref[...] = z


def _rms_matmul(x, g, w, tm, tn, n_norm=0, gseg=None):
    t, n = x.shape[0], w.shape[1]
    extra, extra_specs = [], []
    if n_norm:
        assert tn == n
        extra = [gseg]
        extra_specs = [_const_spec(gseg.shape)]
    return pl.pallas_call(
        functools.partial(_rms_matmul_kernel, n_norm=n_norm),
        grid=(t // tm, n // tn),
        in_specs=[pl.BlockSpec((tm, D_MODEL), lambda i, j: (i, 0)), _const_spec((1, D_MODEL)),
                  pl.BlockSpec((D_MODEL, tn), lambda i, j: (0, j))] + extra_specs,
        out_specs=pl.BlockSpec((tm, tn), lambda i, j: (i, j)),
        out_shape=jax.ShapeDtypeStruct((t, n), F32),
        compiler_params=_params(("arbitrary", "arbitrary")),
        name="rms_matmul",
    )(x, g, w, *extra)


def _matmul_res_kernel(*refs, na):
    a_refs = refs[:na]
    w_ref, x_ref, out_ref = refs[na:]
    acc = x_ref[...]
    k0 = 0
    for a_ref in a_refs:
        kw = a_ref.shape[1]
        acc = acc + _dot(a_ref[...].astype(BF16), w_ref[k0:k0 + kw, :])
        k0 += kw
    out_ref[...] = acc


def _matmul_res(acts, w, x, tm):
    t = x.shape[0]
    return pl.pallas_call(
        functools.partial(_matmul_res_kernel, na=len(acts)),
        grid=(t // tm,),
        in_specs=[pl.BlockSpec((tm, a.shape[1]), lambda i: (i, 0)) for a in acts]
                 + [_const_spec(w.shape), pl.BlockSpec((tm, D_MODEL), lambda i: (i, 0))],
        out_specs=pl.BlockSpec((tm, D_MODEL), lambda i: (i, 0)),
        out_shape=jax.ShapeDtypeStruct((t, D_MODEL), F32),
        compiler_params=_params(("arbitrary",)),
        name="matmul_residual",
    )(*acts, w, x)


def _cross_attn_kernel(q_ref, mkv_ref, out_ref):
    for h in range(MEM_HEADS):
        sl = slice(h * MEM_HD, (h + 1) * MEM_HD)
        q = q_ref[:, sl].astype(BF16)
        if len(mkv_ref.shape) == 2:
            kk = mkv_ref[:, sl].astype(BF16)
            vv = mkv_ref[:, D_MODEL + h * MEM_HD:D_MODEL + (h + 1) * MEM_HD].astype(BF16)
        else:
            kk = mkv_ref[:, 0, h, :].astype(BF16)
            vv = mkv_ref[:, 1, h, :].astype(BF16)
        s = _dot_nt(q, kk) * (MEM_HD ** -0.5)
        m = jnp.max(s, axis=-1, keepdims=True)
        p = jnp.exp(s - m)
        p = p / jnp.sum(p, axis=-1, keepdims=True)
        out_ref[:, sl] = _dot(p.astype(BF16), vv)


def _cross_attn(q, mkv, layer, *, nseq, tq):
    t = q.shape[0]
    nqb = t // (nseq * tq)
    if mkv.ndim == 3:
        mkv_spec = pl.BlockSpec((None,) + tuple(mkv.shape[1:]), lambda s, b: (s, 0, 0))
    else:
        mkv_spec = pl.BlockSpec((None, None) + tuple(mkv.shape[2:]), lambda s, b: (layer, s, 0, 0, 0, 0))
    return pl.pallas_call(
        _cross_attn_kernel,
        grid=(nseq, nqb),
        in_specs=[pl.BlockSpec((tq, D_MODEL), lambda s, b: (s * nqb + b, 0)), mkv_spec],
        out_specs=pl.BlockSpec((tq, D_MODEL), lambda s, b: (s * nqb + b, 0)),
        out_shape=jax.ShapeDtypeStruct((t, D_MODEL), F32),
        compiler_params=_params(("arbitrary", "arbitrary")),
        name="cross_attn",
    )(q, mkv)


def _cross_block_kernel(x_ref, g_ref, wq_ref, gq_ref, mkv_ref, wo_ref, out_ref):
    x = x_ref[...]
    z = _dot(_rms_rows(x, g_ref[...]).astype(BF16), wq_ref[...])
    acc = x
    for h in range(MEM_HEADS):
        sl = slice(h * MEM_HD, (h + 1) * MEM_HD)
        q = _rms_rows(z[:, sl], gq_ref[...]).astype(BF16)
        kk = mkv_ref[:, sl].astype(BF16)
        vv = mkv_ref[:, D_MODEL + h * MEM_HD:D_MODEL + (h + 1) * MEM_HD].astype(BF16)
        s = _dot_nt(q, kk) * (MEM_HD ** -0.5)
        p = jnp.exp(s - jnp.max(s, axis=-1, keepdims=True))
        p = p / jnp.sum(p, axis=-1, keepdims=True)
        acc = acc + _dot(_dot(p.astype(BF16), vv).astype(BF16), wo_ref[sl, :])
    out_ref[...] = acc


def _cross_block(x, g, wq, gq, mkv, wo, tm):
    t = x.shape[0]
    return pl.pallas_call(
        _cross_block_kernel,
        grid=(t // tm,),
        in_specs=[pl.BlockSpec((tm, D_MODEL), lambda i: (i, 0)), _const_spec((1, D_MODEL)), _const_spec(wq.shape),
                  _const_spec(gq.shape), _const_spec(mkv.shape), _const_spec(wo.shape)],
        out_specs=pl.BlockSpec((tm, D_MODEL), lambda i: (i, 0)),
        out_shape=jax.ShapeDtypeStruct((t, D_MODEL), F32),
        compiler_params=_params(("arbitrary",)),
        name="cross_block",
    )(x, g, wq, gq, mkv, wo)


def _convffn_kernel(x_ref, g_ref, wa_ref, wg_ref, dwa_ref, dwg_ref, wd_ref,
                    out_ref, sta_ref, stg_ref,
                    hn_ref, acc_ref, sa_ref, sg_ref, ca_ref, cg_ref, *, tm, nff):
    i = pl.program_id(0)
    j = pl.program_id(1)

    @pl.when(j == 0)
    def _():
        hn_ref[...] = _rms_rows(x_ref[...], g_ref[...]).astype(BF16)
        acc_ref[...] = jnp.zeros(acc_ref.shape, F32)

    @pl.when(i == 0)
    def _():
        ca_ref[j] = jnp.zeros(ca_ref.shape[1:], F32)
        cg_ref[j] = jnp.zeros(cg_ref.shape[1:], F32)

    hn = hn_ref[...]

    def conv(w_ref, s_ref, c_ref, dw_ref, st_ref):
        u = _dot(hn, w_ref[...])
        s_ref[0:8, :] = c_ref[j]
        s_ref[8:8 + tm, :] = u
        c_ref[j] = u[tm - 8:tm]
        st_ref[...] = u[tm - 8:tm]
        dw = dw_ref[...]
        return dw[0:1] * s_ref[6:6 + tm, :] + dw[1:2] * s_ref[7:7 + tm, :] + dw[2:3] * u + dw[3:4]

    a = conv(wa_ref, sa_ref, ca_ref, dwa_ref, sta_ref)
    g = conv(wg_ref, sg_ref, cg_ref, dwg_ref, stg_ref)
    y = (g * _sigmoid(g)) * a
    acc_ref[...] += _dot(y.astype(BF16), wd_ref[...])

    @pl.when(j == nff - 1)
    def _():
        out_ref[...] = x_ref[...] + acc_ref[...]


def _convffn(x, g, w_up, dwb, w_down, tm, tf):
    t = x.shape[0]
    nff = D_FF // tf
    nt = t // tm
    return pl.pallas_call(
        functools.partial(_convffn_kernel, tm=tm, nff=nff),
        grid=(nt, nff),
        in_specs=[pl.BlockSpec((tm, D_MODEL), lambda i, j: (i, 0)), _const_spec((1, D_MODEL)),
                  pl.BlockSpec((D_MODEL, tf), lambda i, j: (0, j)),
                  pl.BlockSpec((D_MODEL, tf), lambda i, j: (0, nff + j)),
                  pl.BlockSpec((8, tf), lambda i, j: (0, j)),
                  pl.BlockSpec((8, tf), lambda i, j: (0, nff + j)),
                  pl.BlockSpec((tf, D_MODEL), lambda i, j: (j, 0))],
        out_specs=[pl.BlockSpec((tm, D_MODEL), lambda i, j: (i, 0)),
                   pl.BlockSpec((None, 8, tf), lambda i, j: (i, 0, j)),
                   pl.BlockSpec((None, 8, tf), lambda i, j: (i, 0, j))],
        out_shape=[jax.ShapeDtypeStruct((t, D_MODEL), F32),
                   jax.ShapeDtypeStruct((nt, 8, D_FF), F32),
                   jax.ShapeDtypeStruct((nt, 8, D_FF), F32)],
        scratch_shapes=[pltpu.VMEM((tm, D_MODEL), BF16), pltpu.VMEM((tm, D_MODEL), F32),
                        pltpu.VMEM((tm + 8, tf), F32), pltpu.VMEM((tm + 8, tf), F32),
                        pltpu.VMEM((nff, 8, tf), F32), pltpu.VMEM((nff, 8, tf), F32)],
        compiler_params=_params(("arbitrary", "arbitrary")),
        name="convffn",
    )(x, g, w_up, w_up, dwb, dwb, w_down)


def _convgate_down_kernel(a0_ref, a1_ref, a2_ref, g0_ref, g1_ref, g2_ref, dwa_ref, dwg_ref, wd_ref, x_ref,
                          out_ref, acc_ref, *, nff):
    j = pl.program_id(0)

    @pl.when(j == 0)
    def _():
        acc_ref[...] = jnp.zeros(acc_ref.shape, F32)

    dwa = dwa_ref[...]
    dwg = dwg_ref[...]
    a = dwa[0:1] * a0_ref[...] + dwa[1:2] * a1_ref[...] + dwa[2:3] * a2_ref[...] + dwa[3:4]
    g = dwg[0:1] * g0_ref[...] + dwg[1:2] * g1_ref[...] + dwg[2:3] * g2_ref[...] + dwg[3:4]
    y = (g * _sigmoid(g)) * a
    acc_ref[...] += _dot(y.astype(BF16), wd_ref[...])

    @pl.when(j == nff - 1)
    def _():
        out_ref[...] = x_ref[...] + acc_ref[...]


def _convgate_down(f0, f1, f2, dwb, w_down, x, tf):
    t = x.shape[0]
    nff = D_FF // tf
    fa = pl.BlockSpec((t, tf), lambda j: (0, j))
    fg = pl.BlockSpec((t, tf), lambda j: (0, nff + j))
    return pl.pallas_call(
        functools.partial(_convgate_down_kernel, nff=nff),
        grid=(nff,),
        in_specs=[fa, fa, fa, fg, fg, fg,
                  pl.BlockSpec((8, tf), lambda j: (0, j)), pl.BlockSpec((8, tf), lambda j: (0, nff + j)),
                  pl.BlockSpec((tf, D_MODEL), lambda j: (j, 0)), _const_spec((t, D_MODEL))],
        out_specs=_const_spec((t, D_MODEL)),
        out_shape=jax.ShapeDtypeStruct((t, D_MODEL), F32),
        scratch_shapes=[pltpu.VMEM((t, D_MODEL), F32)],
        compiler_params=_params(("arbitrary",)),
        name="convgate_down",
    )(f0, f1, f2, f0, f1, f2, dwb, dwb, w_down, x)


def _inproj_odd_kernel(x_ref, g_ref, w_ref, glu_ref, q_ref, rows_ref):
    h = _rms_rows(x_ref[...], g_ref[...]).astype(BF16)
    z = _dot(h, w_ref[...])
    glu_ref[...] = z[:, 0:512] * _sigmoid(z[:, 512:1024])
    q_ref[...] = z[:, 1024:1536]
    rows_ref[...] = z[:, 1536:2560]


def _inproj_odd(x, g, w, tm):
    t = x.shape[0]
    row = lambda n: pl.BlockSpec((tm, n), lambda i: (i, 0))
    return pl.pallas_call(
        _inproj_odd_kernel,
        grid=(t // tm,),
        in_specs=[row(D_MODEL), _const_spec((1, D_MODEL)), _const_spec(w.shape)],
        out_specs=[row(512), row(512), row(1024)],
        out_shape=[jax.ShapeDtypeStruct((t, 512), F32), jax.ShapeDtypeStruct((t, 512), F32),
                   jax.ShapeDtypeStruct((t, 1024), F32)],
        compiler_params=_params(("arbitrary",)),
        name="inproj_odd",
    )(x, g, w)


def _conv_module_kernel(prev_ref, cur_ref, dw_ref, aux_ref, out_ref, s_ref, *, tm, zero_first, rb):
    i = pl.program_id(1)
    prev = prev_ref[...]
    if zero_first:
        prev = jnp.where(i == 0, 0.0, prev)
    s_ref[0:32, :] = prev
    s_ref[32:32 + tm, :] = cur_ref[...]
    aux = aux_ref[...]
    for r0 in range(0, tm, rb):
        acc = jnp.zeros((rb, CV_WIDTH), F32) + aux[0:1]
        for d in range(CV_K):
            acc = acc + dw_ref[CV_K - 1 - d:CV_K - d, :] * s_ref[32 - d + r0:32 - d + r0 + rb, :]
        c = acc - jnp.mean(acc, axis=-1, keepdims=True)
        y = c * lax.rsqrt(jnp.mean(c * c, axis=-1, keepdims=True) + EPS) * aux[1:2] + aux[2:3]
        out_ref[r0:r0 + rb, :] = y * _sigmoid(y)


def _conv_module(prev3, cur3, dw, aux, *, tm, zero_first):
    nseq, tseg, _ = cur3.shape
    nb = tseg // tm
    if zero_first:
        prev_spec = pl.BlockSpec((None, 32, CV_WIDTH),
                                 lambda s, i: (s, jnp.maximum(i * (tm // 32) - 1, 0), 0))
    else:
        prev_spec = pl.BlockSpec((None, 32, CV_WIDTH), lambda s, i: (s, 0, 0))
    return pl.pallas_call(
        functools.partial(_conv_module_kernel, tm=tm, zero_first=zero_first, rb=min(tm, 32)),
        grid=(nseq, nb),
        in_specs=[prev_spec, pl.BlockSpec((None, tm, CV_WIDTH), lambda s, i: (s, i, 0)),
                  _const_spec(dw.shape), _const_spec(aux.shape)],
        out_specs=pl.BlockSpec((None, tm, CV_WIDTH), lambda s, i: (s, i, 0)),
        out_shape=jax.ShapeDtypeStruct((nseq, tseg, CV_WIDTH), F32),
        scratch_shapes=[pltpu.VMEM((tm + 32, CV_WIDTH), F32)],
        compiler_params=_params(("arbitrary", "arbitrary")),
        name="conv_module",
    )(prev3, cur3, dw, aux)


def _sb_kernel(pt_ref, q_ref, tail_ref, p1_ref, p2_ref, ucat_ref, pool_ref, out_ref,
               carry_ref, acc_ref, buf_ref, sem, *, tq, pos0, c0, c1, layer):
    s = pl.program_id(0)
    qb = pl.program_id(1)
    npast = c0 + c1 * qb
    lane = lax.broadcasted_iota(jnp.int32, (tq, LANES), 1)
    lo64 = lane < 64
    q = q_ref[...]
    qs = []
    for i in range(4):
        blk = q[:, i * LANES:(i + 1) * LANES]
        qs.append(jnp.concatenate([jnp.where(lo64, blk, 0.0), jnp.where(lo64, 0.0, blk)], axis=0).astype(BF16))
    carry_ref[...] = jnp.zeros(carry_ref.shape, F32)
    acc_ref[...] = jnp.zeros(acc_ref.shape, F32)
    ucat = ucat_ref[...]

    def process(kv_ref, mask):
        zs, pvs = [], []
        for i in range(4):
            if len(kv_ref.shape) == 2:
                kb = kv_ref[:, i * LANES:(i + 1) * LANES].astype(BF16)
                vb = kv_ref[:, SB_WIDTH + i * LANES:SB_WIDTH + (i + 1) * LANES].astype(BF16)
                zs.append(_dot_nt(qs[i], kb))
                pvs.append(functools.partial(_dot, b=vb))
            else:
                kt = jnp.concatenate([kv_ref[0, 2 * i], kv_ref[0, 2 * i + 1]], axis=0).astype(BF16)
                vt = jnp.concatenate([kv_ref[1, 2 * i], kv_ref[1, 2 * i + 1]], axis=0).astype(BF16)
                zs.append(_dot(qs[i], kt))
                pvs.append(functools.partial(_dot_nt, b=vt))
        z = jnp.concatenate(zs, axis=0) * (HEAD_DIM ** -0.5)
        log_b = -(jnp.maximum(-z, 0.0) + jnp.log1p(jnp.exp(-jnp.abs(z))))
        l1m = log_b - z
        if mask is not None:
            l1m = jnp.where(mask[None], l1m.reshape(4, 2 * tq, PAGE), 0.0).reshape(8 * tq, PAGE)
        ac = _dot_hilo(l1m, ucat)
        a = jnp.exp(log_b + ac[:, 0:PAGE] + carry_ref[...])
        if mask is not None:
            a = jnp.where(mask[None], a.reshape(4, 2 * tq, PAGE), 0.0).reshape(8 * tq, PAGE)
        ab = a.astype(BF16)
        acc_ref[...] += jnp.concatenate(
            [pvs[i](ab[i * 2 * tq:(i + 1) * 2 * tq]) for i in range(4)], axis=0)
        carry_ref[...] += ac[:, PAGE:2 * PAGE]

    def live():
        return jnp.max(carry_ref[...]) >= SB_EXIT

    qpos = pos0 + qb * tq + lax.broadcasted_iota(jnp.int32, (tq, 1), 0)
    qpos2 = jnp.concatenate([qpos, qpos], axis=0)
    kpos = npast * PAGE + lax.broadcasted_iota(jnp.int32, (2 * tq, PAGE), 1)
    process(tail_ref, kpos < qpos2)

    @pl.when(npast >= 1)
    def _():
        process(p1_ref, None)

    @pl.when((npast >= 2) & live())
    def _():
        process(p2_ref, None)

    def body(state):
        p, _ = state
        src = pool_ref.at[pt_ref[s, p]] if layer is None else pool_ref.at[layer, pt_ref[s, p]]
        cp = pltpu.make_async_copy(src, buf_ref, sem)
        cp.start()
        cp.wait()
        process(buf_ref, None)
        return p - 1, live().astype(jnp.int32)

    lax.while_loop(lambda st: (st[0] >= 0) & (st[1] > 0), body, (npast - 3, live().astype(jnp.int32)))

    for i in range(4):
        r0 = 2 * i * tq
        out_ref[:, i * LANES:(i + 1) * LANES] = jnp.where(lo64, acc_ref[r0:r0 + tq], acc_ref[r0 + tq:r0 + 2 * tq])


def _sb_attn(q, tail3, pool, layer, page_table, ucat, *, nseq, tq, pos0, c0, c1):
    t = q.shape[0]
    nqb = t // (nseq * tq)
    npages = page_table.shape[1]
    page_shape = tuple(pool.shape[1:]) if pool.ndim == 3 else tuple(pool.shape[2:])

    def page_spec(back):
        return _page_spec(pool, layer, None,
                          lambda s, b, pt: pt[s, jnp.clip(c0 + c1 * b - back, 0, npages - 1)])

    grid_spec = pltpu.PrefetchScalarGridSpec(
        num_scalar_prefetch=1,
        grid=(nseq, nqb),
        in_specs=[pl.BlockSpec((tq, SB_WIDTH), lambda s, b, pt: (s * nqb + b, 0)),
                  pl.BlockSpec((None, PAGE, 2 * SB_WIDTH), lambda s, b, pt: (s * nqb + b, 0, 0)),
                  page_spec(1), page_spec(2),
                  pl.BlockSpec(ucat.shape, lambda s, b, pt: (0, 0)),
                  pl.BlockSpec(memory_space=pl.ANY)],
        out_specs=pl.BlockSpec((tq, SB_WIDTH), lambda s, b, pt: (s * nqb + b, 0)),
        scratch_shapes=[pltpu.VMEM((8 * tq, PAGE), F32), pltpu.VMEM((8 * tq, LANES), F32),
                        pltpu.VMEM(page_shape, F32), pltpu.SemaphoreType.DMA(())],
    )
    return pl.pallas_call(
        functools.partial(_sb_kernel, tq=tq, pos0=pos0, c0=c0, c1=c1, layer=None if pool.ndim == 3 else layer),
        grid_spec=grid_spec,
        out_shape=jax.ShapeDtypeStruct((t, SB_WIDTH), F32),
        compiler_params=_params(("arbitrary", "arbitrary")),
        name="stick_breaking",
    )(page_table, q, tail3, pool, pool, ucat, pool)


def _q_perm():
    idx = np.zeros((512,), np.int32)
    for i in range(4):
        for half in range(2):
            for d in range(HEAD_DIM):
                idx[i * 128 + half * 64 + d] = (half * 4 + i) * HEAD_DIM + d
    return idx


def _gate_perm():
    idx = np.zeros((24,), np.int32)
    for c in range(3):
        for i in range(4):
            for half in range(2):
                idx[c * 8 + 2 * i + half] = (half * 4 + i) * 3 + c
    return idx


def _gate_expand_mats():
    g = np.zeros((3, 128, 512), np.float32)
    for c in range(3):
        for i in range(4):
            for half in range(2):
                g[c, c * 8 + 2 * i + half, i * 128 + half * 64:i * 128 + (half + 1) * 64] = 1.0
    return jnp.asarray(g, BF16)


def _block_ones(n, w):
    r = np.arange(n) // w
    return jnp.asarray((r[:, None] == r[None, :]).astype(np.float32), BF16)


def _rope_tables(pos):
    half = HEAD_DIM // 2
    inv = jnp.power(ROPE_THETA, -jnp.arange(half, dtype=F32) / half)
    ang = pos.astype(F32)[:, None] * inv[None, :]
    c, s = jnp.cos(ang), jnp.sin(ang)
    return jnp.concatenate([c, c, c, c], -1), jnp.concatenate([-s, s, -s, s], -1)


def _sel_sum_matrix(nch, nsp, nc):
    n = np.arange(nch)[:, None]
    j = np.arange(nsp)[None, :]
    step = SEL_LEN // CMP_STRIDE
    lo = 1 - CMP_LEN // CMP_STRIDE
    m = (n >= step * j + lo) & (n <= step * j + step - 1) & (n < nc)
    return jnp.asarray(m.astype(np.float32), BF16)


def _chunk_perm():
    p = np.zeros((256, 256), np.float32)
    for l in range(16):
        for c in range(16):
            p[l * 16 + c, c * 16 + l] = 1.0
    return jnp.asarray(p, BF16)


def _compress_weights(cmp_pe, w_cmp):
    wl = jnp.zeros((16, 4, HEAD_DIM, 2, 4, HEAD_DIM), F32)
    for part in range(4):
        for half in range(2):
            wl = wl.at[:, part, :, half, part, :].set(w_cmp[part // 2, half * 16:(half + 1) * 16])
    wl = wl.reshape(16, 256, 512).astype(BF16)
    pe1 = jnp.concatenate([cmp_pe[0, 0:16]] * 2 + [cmp_pe[1, 0:16]] * 2, -1)
    pe2 = jnp.concatenate([cmp_pe[0, 16:32]] * 2 + [cmp_pe[1, 16:32]] * 2, -1)
    h1, l1 = _split_bf16(pe1)
    h2, l2 = _split_bf16(pe2)
    pe_rows = jnp.zeros((16, 16, 256), BF16)
    pe_rows = pe_rows.at[:, 0].set(h1).at[:, 1].set(h2).at[:, 2].set(l1).at[:, 3].set(l2)
    return wl, pe_rows


def _sb_ucat():
    j = np.arange(PAGE)[:, None]
    s = np.arange(PAGE)[None, :]
    u = (j > s).astype(np.float32)
    return jnp.asarray(np.concatenate([u, np.ones((PAGE, PAGE), np.float32)], 1), BF16)


def _pad_rows(a, n):
    return jnp.pad(a, ((0, 0), (0, n - a.shape[1]), (0, 0)))


def _even_layer(xp, xs, i, g0, page_table, cache_nsa, cache_nsa_win, w_in_e, g_qk_nsa, cmp_pe, w_cmp,
                gm_ws, gm_b, gm_ln, w_out_e, past):
    sp = xp.shape[0]
    nseq = page_table.shape[0]
    ts = xs.shape[0] // nseq
    qperm, gperm = _q_perm(), _gate_perm()
    w = w_in_e[i]
    wq = w[:, 0:512][:, qperm]
    wg = jnp.pad(w[:, 1280:1304][:, gperm], ((0, 0), (0, 104)))
    w_all = jnp.concatenate([wq, w[:, 512:1280], wg, w[:, 1304:2328]], -1).astype(BF16)
    gq = jnp.tile(g_qk_nsa[i, 0], 8)[None]
    gk = jnp.stack([jnp.tile(g_qk_nsa[i, r], 2) for r in (1, 2, 3)])
    ones_bd = _block_ones(512, HEAD_DIM)
    gexp = _gate_expand_mats()
    w_out = jnp.concatenate([w_out_e[i][0:512][qperm], w_out_e[i][512:1024]], 0).astype(BF16)
    wl, pe_rows = _compress_weights(cmp_pe[i], w_cmp[i])
    perm = _chunk_perm()

    def inproj(x, pos, tm, rchunk):
        cos_t, sin_t = _rope_tables(pos)
        ws_t = jnp.tril(gm_ws[i])[:, :rchunk, :rchunk]
        eye = jnp.eye(tm // rchunk, dtype=F32)
        wsg = jnp.einsum("ab,gts->gatbs", eye, ws_t).reshape(GM_GROUPS, tm, tm).astype(BF16)
        sbt = jnp.tile(jnp.repeat(gm_b[i][:, :rchunk].T, GM_WIDTH // GM_GROUPS, axis=1), (tm // rchunk, 1))
        return _inproj_even(x, g0, w_all, cos_t, sin_t, gq, gk, gm_ln[i], ones_bd, wsg, sbt, tm)

    tqp = 128
    qp, qr, rows, win, gates, v, ogm = inproj(xp, jnp.arange(sp), 256, CHUNK)
    npp = sp // PAGE
    pt_p = jnp.arange(npp, dtype=jnp.int32)[None]
    pool_p = rows.reshape(npp, PAGE, 512)
    cmp_p = _compress(pool_p, None, pt_p, perm, wl, pe_rows)
    nc = (sp - CMP_LEN) // CMP_STRIDE + 1
    ns = -(-sp // SEL_LEN)
    nsp = -(-ns // LANES) * LANES
    assert nc + 1 <= cmp_p.shape[1]
    msel = _sel_sum_matrix(cmp_p.shape[1], nsp, nc)
    o, sel = _cmp_attn(qp, cmp_p, gates, msel, gexp[0], nseq=1, tq=tqp, pos0=0, nc=nc, ns=ns,
                       k_eff=min(N_SEL, ns))
    nwb = WINDOW // tqp
    kspecs = [pl.BlockSpec((tqp, 256), (lambda s, b, j=j: (jnp.maximum(b - nwb + j, 0), 0)))
              for j in range(nwb + 1)]
    kblocks = [((j - nwb) * tqp, tqp, tqp) for j in range(nwb + 1)]
    o = _win_attn(qr, gates, gexp[2], o, [win] * (nwb + 1), kspecs, kblocks, nseq=1, tq=tqp, pos0=0)
    o = _sel_attn(qr, sel, gates, gexp[1], o, pool_p, pool_p, None, pt_p, nseq=1, tq=tqp, pos0=0, c0=0, c1=1,
                  npg=8)
    xp = _matmul_res([o, ogm], w_out, xp, 512)
    outs_p = (rows, win[sp - min(WINDOW, sp):], v[((sp - 1) // CHUNK) * CHUNK:])

    pos_s = jnp.tile(past + jnp.arange(ts), nseq)
    qp, qr, rows, win, gates, v, ogm = inproj(xs, pos_s, nseq * ts, ts)
    pool_s = jnp.transpose(cache_nsa, (0, 1, 3, 4, 5, 2))
    cmp_s = _compress(pool_s, i, page_table, perm, wl, pe_rows)
    ltot = past + ts
    nc = (ltot - CMP_LEN) // CMP_STRIDE + 1
    ns = -(-ltot // SEL_LEN)
    nsp = -(-ns // LANES) * LANES
    assert nc + 1 <= cmp_s.shape[1] and ts <= PAGE and past % PAGE == 0
    msel = _sel_sum_matrix(cmp_s.shape[1], nsp, nc)
    o, sel = _cmp_attn(qp, cmp_s, gates, msel, gexp[0], nseq=nseq, tq=ts, pos0=past, nc=nc, ns=ns,
                       k_eff=min(N_SEL, ns))
    wb = cache_nsa_win.shape[2]
    win_old = cache_nsa_win[i].reshape(nseq, wb, 256)
    win_new = _pad_rows(win.reshape(nseq, ts, 256), PAGE)
    kspecs = [pl.BlockSpec((None, wb, 256), lambda s, b: (s, 0, 0)),
              pl.BlockSpec((None, PAGE, 256), lambda s, b: (s, 0, 0))]
    kblocks = [(past - wb, 0, wb), (past, 0, PAGE)]
    o = _win_attn(qr, gates, gexp[2], o, [win_old, win_new], kspecs, kblocks, nseq=nseq, tq=ts, pos0=past)
    tail_s = _pad_rows(rows.reshape(nseq, ts, 512), PAGE)
    o = _sel_attn(qr, sel, gates, gexp[1], o, tail_s, pool_s, i, page_table, nseq=nseq, tq=ts, pos0=past,
                  c0=past // PAGE, c1=0, npg=next(n for n in (32, 16, 8) if (past // PAGE) % n == 0))
    xs = _matmul_res([o, ogm], w_out, xs, nseq * ts)
    win_s = jnp.concatenate([win_old, win.reshape(nseq, ts, 256)], 1)[:, -wb:]
    outs_s = (rows, win_s, v)
    return xp, xs, outs_p, outs_s


def _odd_layer(xp, xs, i, g0, page_table, cache_sb, state_conv, w_in_o, cv_dw, cv_b, cv_ln, w_out_o, past):
    sp = xp.shape[0]
    nseq = page_table.shape[0]
    ts = xs.shape[0] // nseq
    w_in = w_in_o[i].astype(BF16)
    w_out = w_out_o[i].astype(BF16)
    dw = jnp.pad(cv_dw[i], ((0, 1), (0, 0)))
    aux = jnp.concatenate([cv_b[i][None], cv_ln[i], jnp.zeros((5, CV_WIDTH), F32)], 0)
    ucat = _sb_ucat()

    glu, q, rows = _inproj_odd(xp, g0, w_in, 256)
    o_cv = _conv_module(glu[None], glu[None], dw, aux, tm=256, zero_first=True)[0]
    npp = sp // PAGE
    pt_p = jnp.arange(npp, dtype=jnp.int32)[None]
    pool_p = rows.reshape(npp, PAGE, 2 * SB_WIDTH)
    o_sb = _sb_attn(q, pool_p, pool_p, None, pt_p, ucat, nseq=1, tq=PAGE, pos0=0, c0=0, c1=1)
    xp = _matmul_res([o_cv, o_sb], w_out, xp, 512)
    outs_p = (rows, glu[sp - (CV_K - 1):])

    glu, q, rows = _inproj_odd(xs, g0, w_in, nseq * ts)
    glu3 = glu.reshape(nseq, ts, CV_WIDTH)
    prev = jnp.pad(state_conv[i], ((0, 0), (32 - (CV_K - 1), 0), (0, 0)))
    o_cv = _conv_module(prev, glu3, dw, aux, tm=ts, zero_first=False).reshape(nseq * ts, CV_WIDTH)
    pool_s = jnp.transpose(cache_sb, (0, 1, 3, 4, 5, 2))
    tail_s = _pad_rows(rows.reshape(nseq, ts, 2 * SB_WIDTH), PAGE)
    o_sb = _sb_attn(q, tail_s, pool_s, i, page_table, ucat, nseq=nseq, tq=ts, pos0=past, c0=past // PAGE, c1=0)
    xs = _matmul_res([o_cv, o_sb], w_out, xs, nseq * ts)
    conv_s = jnp.concatenate([state_conv[i], glu3], 1)[:, -(CV_K - 1):]
    outs_s = (rows, conv_s)
    return xp, xs, outs_p, outs_s


def kernel(x_prompt, x_sample, mem_prompt, page_table, cache_nsa, cache_nsa_win, cache_sb, state_conv, state_ffn, cache_mem, g_norm, w_in_e, g_qk_nsa, cmp_pe, w_cmp, gm_ws, gm_b, gm_ln, w_out_e, w_in_o, cv_dw, cv_b, cv_ln, w_out_o, w_xq, w_xkv, g_xqk, w_xo, w_up, ffn_dw, ffn_db, w_down):
    bp, sp, _ = x_prompt.shape
    nseq, ts, _ = x_sample.shape
    assert bp == 1
    depth = g_norm.shape[0]
    past = page_table.shape[1] * PAGE
    mlen = mem_prompt.shape[1]
    xp = x_prompt.reshape(sp, D_MODEL)
    xs = x_sample.reshape(nseq * ts, D_MODEL)
    o = {k: [] for k in ("nsa_rows_p", "nsa_rows_s", "nsa_win_p", "nsa_win_s", "gm_v_p", "gm_v_s",
                         "sb_rows_p", "sb_rows_s", "conv_p", "conv_s", "ffn_p", "ffn_s", "memkv_p")}
    for l in range(depth):
        i = l // 2
        g0 = g_norm[l, 0][None]
        if l % 2 == 0:
            xp, xs, (rows_p, win_p, v_p), (rows_s, win_s, v_s) = _even_layer(
                xp, xs, i, g0, page_table, cache_nsa, cache_nsa_win, w_in_e, g_qk_nsa, cmp_pe, w_cmp,
                gm_ws, gm_b, gm_ln, w_out_e, past)
            o["nsa_rows_p"].append(rows_p.reshape(1, sp, 4, NSA_KV, HEAD_DIM))
            o["nsa_rows_s"].append(rows_s.reshape(nseq, ts, 4, NSA_KV, HEAD_DIM))
            o["nsa_win_p"].append(win_p.reshape(1, -1, 2, NSA_KV, HEAD_DIM))
            o["nsa_win_s"].append(win_s.reshape(nseq, -1, 2, NSA_KV, HEAD_DIM))
            o["gm_v_p"].append(v_p[None])
            o["gm_v_s"].append(v_s.reshape(nseq, ts, GM_WIDTH))
        else:
            xp, xs, (rows_p, conv_p), (rows_s, conv_s) = _odd_layer(
                xp, xs, i, g0, page_table, cache_sb, state_conv, w_in_o, cv_dw, cv_b, cv_ln, w_out_o, past)
            o["sb_rows_p"].append(rows_p.reshape(1, sp, 2, SB_HEADS, HEAD_DIM))
            o["sb_rows_s"].append(rows_s.reshape(nseq, ts, 2, SB_HEADS, HEAD_DIM))
            o["conv_p"].append(conv_p[None])
            o["conv_s"].append(conv_s)

        w_xq_b = w_xq[l].astype(BF16)
        w_xo_b = w_xo[l].astype(BF16)
        g_xk = g_xqk[l, 1][None]
        g_xq = g_xqk[l, 0][None]
        mkv = _rms_matmul(mem_prompt.reshape(mlen, D_MODEL), g_norm[l, 3][None], w_xkv[l].astype(BF16),
                          mlen, 2 * D_MODEL, D_MODEL, g_xk)
        o["memkv_p"].append(mkv.reshape(1, mlen, 2, MEM_HEADS, MEM_HD))
        g1 = g_norm[l, 1][None]
        xp = _cross_block(xp, g1, w_xq_b, g_xq, mkv, w_xo_b, 512)
        qx = _rms_matmul(xs, g1, w_xq_b, nseq * ts, D_MODEL, D_MODEL, g_xq)
        xs = _matmul_res([_cross_attn(qx, cache_mem, l, nseq=nseq, tq=ts)], w_xo_b, xs, nseq * ts)

        g2 = g_norm[l, 2][None]
        w_up_b = w_up[l].astype(BF16)
        w_down_b = w_down[l].astype(BF16)
        dwb = jnp.concatenate([ffn_dw[l], ffn_db[l][None], jnp.zeros((4, 2 * D_FF), F32)], 0)
        xp, sta, stg = _convffn(xp, g2, w_up_b, dwb, w_down_b, 512, D_FF // 2)
        o["ffn_p"].append(jnp.concatenate([sta[-1, 8 - (FFN_K - 1):], stg[-1, 8 - (FFN_K - 1):]], -1)[None])
        up = _rms_matmul(xs, g2, w_up_b, nseq * ts, D_FF // 2)
        full = jnp.concatenate([state_ffn[l], up.reshape(nseq, ts, 2 * D_FF)], 1)
        taps = [full[:, k:k + ts].reshape(nseq * ts, 2 * D_FF) for k in range(FFN_K)]
        xs = _convgate_down(taps[0], taps[1], taps[2], dwb, w_down_b, xs, D_FF // 2)
        o["ffn_s"].append(full[:, -(FFN_K - 1):])

    return (xp.reshape(1, sp, D_MODEL), xs.reshape(nseq, ts, D_MODEL),
            jnp.stack(o["nsa_rows_p"]), jnp.stack(o["nsa_rows_s"]),
            jnp.stack(o["nsa_win_p"]), jnp.stack(o["nsa_win_s"]),
            jnp.stack(o["gm_v_p"]), jnp.stack(o["gm_v_s"]),
            jnp.stack(o["sb_rows_p"]), jnp.stack(o["sb_rows_s"]),
            jnp.stack(o["conv_p"]), jnp.stack(o["conv_s"]),
            jnp.stack(o["ffn_p"]), jnp.stack(o["ffn_s"]),
            jnp.stack(o["memkv_p"]))
```

```python
import functools

import numpy as np
import jax
import jax.numpy as jnp
from jax import lax
from jax.experimental import pallas as pl
from jax.experimental.pallas import tpu as pltpu

F32 = jnp.float32
BF16 = jnp.bfloat16

D_MODEL = 1024
HEAD_DIM = 64
NSA_HEADS = 8
NSA_KV = 2
CMP_LEN = 32
CMP_STRIDE = 16
SEL_LEN = 64
N_SEL = 16
WINDOW = 512
FORCE_SCORE = 1.0e4
GM_WIDTH = 512
GM_GROUPS = 8
CHUNK = 128
CV_WIDTH = 512
CV_K = 31
SB_HEADS = 8
SB_WIDTH = 512
MEM_HEADS = 4
MEM_HD = 256
D_FF = 2816
FFN_K = 3
PAGE = 128
ROPE_THETA = 10000.0
EPS = 1e-6
TINY = 1e-30
IN_E_PAD = 2432
NEG = -1e30
LOG2E = 1.4426950408889634
SB_EXIT = -120.0

LANES = 128
V7X_VMEM_LIMIT = 56 * 1024 * 1024


def _params(sem):
    return pltpu.CompilerParams(dimension_semantics=sem, vmem_limit_bytes=V7X_VMEM_LIMIT)


def _const_spec(shape):
    nd = len(shape)
    return pl.BlockSpec(shape, lambda *_: (0,) * nd)


def _dot(a, b):
    return jnp.dot(a, b, preferred_element_type=F32)


def _dot_nt(a, b):
    return lax.dot_general(a, b, (((1,), (1,)), ((), ())), preferred_element_type=F32)


def _split_bf16(x):
    hi = x.astype(BF16)
    lo = (x - hi.astype(F32)).astype(BF16)
    return hi, lo


def _dot_hilo(x, w):
    hi, lo = _split_bf16(x)
    return _dot(hi, w) + _dot(lo, w)


def _rms_rows(x, g):
    return x * lax.rsqrt(jnp.mean(x * x, axis=-1, keepdims=True) + EPS) * g


def _seg_rms(x, ones_bd, g, width):
    ss = _dot((x * x).astype(BF16), ones_bd)
    return x * lax.rsqrt(ss * (1.0 / width) + EPS) * g


def _sigmoid(x):
    return 1.0 / (1.0 + jnp.exp(-x))


def _gelu_tanh(x):
    return 0.5 * x * (1.0 + jnp.tanh(0.7978845608028654 * (x + 0.044715 * (x * x * x))))


def _rope_blk(x, cos_t, sin_t, lo32):
    sw = jnp.where(lo32, pltpu.roll(x, 96, 1), pltpu.roll(x, 32, 1))
    return x * cos_t + sw * sin_t


def _inproj_even_kernel(x_ref, g_ref, w_ref, cos_ref, sin_ref, gq_ref, gk_ref, ln_ref, ones_ref,
                        wsg_ref, sbt_ref,
                        qp_ref, qr_ref, rows_ref, win_ref, gates_ref, v_ref, ogm_ref):
    x = x_ref[...]
    tm = x.shape[0]
    h = _rms_rows(x, g_ref[...]).astype(BF16)
    z = _dot(h, w_ref[...])
    lane = lax.broadcasted_iota(jnp.int32, (tm, LANES), 1)
    lo32 = (lane & 32) == 0
    lo64 = lane < 64
    cos_t = cos_ref[...]
    sin_t = sin_ref[...]
    ones128 = ones_ref[0:LANES, 0:LANES]
    gk = gk_ref[...]

    q = _seg_rms(z[:, 0:512], ones_ref[...], gq_ref[...], HEAD_DIM)
    qp_ref[...] = q
    for i in range(4):
        sl = slice(i * LANES, (i + 1) * LANES)
        qr_ref[:, sl] = _rope_blk(q[:, sl], cos_t, sin_t, lo32)

    rows_ref[:, 0:128] = _seg_rms(z[:, 512:640], ones128, gk[0:1], HEAD_DIM)
    rows_ref[:, 128:256] = z[:, 640:768]
    ks_k = _seg_rms(z[:, 768:896], ones128, gk[1:2], HEAD_DIM)
    rows_ref[:, 256:384] = _rope_blk(ks_k, cos_t, sin_t, lo32)
    rows_ref[:, 384:512] = z[:, 896:1024]
    kw_k = _seg_rms(z[:, 1024:1152], ones128, gk[2:3], HEAD_DIM)
    win_ref[:, 0:128] = _rope_blk(kw_k, cos_t, sin_t, lo32)
    win_ref[:, 128:256] = z[:, 1152:1280]

    gates_ref[...] = _sigmoid(z[:, 1280:1408])

    u = _gelu_tanh(z[:, 1408:1920])
    vv = _gelu_tanh(z[:, 1920:2432])
    ln = ln_ref[...]
    vc = vv - jnp.mean(vv, axis=-1, keepdims=True)
    v = vc * lax.rsqrt(jnp.mean(vc * vc, axis=-1, keepdims=True) + EPS) * ln[0:1] + ln[1:2]
    v_ref[...] = v
    vb = v.astype(BF16)
    for i in range(4):
        sl = slice(i * LANES, (i + 1) * LANES)
        m_lo = _dot(wsg_ref[2 * i], vb[:, sl])
        m_hi = _dot(wsg_ref[2 * i + 1], vb[:, sl])
        mixed = jnp.where(lo64, m_lo, m_hi) + sbt_ref[:, sl]
        ogm_ref[:, sl] = u[:, sl] * mixed


def _inproj_even(x, g, w, cos_t, sin_t, gq, gk, ln, ones_bd, wsg, sbt, tm):
    t = x.shape[0]
    row = lambda n: pl.BlockSpec((tm, n), lambda i: (i, 0))
    outs = [(512, F32), (512, F32), (512, F32), (256, F32), (128, F32), (512, F32), (512, F32)]
    return pl.pallas_call(
        _inproj_even_kernel,
        grid=(t // tm,),
        in_specs=[row(D_MODEL), _const_spec((1, D_MODEL)), _const_spec(w.shape), row(128), row(128),
                  _const_spec(gq.shape), _const_spec(gk.shape), _const_spec(ln.shape),
                  _const_spec(ones_bd.shape), _const_spec(wsg.shape), _const_spec(sbt.shape)],
        out_specs=[row(n) for n, _ in outs],
        out_shape=[jax.ShapeDtypeStruct((t, n), d) for n, d in outs],
        compiler_params=_params(("arbitrary",)),
        name="inproj_even",
    )(x, g, w, cos_t, sin_t, gq, gk, ln, ones_bd, wsg, sbt)


def _compress_kernel(pt_ref, *refs, npg, nsteps, nch):
    page_refs = refs[:npg]
    perm_ref, wl_ref, pe_ref, out_ref, xl_ref, a2_ref = refs[npg:]
    k = pl.program_id(1)

    @pl.when(k == 0)
    def _():
        for l in range(16):
            xl_ref[l, nch:nch + 16, :] = pe_ref[l]

    perm = perm_ref[...]
    for pp in range(npg // 2):
        ra, rb = page_refs[2 * pp], page_refs[2 * pp + 1]
        if len(ra.shape) == 2:
            pg = jnp.concatenate([ra[...], rb[...]], axis=0).astype(BF16)
            xp = _dot(perm, pg)
        else:
            pg = jnp.concatenate([ra[...].reshape(256, PAGE), rb[...].reshape(256, PAGE)], axis=1).astype(BF16)
            xp = _dot_nt(perm, pg)
        xp = xp.astype(BF16)
        row0 = pl.multiple_of((k * (npg // 2) + pp) * 16, 16)
        for l in range(16):
            xl_ref[l, pl.ds(row0, 16), :] = xp[l * 16:(l + 1) * 16]

    @pl.when(k == nsteps - 1)
    def _():
        acc = _dot(xl_ref[0], wl_ref[0])
        for l in range(1, 16):
            acc = acc + _dot(xl_ref[l], wl_ref[l])
        bias = (acc[nch:nch + 1, 0:256] + acc[nch + 2:nch + 3, 0:256]
                + acc[nch + 1:nch + 2, 256:512] + acc[nch + 3:nch + 4, 256:512])
        a2_ref[0:nch, :] = acc[0:nch, 256:512]
        a2_ref[nch:nch + 8, :] = jnp.zeros((8, 256), F32)
        out_ref[...] = acc[0:nch, 0:256] + a2_ref[1:nch + 1, :] + bias


def _page_spec(pool, layer, half, phys_fn):
    h = 0 if half is None else half
    if pool.ndim == 3:
        w = pool.shape[2] if half is None else pool.shape[2] // 2
        return pl.BlockSpec((None, PAGE, w), lambda *a: (phys_fn(*a), 0, h))
    ty = pool.shape[2] if half is None else pool.shape[2] // 2
    return pl.BlockSpec((None, None, ty) + tuple(pool.shape[3:]), lambda *a: (layer, phys_fn(*a), h, 0, 0, 0))


def _compress(pool, layer, page_table, perm, wl, pe_rows):
    nseq, npages = page_table.shape
    npg = next(n for n in (32, 16, 8) if npages % n == 0)
    nsteps = npages // npg
    nch = npages * (PAGE // CMP_STRIDE)

    def page_spec(p):
        return _page_spec(pool, layer, 0, lambda s, k, pt: pt[s, k * npg + p])

    grid_spec = pltpu.PrefetchScalarGridSpec(
        num_scalar_prefetch=1,
        grid=(nseq, nsteps),
        in_specs=[page_spec(p) for p in range(npg)] + [
            pl.BlockSpec(perm.shape, lambda s, k, pt: (0, 0)),
            pl.BlockSpec(wl.shape, lambda s, k, pt: (0, 0, 0)),
            pl.BlockSpec(pe_rows.shape, lambda s, k, pt: (0, 0, 0))],
        out_specs=pl.BlockSpec((None, nch, 256), lambda s, k, pt: (s, 0, 0)),
        scratch_shapes=[pltpu.VMEM((16, nch + 16, 256), BF16), pltpu.VMEM((nch + 8, 256), F32)],
    )
    return pl.pallas_call(
        functools.partial(_compress_kernel, npg=npg, nsteps=nsteps, nch=nch),
        grid_spec=grid_spec,
        out_shape=jax.ShapeDtypeStruct((nseq, nch, 256), F32),
        compiler_params=_params(("arbitrary", "arbitrary")),
        name="nsa_compress",
    )(page_table, *([pool] * npg), perm, wl, pe_rows)


def _stack_q(q, lo64):
    blks = [q[:, i * LANES:(i + 1) * LANES] for i in range(4)]
    slabs = [jnp.where(lo64, b, 0.0) for b in blks] + [jnp.where(lo64, 0.0, b) for b in blks]
    return jnp.concatenate(slabs, axis=0).astype(BF16)


def _unstack_o(o, tq, lo64):
    return [jnp.where(lo64, o[i * tq:(i + 1) * tq], o[(4 + i) * tq:(5 + i) * tq]) for i in range(4)]


def _with_ones(v, axis):
    first = lax.broadcasted_iota(jnp.int32, v.shape, axis) < HEAD_DIM
    return jnp.where(first, v, 1.0).astype(BF16), jnp.where(first, 1.0, v).astype(BF16)


def _gate_expand(gates, gexp):
    return _dot_hilo(gates, gexp)


def _masked_softmax_rows(s, mask):
    s = jnp.where(mask, s, NEG)
    m = jnp.max(s, axis=-1, keepdims=True)
    p = jnp.where(mask, jnp.exp(s - m), 0.0)
    return p / jnp.maximum(jnp.sum(p, axis=-1, keepdims=True), TINY)


def _masked_softmax_stacked(s_all, mask, tq):
    n = s_all.shape[1]
    return _masked_softmax_rows(s_all.reshape(8, tq, n), mask[None]).reshape(8 * tq, n)


def _cmp_attn_kernel(qp_ref, cmp_ref, gates_ref, msel_ref, gexp_ref, oc_ref, sel_ref, *,
                     tq, pos0, nc, ns, nsp, nch, k_eff):
    qb = pl.program_id(1)
    lane = lax.broadcasted_iota(jnp.int32, (tq, LANES), 1)
    lo64 = lane < 64
    qs = _stack_q(qp_ref[...], lo64)
    cm = cmp_ref[...]
    kc = cm[:, 0:128].astype(BF16)
    vc = cm[:, 128:256].astype(BF16)
    s_all = _dot_nt(qs, kc) * (HEAD_DIM ** -0.5)
    qpos = pos0 + qb * tq + lax.broadcasted_iota(jnp.int32, (tq, 1), 0)
    nidx = lax.broadcasted_iota(jnp.int32, (tq, nch), 1)
    cmask = (nidx * CMP_STRIDE + (CMP_LEN - 1) <= qpos) & (nidx < nc)

    blocks_on_rows = tq % LANES == 0
    msel = msel_ref[...]
    ax = 0 if blocks_on_rows else 1
    if blocks_on_rows:
        qpos_b = pos0 + qb * tq + lax.broadcasted_iota(jnp.int32, (1, tq), 1)
    else:
        qpos_b = qpos
    jidx = lax.broadcasted_iota(jnp.int32, (nsp, tq) if blocks_on_rows else (tq, nsp), ax)
    jf = jidx.astype(F32)
    causal_ok = jidx * SEL_LEN <= qpos_b
    forced = (jidx == jnp.right_shift(qpos_b, 6)) | (jidx == 0)
    p_all = _masked_softmax_stacked(s_all, cmask, tq)
    o = _dot(p_all.astype(BF16), vc)
    for kv in range(NSA_KV):
        pg = jnp.sum(p_all[kv * 4 * tq:(kv + 1) * 4 * tq].reshape(4, tq, nch), axis=0)
        if blocks_on_rows:
            hi, lo = _split_bf16(pg)
            p_slc = _dot_nt(msel, hi) + _dot_nt(msel, lo)
        else:
            p_slc = _dot_hilo(pg, msel)
        score = jnp.where(causal_ok, jnp.where(forced, FORCE_SCORE, p_slc), -1.0)
        score = jnp.where(jidx < ns, score, -3.0e38)
        work = score
        for _ in range(k_eff):
            m = jnp.max(work, axis=ax, keepdims=True)
            idx = jnp.min(jnp.where(work == m, jf, 1.0e9), axis=ax, keepdims=True)
            work = jnp.where(jf == idx, -jnp.inf, work)
        picked = jnp.where((work == -jnp.inf) & (score >= 0.0), 1.0, 0.0)
        sel_ref[:, kv * nsp:(kv + 1) * nsp] = picked.T if blocks_on_rows else picked
    gx = _gate_expand(gates_ref[...], gexp_ref[...])
    for i, ob in enumerate(_unstack_o(o, tq, lo64)):
        sl = slice(i * LANES, (i + 1) * LANES)
        oc_ref[:, sl] = gx[:, sl] * ob


def _cmp_attn(qp, cmp, gates, msel, gexp, *, nseq, tq, pos0, nc, ns, k_eff):
    t = qp.shape[0]
    nqb = t // (nseq * tq)
    nch = cmp.shape[1]
    nsp = msel.shape[1]
    if tq % LANES == 0:
        msel = msel.T
    row = lambda n: pl.BlockSpec((tq, n), lambda s, b: (s * nqb + b, 0))
    return pl.pallas_call(
        functools.partial(_cmp_attn_kernel, tq=tq, pos0=pos0, nc=nc, ns=ns, nsp=nsp, nch=nch, k_eff=k_eff),
        grid=(nseq, nqb),
        in_specs=[row(512), pl.BlockSpec((None, nch, 256), lambda s, b: (s, 0, 0)), row(128),
                  _const_spec(msel.shape), _const_spec(gexp.shape)],
        out_specs=[row(512), row(2 * nsp)],
        out_shape=[jax.ShapeDtypeStruct((t, 512), F32), jax.ShapeDtypeStruct((t, 2 * nsp), F32)],
        compiler_params=_params(("arbitrary", "arbitrary")),
        name="nsa_cmp_select",
    )(qp, cmp, gates, msel, gexp)


def _win_attn_kernel(*refs, tq, pos0, kblocks):
    nkb = len(kblocks)
    qr_ref, gates_ref, gexp_ref, oin_ref = refs[:4]
    k_refs = refs[4:4 + nkb]
    out_ref = refs[4 + nkb]
    qb = pl.program_id(1)
    lane = lax.broadcasted_iota(jnp.int32, (tq, LANES), 1)
    lo64 = lane < 64
    qs = _stack_q(qr_ref[...] * (LOG2E * HEAD_DIM ** -0.5), lo64)
    kv = jnp.concatenate([r[...] for r in k_refs], axis=0)
    nk = kv.shape[0]
    kk = kv[:, 0:128].astype(BF16)
    v0, v1 = _with_ones(kv[:, 128:256], 1)
    qpos = pos0 + qb * tq + lax.broadcasted_iota(jnp.int32, (tq, 1), 0)
    kpos = jnp.concatenate(
        [c0 + c1 * qb + lax.broadcasted_iota(jnp.int32, (tq, n), 1) for (c0, c1, n) in kblocks], axis=1)
    mask = (kpos <= qpos) & (kpos > qpos - WINDOW) & (kpos >= 0)
    s = _dot_nt(qs, kk).reshape(8, tq, nk) + jnp.where(mask, 0.0, NEG)[None]
    m = jnp.max(s, axis=-1, keepdims=True)
    p = jnp.exp2(s - m).reshape(8 * tq, nk).astype(BF16)
    acc = jnp.concatenate([_dot(p[0:4 * tq], v0), _dot(p[4 * tq:8 * tq], v1)], axis=0)
    o = jnp.where(m.reshape(8 * tq, 1) > 0.5 * NEG,
                  acc / jnp.maximum(pltpu.roll(acc, HEAD_DIM, 1), TINY), 0.0)
    gx = _gate_expand(gates_ref[...], gexp_ref[...])
    for i, ob in enumerate(_unstack_o(o, tq, lo64)):
        sl = slice(i * LANES, (i + 1) * LANES)
        out_ref[:, sl] = oin_ref[:, sl] + gx[:, sl] * ob


def _win_attn(qr, gates, gexp, o_in, key_arrays, key_specs, kblocks, *, nseq, tq, pos0):
    t = qr.shape[0]
    nqb = t // (nseq * tq)
    row = lambda n: pl.BlockSpec((tq, n), lambda s, b: (s * nqb + b, 0))
    return pl.pallas_call(
        functools.partial(_win_attn_kernel, tq=tq, pos0=pos0, kblocks=tuple(kblocks)),
        grid=(nseq, nqb),
        in_specs=[row(512), row(128), _const_spec(gexp.shape), row(512)] + list(key_specs),
        out_specs=row(512),
        out_shape=jax.ShapeDtypeStruct((t, 512), F32),
        compiler_params=_params(("arbitrary", "arbitrary")),
        name="nsa_window",
    )(qr, gates, gexp, o_in, *key_arrays)


def _sel_schedule(nqb, npg, c0, c1):
    page_steps = lambda b: (c0 + c1 * b + npg - 1) // npg
    if c1 == 0 or nqb % 2:
        return nqb, page_steps(0 if c1 == 0 else nqb - 1) + 1, lambda p, k: (p, k, page_steps(p))
    total = max(page_steps(b) + page_steps(nqb - 1 - b) + 2 for b in range(nqb // 2))

    def sched(p, k):
        n1 = page_steps(p) + 1
        first = k < n1
        b = jnp.where(first, p, nqb - 1 - p)
        return b, jnp.where(first, k, k - n1), page_steps(b)

    return nqb // 2, total, sched


def _sel_attn_kernel(pt_ref, *refs, tq, pos0, npg, nsp, c0, c1, sched, stream_e):
    qr_ref, sel_ref, gates_ref, gexp_ref, oin_ref, tail_ref, e_ref, etail_ref = refs[:8]
    page_refs = refs[8:8 + npg]
    out_ref, qs_ref, m_ref, acc_ref = refs[8 + npg:]
    qb, k, na = sched(pl.program_id(1), pl.program_id(2))
    npast = c0 + c1 * qb
    lane = lax.broadcasted_iota(jnp.int32, (tq, LANES), 1)
    lo64 = lane < 64

    @pl.when(k == 0)
    def _():
        qs_ref[...] = _stack_q(qr_ref[...] * (LOG2E * HEAD_DIM ** -0.5), lo64)
        m_ref[...] = jnp.full(m_ref.shape, NEG, F32)
        acc_ref[...] = jnp.zeros(acc_ref.shape, F32)

    def update(s_all, pv, biases):
        nk = s_all.shape[1]
        for kvi in range(NSA_KV):
            rows = slice(kvi * 4 * tq, (kvi + 1) * 4 * tq)
            s = (s_all[rows].reshape(4, tq, nk) + biases[kvi][None]).reshape(4 * tq, nk)
            m_old = m_ref[rows]
            m_new = jnp.maximum(m_old, jnp.max(s, axis=-1, keepdims=True))
            p = jnp.exp2(s - m_new)
            acc_ref[rows] = jnp.exp2(m_old - m_new) * acc_ref[rows] + pv(p.astype(BF16), kvi)
            m_ref[rows] = m_new

    def block_biases(j0, nk, extra, tail):
        shift = lax.rem(nsp - j0, nsp)
        out = []
        for kvi in range(NSA_KV):
            sel = sel_ref[:, kvi * nsp:(kvi + 1) * nsp]
            if stream_e:
                b = _dot(sel.astype(BF16), (etail_ref if tail else e_ref)[...])
            else:
                b = _dot(pltpu.roll(sel, shift, 1)[:, 0:LANES].astype(BF16), e_ref[:, 0:nk])
            b = (b - 1.0) * (-NEG)
            out.append(b if extra is None else jnp.where(extra, b, NEG))
        return out

    def rows_update(kv, biases):
        vs = _with_ones(kv[:, 128:256], 1)
        update(_dot_nt(qs_ref[...], kv[:, 0:128].astype(BF16)), lambda p, kvi: _dot(p, vs[kvi]), biases)

    @pl.when(k < na)
    def _():
        nk = npg * PAGE
        if c1 == 0 and c0 % npg == 0:
            extra = None
        else:
            extra = k * npg + jnp.right_shift(lax.broadcasted_iota(jnp.int32, (tq, nk), 1), 7) < npast
        biases = block_biases(k * npg * (PAGE // SEL_LEN), nk, extra, False)
        if len(page_refs[0].shape) == 2:
            rows_update(jnp.concatenate([r[...] for r in page_refs], axis=0), biases)
        else:
            kt = jnp.concatenate([r[0].reshape(LANES, PAGE) for r in page_refs], axis=1).astype(BF16)
            vts = _with_ones(jnp.concatenate([r[1].reshape(LANES, PAGE) for r in page_refs], axis=1), 0)
            update(_dot(qs_ref[...], kt), lambda p, kvi: _dot_nt(p, vts[kvi]), biases)

    @pl.when(k == na)
    def _():
        qpos = pos0 + qb * tq + lax.broadcasted_iota(jnp.int32, (tq, 1), 0)
        kpos = npast * PAGE + lax.broadcasted_iota(jnp.int32, (tq, PAGE), 1)
        rows_update(tail_ref[...], block_biases(npast * (PAGE // SEL_LEN), PAGE, kpos <= qpos, True))
        acc = acc_ref[...]
        o = jnp.where(m_ref[...] > 0.5 * NEG, acc / jnp.maximum(pltpu.roll(acc, HEAD_DIM, 1), TINY), 0.0)
        gx = _gate_expand(gates_ref[...], gexp_ref[...])
        for i, ob in enumerate(_unstack_o(o, tq, lo64)):
            sl = slice(i * LANES, (i + 1) * LANES)
            out_ref[:, sl] = oin_ref[:, sl] + gx[:, sl] * ob


def _sel_attn(qr, sel, gates, gexp, o_in, tail3, pool, layer, page_table, *, nseq, tq, pos0, c0, c1, npg):
    t = qr.shape[0]
    nqb = t // (nseq * tq)
    npages = page_table.shape[1]
    assert npages % npg == 0
    nsp = sel.shape[1] // 2
    nrows, nsteps, sched = _sel_schedule(nqb, npg, c0, c1)
    blk = lambda p, k: sched(p, k)[0]
    row = lambda n: pl.BlockSpec((tq, n), lambda s, p, k, pt: (s * nqb + blk(p, k), 0))
    stream_e = c1 != 0
    if stream_e:
        e_map = (jnp.arange((npages + 1) * PAGE)[None, :] // SEL_LEN == jnp.arange(nsp)[:, None]).astype(BF16)

        def e_index(s, p, k, pt):
            _, kl, na = sched(p, k)
            return (0, jnp.clip(kl, 0, jnp.maximum(na - 1, 0)))

        e_specs = [pl.BlockSpec((nsp, npg * PAGE), e_index),
                   pl.BlockSpec((nsp, PAGE), lambda s, p, k, pt: (0, c0 + c1 * blk(p, k)))]
    else:
        assert npg * (PAGE // SEL_LEN) <= LANES
        e_map = (jnp.arange(npg * PAGE)[None, :] // SEL_LEN == jnp.arange(LANES)[:, None]).astype(BF16)
        e_specs = [pl.BlockSpec(e_map.shape, lambda s, p, k, pt: (0, 0))] * 2

    def page_spec(slot):
        def phys(s, p, k, pt):
            b, kl, _ = sched(p, k)
            last = jnp.maximum(c0 + c1 * b - 1, 0)
            return pt[s, jnp.minimum(jnp.minimum(kl * npg + slot, last), npages - 1)]
        return _page_spec(pool, layer, 1, phys)

    grid_spec = pltpu.PrefetchScalarGridSpec(
        num_scalar_prefetch=1,
        grid=(nseq, nrows, nsteps),
        in_specs=[row(512), row(2 * nsp), row(128),
                  pl.BlockSpec(gexp.shape, lambda s, p, k, pt: (0, 0)), row(512),
                  pl.BlockSpec((None, PAGE, 256), lambda s, p, k, pt: (s * nqb + blk(p, k), 0, 1))]
                 + e_specs + [page_spec(slot) for slot in range(npg)],
        out_specs=row(512),
        scratch_shapes=[pltpu.VMEM((8 * tq, LANES), BF16), pltpu.VMEM((8 * tq, 1), F32),
                        pltpu.VMEM((8 * tq, LANES), F32)],
    )
    return pl.pallas_call(
        functools.partial(_sel_attn_kernel, tq=tq, pos0=pos0, npg=npg, nsp=nsp, c0=c0, c1=c1, sched=sched,
                          stream_e=stream_e),
        grid_spec=grid_spec,
        out_shape=jax.ShapeDtypeStruct((t, 512), F32),
        compiler_params=_params(("arbitrary", "arbitrary", "arbitrary")),
        name="nsa_selected",
    )(page_table, qr, sel, gates, gexp, o_in, tail3, e_map, e_map, *([pool] * npg))


def _rms_matmul_kernel(x_ref, g_ref, w_ref, *rest, n_norm):
    if n_norm:
        gseg_ref, out_ref = rest
    else:
        (out_ref,) = rest
    h = _rms_rows(x_ref[...], g_ref[...]).astype(BF16)
    z = _dot(h, w_ref[...])
    if n_norm:
        for c0 in range(0, n_norm, MEM_HD):
            out_ref[:, c0:c0 + MEM_HD] = _rms_rows(z[:, c0:c0 + MEM_HD], gseg_ref[...])
        if n_norm < z.shape[1]:
            out_ref[:, n_norm:] = z[:, n_norm:]
    else:
        out_ref[...] = z


def _rms_matmul(x, g, w, tm, tn, n_norm=0, gseg=None):
    t, n = x.shape[0], w.shape[1]
    extra, extra_specs = [], []
    if n_norm:
        assert tn == n
        extra = [gseg]
        extra_specs = [_const_spec(gseg.shape)]
    return pl.pallas_call(
        functools.partial(_rms_matmul_kernel, n_norm=n_norm),
        grid=(t // tm, n // tn),
        in_specs=[pl.BlockSpec((tm, D_MODEL), lambda i, j: (i, 0)), _const_spec((1, D_MODEL)),
                  pl.BlockSpec((D_MODEL, tn), lambda i, j: (0, j))] + extra_specs,
        out_specs=pl.BlockSpec((tm, tn), lambda i, j: (i, j)),
        out_shape=jax.ShapeDtypeStruct((t, n), F32),
        compiler_params=_params(("arbitrary", "arbitrary")),
        name="rms_matmul",
    )(x, g, w, *extra)


def _matmul_res_kernel(*refs, na):
    a_refs = refs[:na]
    w_ref, x_ref, out_ref = refs[na:]
    acc = x_ref[...]
    k0 = 0
    for a_ref in a_refs:
        kw = a_ref.shape[1]
        acc = acc + _dot(a_ref[...].astype(BF16), w_ref[k0:k0 + kw, :])
        k0 += kw
    out_ref[...] = acc


def _matmul_res(acts, w, x, tm):
    t = x.shape[0]
    return pl.pallas_call(
        functools.partial(_matmul_res_kernel, na=len(acts)),
        grid=(t // tm,),
        in_specs=[pl.BlockSpec((tm, a.shape[1]), lambda i: (i, 0)) for a in acts]
                 + [_const_spec(w.shape), pl.BlockSpec((tm, D_MODEL), lambda i: (i, 0))],
        out_specs=pl.BlockSpec((tm, D_MODEL), lambda i: (i, 0)),
        out_shape=jax.ShapeDtypeStruct((t, D_MODEL), F32),
        compiler_params=_params(("arbitrary",)),
        name="matmul_residual",
    )(*acts, w, x)


def _cross_attn_kernel(q_ref, mkv_ref, out_ref):
    for h in range(MEM_HEADS):
        sl = slice(h * MEM_HD, (h + 1) * MEM_HD)
        q = q_ref[:, sl].astype(BF16)
        if len(mkv_ref.shape) == 2:
            kk = mkv_ref[:, sl].astype(BF16)
            vv = mkv_ref[:, D_MODEL + h * MEM_HD:D_MODEL + (h + 1) * MEM_HD].astype(BF16)
        else:
            kk = mkv_ref[:, 0, h, :].astype(BF16)
            vv = mkv_ref[:, 1, h, :].astype(BF16)
        s = _dot_nt(q, kk) * (MEM_HD ** -0.5)
        m = jnp.max(s, axis=-1, keepdims=True)
        p = jnp.exp(s - m)
        p = p / jnp.sum(p, axis=-1, keepdims=True)
        out_ref[:, sl] = _dot(p.astype(BF16), vv)


def _cross_attn(q, mkv, layer, *, nseq, tq):
    t = q.shape[0]
    nqb = t // (nseq * tq)
    if mkv.ndim == 3:
        mkv_spec = pl.BlockSpec((None,) + tuple(mkv.shape[1:]), lambda s, b: (s, 0, 0))
    else:
        mkv_spec = pl.BlockSpec((None, None) + tuple(mkv.shape[2:]), lambda s, b: (layer, s, 0, 0, 0, 0))
    return pl.pallas_call(
        _cross_attn_kernel,
        grid=(nseq, nqb),
        in_specs=[pl.BlockSpec((tq, D_MODEL), lambda s, b: (s * nqb + b, 0)), mkv_spec],
        out_specs=pl.BlockSpec((tq, D_MODEL), lambda s, b: (s * nqb + b, 0)),
        out_shape=jax.ShapeDtypeStruct((t, D_MODEL), F32),
        compiler_params=_params(("arbitrary", "arbitrary")),
        name="cross_attn",
    )(q, mkv)


def _cross_block_kernel(x_ref, g_ref, wq_ref, gq_ref, mkv_ref, wo_ref, out_ref):
    x = x_ref[...]
    z = _dot(_rms_rows(x, g_ref[...]).astype(BF16), wq_ref[...])
    acc = x
    for h in range(MEM_HEADS):
        sl = slice(h * MEM_HD, (h + 1) * MEM_HD)
        q = _rms_rows(z[:, sl], gq_ref[...]).astype(BF16)
        kk = mkv_ref[:, sl].astype(BF16)
        vv = mkv_ref[:, D_MODEL + h * MEM_HD:D_MODEL + (h + 1) * MEM_HD].astype(BF16)
        s = _dot_nt(q, kk) * (MEM_HD ** -0.5)
        p = jnp.exp(s - jnp.max(s, axis=-1, keepdims=True))
        p = p / jnp.sum(p, axis=-1, keepdims=True)
        acc = acc + _dot(_dot(p.astype(BF16), vv).astype(BF16), wo_ref[sl, :])
    out_ref[...] = acc


def _cross_block(x, g, wq, gq, mkv, wo, tm):
    t = x.shape[0]
    return pl.pallas_call(
        _cross_block_kernel,
        grid=(t // tm,),
        in_specs=[pl.BlockSpec((tm, D_MODEL), lambda i: (i, 0)), _const_spec((1, D_MODEL)), _const_spec(wq.shape),
                  _const_spec(gq.shape), _const_spec(mkv.shape), _const_spec(wo.shape)],
        out_specs=pl.BlockSpec((tm, D_MODEL), lambda i: (i, 0)),
        out_shape=jax.ShapeDtypeStruct((t, D_MODEL), F32),
        compiler_params=_params(("arbitrary",)),
        name="cross_block",
    )(x, g, wq, gq, mkv, wo)


def _convffn_kernel(x_ref, g_ref, wa_ref, wg_ref, dwa_ref, dwg_ref, wd_ref,
                    out_ref, sta_ref, stg_ref,
                    hn_ref, acc_ref, sa_ref, sg_ref, ca_ref, cg_ref, *, tm, nff):
    i = pl.program_id(0)
    j = pl.program_id(1)

    @pl.when(j == 0)
    def _():
        hn_ref[...] = _rms_rows(x_ref[...], g_ref[...]).astype(BF16)
        acc_ref[...] = jnp.zeros(acc_ref.shape, F32)

    @pl.when(i == 0)
    def _():
        ca_ref[j] = jnp.zeros(ca_ref.shape[1:], F32)
        cg_ref[j] = jnp.zeros(cg_ref.shape[1:], F32)

    hn = hn_ref[...]

    def conv(w_ref, s_ref, c_ref, dw_ref, st_ref):
        u = _dot(hn, w_ref[...])
        s_ref[0:8, :] = c_ref[j]
        s_ref[8:8 + tm, :] = u
        c_ref[j] = u[tm - 8:tm]
        st_ref[...] = u[tm - 8:tm]
        dw = dw_ref[...]
        return dw[0:1] * s_ref[6:6 + tm, :] + dw[1:2] * s_ref[7:7 + tm, :] + dw[2:3] * u + dw[3:4]

    a = conv(wa_ref, sa_ref, ca_ref, dwa_ref, sta_ref)
    g = conv(wg_ref, sg_ref, cg_ref, dwg_ref, stg_ref)
    y = (g * _sigmoid(g)) * a
    acc_ref[...] += _dot(y.astype(BF16), wd_ref[...])

    @pl.when(j == nff - 1)
    def _():
        out_ref[...] = x_ref[...] + acc_ref[...]


def _convffn(x, g, w_up, dwb, w_down, tm, tf):
    t = x.shape[0]
    nff = D_FF // tf
    nt = t // tm
    return pl.pallas_call(
        functools.partial(_convffn_kernel, tm=tm, nff=nff),
        grid=(nt, nff),
        in_specs=[pl.BlockSpec((tm, D_MODEL), lambda i, j: (i, 0)), _const_spec((1, D_MODEL)),
                  pl.BlockSpec((D_MODEL, tf), lambda i, j: (0, j)),
                  pl.BlockSpec((D_MODEL, tf), lambda i, j: (0, nff + j)),
                  pl.BlockSpec((8, tf), lambda i, j: (0, j)),
                  pl.BlockSpec((8, tf), lambda i, j: (0, nff + j)),
                  pl.BlockSpec((tf, D_MODEL), lambda i, j: (j, 0))],
        out_specs=[pl.BlockSpec((tm, D_MODEL), lambda i, j: (i, 0)),
                   pl.BlockSpec((None, 8, tf), lambda i, j: (i, 0, j)),
                   pl.BlockSpec((None, 8, tf), lambda i, j: (i, 0, j))],
        out_shape=[jax.ShapeDtypeStruct((t, D_MODEL), F32),
                   jax.ShapeDtypeStruct((nt, 8, D_FF), F32),
                   jax.ShapeDtypeStruct((nt, 8, D_FF), F32)],
        scratch_shapes=[pltpu.VMEM((tm, D_MODEL), BF16), pltpu.VMEM((tm, D_MODEL), F32),
                        pltpu.VMEM((tm + 8, tf), F32), pltpu.VMEM((tm + 8, tf), F32),
                        pltpu.VMEM((nff, 8, tf), F32), pltpu.VMEM((nff, 8, tf), F32)],
        compiler_params=_params(("arbitrary", "arbitrary")),
        name="convffn",
    )(x, g, w_up, w_up, dwb, dwb, w_down)


def _convgate_down_kernel(a0_ref, a1_ref, a2_ref, g0_ref, g1_ref, g2_ref, dwa_ref, dwg_ref, wd_ref, x_ref,
                          out_ref, acc_ref, *, nff):
    j = pl.program_id(0)

    @pl.when(j == 0)
    def _():
        acc_ref[...] = jnp.zeros(acc_ref.shape, F32)

    dwa = dwa_ref[...]
    dwg = dwg_ref[...]
    a = dwa[0:1] * a0_ref[...] + dwa[1:2] * a1_ref[...] + dwa[2:3] * a2_ref[...] + dwa[3:4]
    g = dwg[0:1] * g0_ref[...] + dwg[1:2] * g1_ref[...] + dwg[2:3] * g2_ref[...] + dwg[3:4]
    y = (g * _sigmoid(g)) * a
    acc_ref[...] += _dot(y.astype(BF16), wd_ref[...])

    @pl.when(j == nff - 1)
    def _():
        out_ref[...] = x_ref[...] + acc_ref[...]


def _convgate_down(f0, f1, f2, dwb, w_down, x, tf):
    t = x.shape[0]
    nff = D_FF // tf
    fa = pl.BlockSpec((t, tf), lambda j: (0, j))
    fg = pl.BlockSpec((t, tf), lambda j: (0, nff + j))
    return pl.pallas_call(
        functools.partial(_convgate_down_kernel, nff=nff),
        grid=(nff,),
        in_specs=[fa, fa, fa, fg, fg, fg,
                  pl.BlockSpec((8, tf), lambda j: (0, j)), pl.BlockSpec((8, tf), lambda j: (0, nff + j)),
                  pl.BlockSpec((tf, D_MODEL), lambda j: (j, 0)), _const_spec((t, D_MODEL))],
        out_specs=_const_spec((t, D_MODEL)),
        out_shape=jax.ShapeDtypeStruct((t, D_MODEL), F32),
        scratch_shapes=[pltpu.VMEM((t, D_MODEL), F32)],
        compiler_params=_params(("arbitrary",)),
        name="convgate_down",
    )(f0, f1, f2, f0, f1, f2, dwb, dwb, w_down, x)


def _inproj_odd_kernel(x_ref, g_ref, w_ref, glu_ref, q_ref, rows_ref):
    h = _rms_rows(x_ref[...], g_ref[...]).astype(BF16)
    z = _dot(h, w_ref[...])
    glu_ref[...] = z[:, 0:512] * _sigmoid(z[:, 512:1024])
    q_ref[...] = z[:, 1024:1536]
    rows_ref[...] = z[:, 1536:2560]


def _inproj_odd(x, g, w, tm):
    t = x.shape[0]
    row = lambda n: pl.BlockSpec((tm, n), lambda i: (i, 0))
    return pl.pallas_call(
        _inproj_odd_kernel,
        grid=(t // tm,),
        in_specs=[row(D_MODEL), _const_spec((1, D_MODEL)), _const_spec(w.shape)],
        out_specs=[row(512), row(512), row(1024)],
        out_shape=[jax.ShapeDtypeStruct((t, 512), F32), jax.ShapeDtypeStruct((t, 512), F32),
                   jax.ShapeDtypeStruct((t, 1024), F32)],
        compiler_params=_params(("arbitrary",)),
        name="inproj_odd",
    )(x, g, w)


def _conv_module_kernel(prev_ref, cur_ref, dw_ref, aux_ref, out_ref, s_ref, *, tm, zero_first, rb):
    i = pl.program_id(1)
    prev = prev_ref[...]
    if zero_first:
        prev = jnp.where(i == 0, 0.0, prev)
    s_ref[0:32, :] = prev
    s_ref[32:32 + tm, :] = cur_ref[...]
    aux = aux_ref[...]
    for r0 in range(0, tm, rb):
        acc = jnp.zeros((rb, CV_WIDTH), F32) + aux[0:1]
        for d in range(CV_K):
            acc = acc + dw_ref[CV_K - 1 - d:CV_K - d, :] * s_ref[32 - d + r0:32 - d + r0 + rb, :]
        c = acc - jnp.mean(acc, axis=-1, keepdims=True)
        y = c * lax.rsqrt(jnp.mean(c * c, axis=-1, keepdims=True) + EPS) * aux[1:2] + aux[2:3]
        out_ref[r0:r0 + rb, :] = y * _sigmoid(y)


def _conv_module(prev3, cur3, dw, aux, *, tm, zero_first):
    nseq, tseg, _ = cur3.shape
    nb = tseg // tm
    if zero_first:
        prev_spec = pl.BlockSpec((None, 32, CV_WIDTH),
                                 lambda s, i: (s, jnp.maximum(i * (tm // 32) - 1, 0), 0))
    else:
        prev_spec = pl.BlockSpec((None, 32, CV_WIDTH), lambda s, i: (s, 0, 0))
    return pl.pallas_call(
        functools.partial(_conv_module_kernel, tm=tm, zero_first=zero_first, rb=min(tm, 32)),
        grid=(nseq, nb),
        in_specs=[prev_spec, pl.BlockSpec((None, tm, CV_WIDTH), lambda s, i: (s, i, 0)),
                  _const_spec(dw.shape), _const_spec(aux.shape)],
        out_specs=pl.BlockSpec((None, tm, CV_WIDTH), lambda s, i: (s, i, 0)),
        out_shape=jax.ShapeDtypeStruct((nseq, tseg, CV_WIDTH), F32),
        scratch_shapes=[pltpu.VMEM((tm + 32, CV_WIDTH), F32)],
        compiler_params=_params(("arbitrary", "arbitrary")),
        name="conv_module",
    )(prev3, cur3, dw, aux)


def _sb_kernel(pt_ref, q_ref, tail_ref, p1_ref, p2_ref, ucat_ref, ucat2_ref, pool_ref, out_ref,
               carry_ref, acc_ref, buf_ref, sem, *, tq, pos0, c0, c1, layer):
    s = pl.program_id(0)
    qb = pl.program_id(1)
    npast = c0 + c1 * qb
    lane = lax.broadcasted_iota(jnp.int32, (tq, LANES), 1)
    lo64 = lane < 64
    q = q_ref[...]
    qs = []
    for i in range(4):
        blk = q[:, i * LANES:(i + 1) * LANES]
        qs.append(jnp.concatenate([jnp.where(lo64, blk, 0.0), jnp.where(lo64, 0.0, blk)], axis=0).astype(BF16))
    carry_ref[...] = jnp.zeros(carry_ref.shape, F32)
    acc_ref[...] = jnp.zeros(acc_ref.shape, F32)
    def process(kv_refs, mask):
        nk = PAGE * len(kv_refs)
        ucat = (ucat_ref if len(kv_refs) == 1 else ucat2_ref)[...]
        zs, pvs = [], []
        for i in range(4):
            if len(kv_refs[0].shape) == 2:
                kb = jnp.concatenate([r[:, i * LANES:(i + 1) * LANES] for r in kv_refs], axis=0).astype(BF16)
                vb = jnp.concatenate([r[:, SB_WIDTH + i * LANES:SB_WIDTH + (i + 1) * LANES] for r in kv_refs],
                                     axis=0).astype(BF16)
                zs.append(_dot_nt(qs[i], kb))
                pvs.append(functools.partial(_dot, b=vb))
            else:
                kt = jnp.concatenate([jnp.concatenate([r[0, 2 * i], r[0, 2 * i + 1]], axis=0) for r in kv_refs],
                                     axis=1).astype(BF16)
                vt = jnp.concatenate([jnp.concatenate([r[1, 2 * i], r[1, 2 * i + 1]], axis=0) for r in kv_refs],
                                     axis=1).astype(BF16)
                zs.append(_dot(qs[i], kt))
                pvs.append(functools.partial(_dot_nt, b=vt))
        z = jnp.concatenate(zs, axis=0) * (HEAD_DIM ** -0.5)
        log_b = -(jnp.maximum(-z, 0.0) + jnp.log1p(jnp.exp(-jnp.abs(z))))
        l1m = log_b - z
        if mask is not None:
            l1m = jnp.where(mask[None], l1m.reshape(4, 2 * tq, nk), 0.0).reshape(8 * tq, nk)
        ac = _dot_hilo(l1m, ucat)
        carry = carry_ref[...]
        a = jnp.exp(log_b + ac[:, 0:nk] + jnp.concatenate([carry] * len(kv_refs), axis=1))
        if mask is not None:
            a = jnp.where(mask[None], a.reshape(4, 2 * tq, nk), 0.0).reshape(8 * tq, nk)
        ab = a.astype(BF16)
        acc_ref[...] += jnp.concatenate(
            [pvs[i](ab[i * 2 * tq:(i + 1) * 2 * tq]) for i in range(4)], axis=0)
        carry_ref[...] = carry + ac[:, nk:nk + PAGE]

    def live():
        return jnp.max(carry_ref[...]) >= SB_EXIT

    qpos = pos0 + qb * tq + lax.broadcasted_iota(jnp.int32, (tq, 1), 0)
    qpos2 = jnp.concatenate([qpos, qpos], axis=0)
    kpos = npast * PAGE + lax.broadcasted_iota(jnp.int32, (2 * tq, PAGE), 1)
    process([tail_ref], kpos < qpos2)

    @pl.when(npast == 1)
    def _():
        process([p1_ref], None)

    @pl.when(npast >= 2)
    def _():
        process([p2_ref, p1_ref], None)

    def body(state):
        p, _ = state
        src = pool_ref.at[pt_ref[s, p]] if layer is None else pool_ref.at[layer, pt_ref[s, p]]
        cp = pltpu.make_async_copy(src, buf_ref, sem)
        cp.start()
        cp.wait()
        process([buf_ref], None)
        return p - 1, live().astype(jnp.int32)

    lax.while_loop(lambda st: (st[0] >= 0) & (st[1] > 0), body, (npast - 3, live().astype(jnp.int32)))

    for i in range(4):
        r0 = 2 * i * tq
        out_ref[:, i * LANES:(i + 1) * LANES] = jnp.where(lo64, acc_ref[r0:r0 + tq], acc_ref[r0 + tq:r0 + 2 * tq])


def _sb_attn(q, tail3, pool, layer, page_table, ucats, *, nseq, tq, pos0, c0, c1):
    ucat, ucat2 = ucats
    t = q.shape[0]
    nqb = t // (nseq * tq)
    npages = page_table.shape[1]
    page_shape = tuple(pool.shape[1:]) if pool.ndim == 3 else tuple(pool.shape[2:])

    def page_spec(back):
        return _page_spec(pool, layer, None,
                          lambda s, b, pt: pt[s, jnp.clip(c0 + c1 * b - back, 0, npages - 1)])

    grid_spec = pltpu.PrefetchScalarGridSpec(
        num_scalar_prefetch=1,
        grid=(nseq, nqb),
        in_specs=[pl.BlockSpec((tq, SB_WIDTH), lambda s, b, pt: (s * nqb + b, 0)),
                  pl.BlockSpec((None, PAGE, 2 * SB_WIDTH), lambda s, b, pt: (s * nqb + b, 0, 0)),
                  page_spec(1), page_spec(2),
                  pl.BlockSpec(ucat.shape, lambda s, b, pt: (0, 0)),
                  pl.BlockSpec(ucat2.shape, lambda s, b, pt: (0, 0)),
                  pl.BlockSpec(memory_space=pl.ANY)],
        out_specs=pl.BlockSpec((tq, SB_WIDTH), lambda s, b, pt: (s * nqb + b, 0)),
        scratch_shapes=[pltpu.VMEM((8 * tq, PAGE), F32), pltpu.VMEM((8 * tq, LANES), F32),
                        pltpu.VMEM(page_shape, F32), pltpu.SemaphoreType.DMA(())],
    )
    return pl.pallas_call(
        functools.partial(_sb_kernel, tq=tq, pos0=pos0, c0=c0, c1=c1, layer=None if pool.ndim == 3 else layer),
        grid_spec=grid_spec,
        out_shape=jax.ShapeDtypeStruct((t, SB_WIDTH), F32),
        compiler_params=_params(("arbitrary", "arbitrary")),
        name="stick_breaking",
    )(page_table, q, tail3, pool, pool, ucat, ucat2, pool)


def _q_perm():
    idx = np.zeros((512,), np.int32)
    for i in range(4):
        for half in range(2):
            for d in range(HEAD_DIM):
                idx[i * 128 + half * 64 + d] = (half * 4 + i) * HEAD_DIM + d
    return idx


def _gate_perm():
    idx = np.zeros((24,), np.int32)
    for c in range(3):
        for i in range(4):
            for half in range(2):
                idx[c * 8 + 2 * i + half] = (half * 4 + i) * 3 + c
    return idx


def _gate_expand_mats():
    g = np.zeros((3, 128, 512), np.float32)
    for c in range(3):
        for i in range(4):
            for half in range(2):
                g[c, c * 8 + 2 * i + half, i * 128 + half * 64:i * 128 + (half + 1) * 64] = 1.0
    return jnp.asarray(g, BF16)


def _block_ones(n, w):
    r = np.arange(n) // w
    return jnp.asarray((r[:, None] == r[None, :]).astype(np.float32), BF16)


def _rope_tables(pos):
    half = HEAD_DIM // 2
    inv = jnp.power(ROPE_THETA, -jnp.arange(half, dtype=F32) / half)
    ang = pos.astype(F32)[:, None] * inv[None, :]
    c, s = jnp.cos(ang), jnp.sin(ang)
    return jnp.concatenate([c, c, c, c], -1), jnp.concatenate([-s, s, -s, s], -1)


def _sel_sum_matrix(nch, nsp, nc):
    n = np.arange(nch)[:, None]
    j = np.arange(nsp)[None, :]
    step = SEL_LEN // CMP_STRIDE
    lo = 1 - CMP_LEN // CMP_STRIDE
    m = (n >= step * j + lo) & (n <= step * j + step - 1) & (n < nc)
    return jnp.asarray(m.astype(np.float32), BF16)


def _chunk_perm():
    p = np.zeros((256, 256), np.float32)
    for l in range(16):
        for c in range(16):
            p[l * 16 + c, c * 16 + l] = 1.0
    return jnp.asarray(p, BF16)


def _compress_weights(cmp_pe, w_cmp):
    wl = jnp.zeros((16, 4, HEAD_DIM, 2, 4, HEAD_DIM), F32)
    for part in range(4):
        for half in range(2):
            wl = wl.at[:, part, :, half, part, :].set(w_cmp[part // 2, half * 16:(half + 1) * 16])
    wl = wl.reshape(16, 256, 512).astype(BF16)
    pe1 = jnp.concatenate([cmp_pe[0, 0:16]] * 2 + [cmp_pe[1, 0:16]] * 2, -1)
    pe2 = jnp.concatenate([cmp_pe[0, 16:32]] * 2 + [cmp_pe[1, 16:32]] * 2, -1)
    h1, l1 = _split_bf16(pe1)
    h2, l2 = _split_bf16(pe2)
    pe_rows = jnp.zeros((16, 16, 256), BF16)
    pe_rows = pe_rows.at[:, 0].set(h1).at[:, 1].set(h2).at[:, 2].set(l1).at[:, 3].set(l2)
    return wl, pe_rows


def _sb_ucat():
    def table(n):
        j = np.arange(n)[:, None]
        s = np.arange(n)[None, :]
        return jnp.asarray(np.concatenate([(j > s).astype(np.float32), np.ones((n, PAGE), np.float32)], 1), BF16)
    return table(PAGE), table(2 * PAGE)


def _pad_rows(a, n):
    return jnp.pad(a, ((0, 0), (0, n - a.shape[1]), (0, 0)))


def _even_layer(xp, xs, i, g0, page_table, cache_nsa, cache_nsa_win, w_in_e, g_qk_nsa, cmp_pe, w_cmp,
                gm_ws, gm_b, gm_ln, w_out_e, past):
    sp = xp.shape[0]
    nseq = page_table.shape[0]
    ts = xs.shape[0] // nseq
    qperm, gperm = _q_perm(), _gate_perm()
    w = w_in_e[i]
    wq = w[:, 0:512][:, qperm]
    wg = jnp.pad(w[:, 1280:1304][:, gperm], ((0, 0), (0, 104)))
    w_all = jnp.concatenate([wq, w[:, 512:1280], wg, w[:, 1304:2328]], -1).astype(BF16)
    gq = jnp.tile(g_qk_nsa[i, 0], 8)[None]
    gk = jnp.stack([jnp.tile(g_qk_nsa[i, r], 2) for r in (1, 2, 3)])
    ones_bd = _block_ones(512, HEAD_DIM)
    gexp = _gate_expand_mats()
    w_out = jnp.concatenate([w_out_e[i][0:512][qperm], w_out_e[i][512:1024]], 0).astype(BF16)
    wl, pe_rows = _compress_weights(cmp_pe[i], w_cmp[i])
    perm = _chunk_perm()

    def inproj(x, pos, tm, rchunk):
        cos_t, sin_t = _rope_tables(pos)
        ws_t = jnp.tril(gm_ws[i])[:, :rchunk, :rchunk]
        eye = jnp.eye(tm // rchunk, dtype=F32)
        wsg = jnp.einsum("ab,gts->gatbs", eye, ws_t).reshape(GM_GROUPS, tm, tm).astype(BF16)
        sbt = jnp.tile(jnp.repeat(gm_b[i][:, :rchunk].T, GM_WIDTH // GM_GROUPS, axis=1), (tm // rchunk, 1))
        return _inproj_even(x, g0, w_all, cos_t, sin_t, gq, gk, gm_ln[i], ones_bd, wsg, sbt, tm)

    tqp = 128
    qp, qr, rows, win, gates, v, ogm = inproj(xp, jnp.arange(sp), 256, CHUNK)
    npp = sp // PAGE
    pt_p = jnp.arange(npp, dtype=jnp.int32)[None]
    pool_p = rows.reshape(npp, PAGE, 512)
    cmp_p = _compress(pool_p, None, pt_p, perm, wl, pe_rows)
    nc = (sp - CMP_LEN) // CMP_STRIDE + 1
    ns = -(-sp // SEL_LEN)
    nsp = -(-ns // LANES) * LANES
    assert nc + 1 <= cmp_p.shape[1]
    msel = _sel_sum_matrix(cmp_p.shape[1], nsp, nc)
    o, sel = _cmp_attn(qp, cmp_p, gates, msel, gexp[0], nseq=1, tq=tqp, pos0=0, nc=nc, ns=ns,
                       k_eff=min(N_SEL, ns))
    nwb = WINDOW // tqp
    kspecs = [pl.BlockSpec((tqp, 256), (lambda s, b, j=j: (jnp.maximum(b - nwb + j, 0), 0)))
              for j in range(nwb + 1)]
    kblocks = [((j - nwb) * tqp, tqp, tqp) for j in range(nwb + 1)]
    o = _win_attn(qr, gates, gexp[2], o, [win] * (nwb + 1), kspecs, kblocks, nseq=1, tq=tqp, pos0=0)
    o = _sel_attn(qr, sel, gates, gexp[1], o, pool_p, pool_p, None, pt_p, nseq=1, tq=tqp, pos0=0, c0=0, c1=1,
                  npg=8)
    xp = _matmul_res([o, ogm], w_out, xp, 512)
    outs_p = (rows, win[sp - min(WINDOW, sp):], v[((sp - 1) // CHUNK) * CHUNK:])

    pos_s = jnp.tile(past + jnp.arange(ts), nseq)
    qp, qr, rows, win, gates, v, ogm = inproj(xs, pos_s, nseq * ts, ts)
    pool_s = jnp.transpose(cache_nsa, (0, 1, 3, 4, 5, 2))
    cmp_s = _compress(pool_s, i, page_table, perm, wl, pe_rows)
    ltot = past + ts
    nc = (ltot - CMP_LEN) // CMP_STRIDE + 1
    ns = -(-ltot // SEL_LEN)
    nsp = -(-ns // LANES) * LANES
    assert nc + 1 <= cmp_s.shape[1] and ts <= PAGE and past % PAGE == 0
    msel = _sel_sum_matrix(cmp_s.shape[1], nsp, nc)
    o, sel = _cmp_attn(qp, cmp_s, gates, msel, gexp[0], nseq=nseq, tq=ts, pos0=past, nc=nc, ns=ns,
                       k_eff=min(N_SEL, ns))
    wb = cache_nsa_win.shape[2]
    win_old = cache_nsa_win[i].reshape(nseq, wb, 256)
    win_new = _pad_rows(win.reshape(nseq, ts, 256), PAGE)
    kspecs = [pl.BlockSpec((None, wb, 256), lambda s, b: (s, 0, 0)),
              pl.BlockSpec((None, PAGE, 256), lambda s, b: (s, 0, 0))]
    kblocks = [(past - wb, 0, wb), (past, 0, PAGE)]
    o = _win_attn(qr, gates, gexp[2], o, [win_old, win_new], kspecs, kblocks, nseq=nseq, tq=ts, pos0=past)
    tail_s = _pad_rows(rows.reshape(nseq, ts, 512), PAGE)
    o = _sel_attn(qr, sel, gates, gexp[1], o, tail_s, pool_s, i, page_table, nseq=nseq, tq=ts, pos0=past,
                  c0=past // PAGE, c1=0, npg=next(n for n in (32, 16, 8) if (past // PAGE) % n == 0))
    xs = _matmul_res([o, ogm], w_out, xs, nseq * ts)
    win_s = jnp.concatenate([win_old, win.reshape(nseq, ts, 256)], 1)[:, -wb:]
    outs_s = (rows, win_s, v)
    return xp, xs, outs_p, outs_s


def _odd_layer(xp, xs, i, g0, page_table, cache_sb, state_conv, w_in_o, cv_dw, cv_b, cv_ln, w_out_o, past):
    sp = xp.shape[0]
    nseq = page_table.shape[0]
    ts = xs.shape[0] // nseq
    w_in = w_in_o[i].astype(BF16)
    w_out = w_out_o[i].astype(BF16)
    dw = jnp.pad(cv_dw[i], ((0, 1), (0, 0)))
    aux = jnp.concatenate([cv_b[i][None], cv_ln[i], jnp.zeros((5, CV_WIDTH), F32)], 0)
    ucat = _sb_ucat()

    glu, q, rows = _inproj_odd(xp, g0, w_in, 256)
    o_cv = _conv_module(glu[None], glu[None], dw, aux, tm=256, zero_first=True)[0]
    npp = sp // PAGE
    pt_p = jnp.arange(npp, dtype=jnp.int32)[None]
    pool_p = rows.reshape(npp, PAGE, 2 * SB_WIDTH)
    o_sb = _sb_attn(q, pool_p, pool_p, None, pt_p, ucat, nseq=1, tq=PAGE, pos0=0, c0=0, c1=1)
    xp = _matmul_res([o_cv, o_sb], w_out, xp, 512)
    outs_p = (rows, glu[sp - (CV_K - 1):])

    glu, q, rows = _inproj_odd(xs, g0, w_in, nseq * ts)
    glu3 = glu.reshape(nseq, ts, CV_WIDTH)
    prev = jnp.pad(state_conv[i], ((0, 0), (32 - (CV_K - 1), 0), (0, 0)))
    o_cv = _conv_module(prev, glu3, dw, aux, tm=ts, zero_first=False).reshape(nseq * ts, CV_WIDTH)
    pool_s = jnp.transpose(cache_sb, (0, 1, 3, 4, 5, 2))
    tail_s = _pad_rows(rows.reshape(nseq, ts, 2 * SB_WIDTH), PAGE)
    o_sb = _sb_attn(q, tail_s, pool_s, i, page_table, ucat, nseq=nseq, tq=ts, pos0=past, c0=past // PAGE, c1=0)
    xs = _matmul_res([o_cv, o_sb], w_out, xs, nseq * ts)
    conv_s = jnp.concatenate([state_conv[i], glu3], 1)[:, -(CV_K - 1):]
    outs_s = (rows, conv_s)
    return xp, xs, outs_p, outs_s


def kernel(x_prompt, x_sample, mem_prompt, page_table, cache_nsa, cache_nsa_win, cache_sb, state_conv, state_ffn, cache_mem, g_norm, w_in_e, g_qk_nsa, cmp_pe, w_cmp, gm_ws, gm_b, gm_ln, w_out_e, w_in_o, cv_dw, cv_b, cv_ln, w_out_o, w_xq, w_xkv, g_xqk, w_xo, w_up, ffn_dw, ffn_db, w_down):
    bp, sp, _ = x_prompt.shape
    nseq, ts, _ = x_sample.shape
    assert bp == 1
    depth = g_norm.shape[0]
    past = page_table.shape[1] * PAGE
    mlen = mem_prompt.shape[1]
    xp = x_prompt.reshape(sp, D_MODEL)
    xs = x_sample.reshape(nseq * ts, D_MODEL)
    o = {k: [] for k in ("nsa_rows_p", "nsa_rows_s", "nsa_win_p", "nsa_win_s", "gm_v_p", "gm_v_s",
                         "sb_rows_p", "sb_rows_s", "conv_p", "conv_s", "ffn_p", "ffn_s", "memkv_p")}
    for l in range(depth):
        i = l // 2
        g0 = g_norm[l, 0][None]
        if l % 2 == 0:
            xp, xs, (rows_p, win_p, v_p), (rows_s, win_s, v_s) = _even_layer(
                xp, xs, i, g0, page_table, cache_nsa, cache_nsa_win, w_in_e, g_qk_nsa, cmp_pe, w_cmp,
                gm_ws, gm_b, gm_ln, w_out_e, past)
            o["nsa_rows_p"].append(rows_p.reshape(1, sp, 4, NSA_KV, HEAD_DIM))
            o["nsa_rows_s"].append(rows_s.reshape(nseq, ts, 4, NSA_KV, HEAD_DIM))
            o["nsa_win_p"].append(win_p.reshape(1, -1, 2, NSA_KV, HEAD_DIM))
            o["nsa_win_s"].append(win_s.reshape(nseq, -1, 2, NSA_KV, HEAD_DIM))
            o["gm_v_p"].append(v_p[None])
            o["gm_v_s"].append(v_s.reshape(nseq, ts, GM_WIDTH))
        else:
            xp, xs, (rows_p, conv_p), (rows_s, conv_s) = _odd_layer(
                xp, xs, i, g0, page_table, cache_sb, state_conv, w_in_o, cv_dw, cv_b, cv_ln, w_out_o, past)
            o["sb_rows_p"].append(rows_p.reshape(1, sp, 2, SB_HEADS, HEAD_DIM))
            o["sb_rows_s"].append(rows_s.reshape(nseq, ts, 2, SB_HEADS, HEAD_DIM))
            o["conv_p"].append(conv_p[None])
            o["conv_s"].append(conv_s)

        w_xq_b = w_xq[l].astype(BF16)
        w_xo_b = w_xo[l].astype(BF16)
        g_xk = g_xqk[l, 1][None]
        g_xq = g_xqk[l, 0][None]
        mkv = _rms_matmul(mem_prompt.reshape(mlen, D_MODEL), g_norm[l, 3][None], w_xkv[l].astype(BF16),
                          mlen, 2 * D_MODEL, D_MODEL, g_xk)
        o["memkv_p"].append(mkv.reshape(1, mlen, 2, MEM_HEADS, MEM_HD))
        g1 = g_norm[l, 1][None]
        xp = _cross_block(xp, g1, w_xq_b, g_xq, mkv, w_xo_b, 512)
        qx = _rms_matmul(xs, g1, w_xq_b, nseq * ts, D_MODEL, D_MODEL, g_xq)
        xs = _matmul_res([_cross_attn(qx, cache_mem, l, nseq=nseq, tq=ts)], w_xo_b, xs, nseq * ts)

        g2 = g_norm[l, 2][None]
        w_up_b = w_up[l].astype(BF16)
        w_down_b = w_down[l].astype(BF16)
        dwb = jnp.concatenate([ffn_dw[l], ffn_db[l][None], jnp.zeros((4, 2 * D_FF), F32)], 0)
        xp, sta, stg = _convffn(xp, g2, w_up_b, dwb, w_down_b, 512, D_FF // 2)
        o["ffn_p"].append(jnp.concatenate([sta[-1, 8 - (FFN_K - 1):], stg[-1, 8 - (FFN_K - 1):]], -1)[None])
        up = _rms_matmul(xs, g2, w_up_b, nseq * ts, D_FF // 2)
        full = jnp.concatenate([state_ffn[l], up.reshape(nseq, ts, 2 * D_FF)], 1)
        taps = [full[:, k:k + ts].reshape(nseq * ts, 2 * D_FF) for k in range(FFN_K)]
        xs = _convgate_down(taps[0], taps[1], taps[2], dwb, w_down_b, xs, D_FF // 2)
        o["ffn_s"].append(full[:, -(FFN_K - 1):])

    return (xp.reshape(1, sp, D_MODEL), xs.reshape(nseq, ts, D_MODEL),
            jnp.stack(o["nsa_rows_p"]), jnp.stack(o["nsa_rows_s"]),
            jnp.stack(o["nsa_win_p"]), jnp.stack(o["nsa_win_s"]),
            jnp.stack(o["gm_v_p"]), jnp.stack(o["gm_v_s"]),
            jnp.stack(o["sb_rows_p"]), jnp.stack(o["sb_rows_s"]),
            jnp.stack(o["conv_p"]), jnp.stack(o["conv_s"]),
            jnp.stack(o["ffn_p"]), jnp.stack(o["ffn_s"]),
            jnp.stack(o["memkv_p"]))
```
